```python
import jax, jax.numpy as jnp
from jax import lax
import numpy as np

D_MODEL = 1024
BATCH = 2
SEQ = 8192
DEPTH = 1

CHUNK = 64
POOL_GROUPS = 4
POOL_WINDOWS = (2, 4, 8, 16)
POOL_WIDTH = D_MODEL // 2
POOL_GROUP_DIM = POOL_WIDTH // POOL_GROUPS
SB_HEAD_DIM = 64
SB_HEADS = D_MODEL // 128
SB_WIDTH = SB_HEADS * SB_HEAD_DIM
SB_BLOCK = 128
MEM_LEN = 256
X_HEADS = 4
X_HEAD_DIM = D_MODEL // 16
X_WIDTH = X_HEADS * X_HEAD_DIM
N_BRANCHES = 3
D_FF = -(-8 * D_MODEL // (3 * 256)) * 256
RMS_EPS = 1e-6
IN_WIDTH = POOL_WIDTH + 3 * SB_WIDTH + X_WIDTH + N_BRANCHES * D_MODEL

kernel_name = "hybrid_pool_stickbreak_memx_block"


def rmsnorm(x, g):
    xf = x.astype(jnp.float32)
    y = xf * lax.rsqrt(jnp.mean(xf * xf, axis=-1, keepdims=True) + RMS_EPS) * g.astype(jnp.float32)
    return y.astype(x.dtype)


def pool_mixer(p, w_mix, scale):
    b, s, _ = p.shape
    pg = p.astype(jnp.float32).reshape(b, s, POOL_GROUPS, POOL_GROUP_DIM)
    c = jnp.concatenate([jnp.zeros((b, 1, POOL_GROUPS, POOL_GROUP_DIM), jnp.float32),
                         jnp.cumsum(pg, axis=1)], axis=1)
    windows = jnp.array(POOL_WINDOWS, jnp.int32)
    t1 = jnp.arange(1, s + 1, dtype=jnp.int32)[:, None]
    lo = jnp.maximum(t1 - windows[None, :], 0)
    g_idx = jnp.arange(POOL_GROUPS, dtype=jnp.int32)[None, :]
    c_lo = c[:, lo, g_idx]
    count = jnp.minimum(t1, windows[None, :]).astype(jnp.float32)
    mixed = (c[:, 1:] - c_lo) / count[None, :, :, None] - pg
    mixed = mixed.astype(p.dtype)
    y = jnp.einsum('bsgc,gcd->bsgd', mixed, w_mix)
    return y.reshape(b, s, POOL_WIDTH) * scale


def stick_breaking_attention(q, k, v):
    b, s, h, dh = q.shape
    nb = s // SB_BLOCK
    scale = 1.0 / float(np.sqrt(dh))
    kf = k.astype(jnp.float32).transpose(0, 2, 1, 3)
    vf = v.astype(jnp.float32).transpose(0, 2, 1, 3)
    qb = q.astype(jnp.float32).transpose(0, 2, 1, 3).reshape(b, h, nb, SB_BLOCK, dh)
    qb = qb.transpose(2, 0, 1, 3, 4)
    key_pos = jnp.arange(s, dtype=jnp.int32)[None, :]

    def one_block(args):
        q_blk, bi = args
        z = jnp.einsum('bhqd,bhkd->bhqk', q_blk, kf) * scale
        qpos = bi * SB_BLOCK + jnp.arange(SB_BLOCK, dtype=jnp.int32)[:, None]
        mask = key_pos < qpos
        log1m = jnp.where(mask, jax.nn.log_sigmoid(-z), 0.0)
        excl = lax.cumsum(log1m, axis=3, reverse=True) - log1m
        a = jnp.where(mask, jnp.exp(jax.nn.log_sigmoid(z) + excl), 0.0)
        return jnp.einsum('bhqk,bhkd->bhqd', a, vf)

    out = lax.map(one_block, (qb, jnp.arange(nb, dtype=jnp.int32)))
    out = out.transpose(1, 0, 3, 2, 4).reshape(b, s, h * dh)
    return out.astype(q.dtype)


def memory_cross_attention(q, mem_n, w_mem_kv):
    b, s = q.shape[:2]
    m = mem_n.shape[1]
    kv = (mem_n @ w_mem_kv).reshape(b, m, 2, X_HEADS, X_HEAD_DIM)
    mk, mv = kv[:, :, 0], kv[:, :, 1]
    scores = jnp.einsum('bshd,bmhd->bhsm', q.astype(jnp.float32), mk.astype(jnp.float32))
    p = jax.nn.softmax(scores / float(np.sqrt(X_HEAD_DIM)), axis=-1)
    out = jnp.einsum('bhsm,bmhd->bshd', p, mv.astype(jnp.float32))
    return out.reshape(b, s, X_WIDTH).astype(q.dtype)


def setup_inputs(seed: int = 0) -> dict:
    key = jax.random.key(seed)
    ks = jax.random.split(key, 20)
    f32 = jnp.float32

    def nrm(k, shape, fan_in):
        return jax.random.normal(k, shape, f32) * (fan_in ** -0.5)

    def gain(k, shape):
        return 1.0 + 0.02 * jax.random.normal(k, shape, f32)

    L = DEPTH
    return {
        "x": jax.random.normal(ks[0], (BATCH, SEQ, D_MODEL), f32),
        "mem": jax.random.normal(ks[1], (BATCH, MEM_LEN, D_MODEL), f32),
        "norm_mix_pre": gain(ks[2], (L, D_MODEL)),
        "w_in": nrm(ks[3], (L, D_MODEL, IN_WIDTH), D_MODEL),
        "w_pool_mix": nrm(ks[4], (L, POOL_GROUPS, POOL_GROUP_DIM, POOL_GROUP_DIM), POOL_GROUP_DIM),
        "pool_scale": gain(ks[5], (L, POOL_WIDTH)),
        "w_pool_o": nrm(ks[6], (L, POOL_WIDTH, D_MODEL), POOL_WIDTH),
        "w_sb_o": nrm(ks[7], (L, SB_WIDTH, D_MODEL), SB_WIDTH),
        "norm_mem": gain(ks[8], (L, D_MODEL)),
        "w_mem_kv": nrm(ks[9], (L, D_MODEL, 2 * X_WIDTH), D_MODEL),
        "w_x_o": nrm(ks[10], (L, X_WIDTH, D_MODEL), X_WIDTH),
        "w_out": nrm(ks[11], (L, D_MODEL, D_MODEL), D_MODEL),
        "norm_mix_post": gain(ks[12], (L, D_MODEL)),
        "norm_ffn_pre": gain(ks[13], (L, D_MODEL)),
        "w_ffn_in": nrm(ks[14], (L, D_MODEL, 2 * D_FF), D_MODEL),
        "w_ffn_out": nrm(ks[15], (L, D_FF, D_MODEL), D_FF),
        "norm_ffn_post": gain(ks[16], (L, D_MODEL)),
    }


def reference(x, mem, norm_mix_pre, w_in, w_pool_mix, pool_scale, w_pool_o, w_sb_o,
              norm_mem, w_mem_kv, w_x_o, w_out, norm_mix_post, norm_ffn_pre,
              w_ffn_in, w_ffn_out, norm_ffn_post):
    b, s, d = x.shape
    h = x
    for l in range(DEPTH):
        n = rmsnorm(h, norm_mix_pre[l])
        proj = n @ w_in[l]
        o0 = POOL_WIDTH
        o1 = o0 + 3 * SB_WIDTH
        o2 = o1 + X_WIDTH
        p_in = proj[..., :o0]
        qkv = proj[..., o0:o1].reshape(b, s, 3, SB_HEADS, SB_HEAD_DIM)
        xq = proj[..., o1:o2].reshape(b, s, X_HEADS, X_HEAD_DIM)
        gates = jax.nn.sigmoid(proj[..., o2:].astype(jnp.float32)).reshape(b, s, N_BRANCHES, d)

        y_pool = pool_mixer(p_in, w_pool_mix[l], pool_scale[l]) @ w_pool_o[l]
        y_sb = stick_breaking_attention(qkv[:, :, 0], qkv[:, :, 1], qkv[:, :, 2]) @ w_sb_o[l]
        mem_n = rmsnorm(mem, norm_mem[l])
        y_x = memory_cross_attention(xq, mem_n, w_mem_kv[l]) @ w_x_o[l]

        merged = (gates[:, :, 0] * y_pool.astype(jnp.float32)
                  + gates[:, :, 1] * y_sb.astype(jnp.float32)
                  + gates[:, :, 2] * y_x.astype(jnp.float32)).astype(h.dtype)
        h = h + rmsnorm(merged @ w_out[l], norm_mix_post[l])

        n2 = rmsnorm(h, norm_ffn_pre[l])
        gu = n2 @ w_ffn_in[l]
        ff = (jax.nn.silu(gu[..., :D_FF]) * gu[..., D_FF:]) @ w_ffn_out[l]
        h = h + rmsnorm(ff, norm_ffn_post[l])
    return h
```

```python
import functools

import jax
import jax.numpy as jnp
from jax import lax
from jax.experimental import pallas as pl
from jax.experimental.pallas import tpu as pltpu

F32 = jnp.float32
BF16 = jnp.bfloat16

RMS_EPS = 1e-6
POOL_WINDOWS = (2, 4, 8, 16)
POOL_HALO = 16
LANES = 128
HEAD_DIM = 64
SB_BLOCK = 128
SB_EXIT_LOG = -104.0
VMEM_LIMIT = 56 * 1024 * 1024


def _rmsnorm(x, g):
    ms = jnp.mean(x * x, axis=-1, keepdims=True)
    return x * lax.rsqrt(ms + RMS_EPS) * g


def _dot(a, b):
    return jnp.dot(a, b, preferred_element_type=F32)


def _dot_nt(a, b):
    return lax.dot_general(a, b, (((1,), (1,)), ((), ())), preferred_element_type=F32)


def _const_spec(shape):
    nd = len(shape)
    return pl.BlockSpec(shape, lambda *_: (0,) * nd, pipeline_mode=pl.Buffered(1))


def _mem_kv_kernel(mem_ref, g_ref, w_ref, o_ref):
    n = _rmsnorm(mem_ref[0], g_ref[...])
    o_ref[0] = _dot(n.astype(BF16), w_ref[...]).astype(BF16)


def _mem_kv(mem, g, w):
    b, m, d = mem.shape
    n_out = w.shape[1]
    return pl.pallas_call(
        _mem_kv_kernel,
        grid=(b,),
        in_specs=[pl.BlockSpec((1, m, d), lambda i: (i, 0, 0)),
                  _const_spec((1, d)),
                  _const_spec((d, n_out))],
        out_specs=pl.BlockSpec((1, m, n_out), lambda i: (i, 0, 0)),
        out_shape=jax.ShapeDtypeStruct((b, m, n_out), BF16),
        compiler_params=pltpu.CompilerParams(dimension_semantics=("arbitrary",)),
        name="mem_kv",
    )(mem, g, w)


def _in_proj_kernel(x_ref, g_ref, w_ref, p_ref, qkv_ref, xq_ref, *, pool_w, sb_w):
    nb = _rmsnorm(x_ref[...], g_ref[...]).astype(BF16)
    scale = 1.0 / (HEAD_DIM ** 0.5)
    o0, o1, o2 = pool_w, pool_w + sb_w, pool_w + 3 * sb_w
    p_ref[...] = _dot(nb, w_ref[:, :o0]).astype(BF16)
    qkv_ref[:, :sb_w] = (_dot(nb, w_ref[:, o0:o1]) * scale).astype(BF16)
    qkv_ref[:, sb_w:] = _dot(nb, w_ref[:, o1:o2]).astype(BF16)
    xq_ref[...] = (_dot(nb, w_ref[:, o2:]) * scale).astype(BF16)


def _in_proj(x2, g, w, *, tm, pool_w, sb_w, x_w):
    t, d = x2.shape
    n_in = w.shape[1]
    kern = functools.partial(_in_proj_kernel, pool_w=pool_w, sb_w=sb_w)
    return pl.pallas_call(
        kern,
        grid=(t // tm,),
        in_specs=[pl.BlockSpec((tm, d), lambda i: (i, 0)),
                  _const_spec((1, d)),
                  _const_spec((d, n_in))],
        out_specs=[pl.BlockSpec((tm, pool_w), lambda i: (i, 0)),
                   pl.BlockSpec((tm, 3 * sb_w), lambda i: (i, 0)),
                   pl.BlockSpec((tm, x_w), lambda i: (i, 0))],
        out_shape=[jax.ShapeDtypeStruct((t, pool_w), BF16),
                   jax.ShapeDtypeStruct((t, 3 * sb_w), BF16),
                   jax.ShapeDtypeStruct((t, x_w), BF16)],
        compiler_params=pltpu.CompilerParams(dimension_semantics=("arbitrary",),
                                             vmem_limit_bytes=VMEM_LIMIT),
        name="in_proj",
    )(x2, g, w)


def _sb_kernel(q_ref, k_ref, v_ref, cm_ref, o_ref, qm_ref, acc_ref, carry_ref, *, heads):
    i = pl.program_id(1)
    blk = SB_BLOCK
    row = lax.broadcasted_iota(jnp.int32, (blk, blk), 0)
    col = lax.broadcasted_iota(jnp.int32, (blk, blk), 1)
    tri = col < row
    lo_half = col < HEAD_DIM

    for h in range(heads):
        p = h // 2
        qp = q_ref[0, :, p * LANES:(p + 1) * LANES]
        keep = lo_half if h % 2 == 0 else jnp.logical_not(lo_half)
        qm_ref[h] = jnp.where(keep, qp, jnp.zeros_like(qp))

    def visit(j, diag):
        start = pl.multiple_of(j * blk, blk)
        mx = None
        for h in range(heads):
            p = h // 2
            kp = k_ref[0, pl.ds(start, blk), p * LANES:(p + 1) * LANES]
            vp = v_ref[0, pl.ds(start, blk), p * LANES:(p + 1) * LANES]
            z = _dot_nt(qm_ref[h], kp)
            sp = jnp.maximum(z, 0.0) + jnp.log(1.0 + jnp.exp(-jnp.abs(z)))
            spm = jnp.where(tri, sp, 0.0) if diag else sp
            hi = spm.astype(BF16)
            lo = (spm - hi.astype(F32)).astype(BF16)
            r = _dot(jnp.concatenate([hi, lo], axis=1), cm_ref[...])
            excl = r[:, :blk]
            tot = r[:, blk:]
            if diag:
                a = jnp.where(tri, jnp.exp(z - sp + excl), 0.0)
                c_new = tot
                acc_ref[h] = _dot(a.astype(BF16), vp)
            else:
                c = carry_ref[h]
                a = jnp.exp(z - sp + excl + c)
                c_new = c + tot
                acc_ref[h] += _dot(a.astype(BF16), vp)
            carry_ref[h] = c_new
            mx = c_new if mx is None else jnp.maximum(mx, c_new)
        return jnp.max(mx)

    m0 = visit(i, True)

    def cond(st):
        j, m = st
        return jnp.logical_and(j >= 0, m > SB_EXIT_LOG)

    def body(st):
        j, _ = st
        return j - 1, visit(j, False)

    lax.while_loop(cond, body, (i - 1, m0))

    for p in range(heads // 2):
        o = jnp.where(lo_half, acc_ref[2 * p], acc_ref[2 * p + 1])
        o_ref[0, :, p * LANES:(p + 1) * LANES] = o.astype(BF16)


def _sb_attn(qkv, cm, *, heads):
    b, s, w3 = qkv.shape
    w = w3 // 3
    blk = SB_BLOCK
    kern = functools.partial(_sb_kernel, heads=heads)
    return pl.pallas_call(
        kern,
        grid=(b, s // blk),
        in_specs=[pl.BlockSpec((1, blk, w), lambda bi, i: (bi, i, 0)),
                  pl.BlockSpec((1, s, w), lambda bi, i: (bi, 0, 1), pipeline_mode=pl.Buffered(1)),
                  pl.BlockSpec((1, s, w), lambda bi, i: (bi, 0, 2), pipeline_mode=pl.Buffered(1)),
                  _const_spec((2 * blk, 2 * blk))],
        out_specs=pl.BlockSpec((1, blk, w), lambda bi, i: (bi, i, 0)),
        out_shape=jax.ShapeDtypeStruct((b, s, w), BF16),
        scratch_shapes=[pltpu.VMEM((heads, blk, LANES), BF16),
                        pltpu.VMEM((heads, blk, LANES), F32),
                        pltpu.VMEM((heads, blk, LANES), F32)],
        compiler_params=pltpu.CompilerParams(dimension_semantics=("arbitrary", "arbitrary"),
                                             vmem_limit_bytes=VMEM_LIMIT),
        name="sb_attn",
    )(qkv, qkv, qkv, cm)


def _suffix_sum_matrix():
    blk = SB_BLOCK
    j = jnp.arange(2 * blk)[:, None] % blk
    s = jnp.arange(2 * blk)[None, :]
    m = jnp.where(s < blk, j > s, True)
    return jnp.where(m, -1.0, 0.0).astype(BF16)


def _mix_kernel(x_ref, pin_ref, halo_ref, ysb_ref, xq_ref, kv_ref, gpre_ref, wg_ref, wmix_ref,
                pscale_ref, wpo_ref, wsbo_ref, wxo_ref, wout_ref, gpost_ref, o_ref, ext_ref,
                *, seq, x_heads):
    t = pl.program_id(0)
    tm, d = x_ref.shape
    x = x_ref[...]
    nb = _rmsnorm(x, gpre_ref[...]).astype(BF16)

    tok0 = (t * tm) % seq
    halo = halo_ref[...].astype(F32)
    ext_ref[:POOL_HALO, :] = jnp.where(tok0 == 0, 0.0, halo)
    ext_ref[POOL_HALO:, :] = pin_ref[...].astype(F32)
    pos1 = tok0 + 1 + lax.broadcasted_iota(jnp.int32, (tm, 1), 0)
    groups = []
    for g, w in enumerate(POOL_WINDOWS):
        cs = slice(g * LANES, (g + 1) * LANES)
        cur = ext_ref[POOL_HALO:, cs]
        acc = cur
        for k in range(1, w):
            acc = acc + ext_ref[POOL_HALO - k:POOL_HALO - k + tm, cs]
        inv = 1.0 / jnp.minimum(pos1, w).astype(F32)
        mixed = acc * inv - cur
        groups.append(_dot(mixed.astype(BF16), wmix_ref[g]))
    ypool = jnp.concatenate(groups, axis=1) * pscale_ref[...]
    merged = jax.nn.sigmoid(_dot(nb, wg_ref[:, :d])) * _dot(ypool.astype(BF16), wpo_ref[...])

    merged += jax.nn.sigmoid(_dot(nb, wg_ref[:, d:2 * d])) * _dot(ysb_ref[...], wsbo_ref[...])

    m_len = kv_ref.shape[1]
    xw = x_heads * HEAD_DIM
    lane = lax.broadcasted_iota(jnp.int32, (tm, LANES), 1)
    lo_half = lane < HEAD_DIM
    ones = jnp.ones((m_len, LANES), BF16)
    pairs = []
    for p in range(x_heads // 2):
        cs = slice(p * LANES, (p + 1) * LANES)
        xq = xq_ref[:, cs]
        mk = kv_ref[0, :, cs]
        mv1 = jnp.concatenate([kv_ref[0, :, xw + p * LANES:xw + (p + 1) * LANES], ones], axis=1)
        outs = []
        for hh in range(2):
            keep = lo_half if hh == 0 else jnp.logical_not(lo_half)
            sc = _dot_nt(jnp.where(keep, xq, jnp.zeros_like(xq)), mk)
            e = jnp.exp(sc - jnp.max(sc, axis=-1, keepdims=True)).astype(BF16)
            r = _dot(e, mv1)
            outs.append(r[:, :LANES] / r[:, LANES:])
        pairs.append(jnp.where(lo_half, outs[0], outs[1]))
    yx = jnp.concatenate(pairs, axis=1).astype(BF16)
    merged += jax.nn.sigmoid(_dot(nb, wg_ref[:, 2 * d:])) * _dot(yx, wxo_ref[...])

    mo = _dot(merged.astype(BF16), wout_ref[...])
    o_ref[...] = x + _rmsnorm(mo, gpost_ref[...])


def _mix(x2, pin, ysb, xq, kv, gpre, wg, wmix, pscale, wpo, wsbo, wxo, wout, gpost, *, tm, seq,
         x_heads):
    t, d = x2.shape
    pw = pin.shape[1]
    hb = tm // POOL_HALO
    kern = functools.partial(_mix_kernel, seq=seq, x_heads=x_heads)
    return pl.pallas_call(
        kern,
        grid=(t // tm,),
        in_specs=[pl.BlockSpec((tm, d), lambda i: (i, 0)),
                  pl.BlockSpec((tm, pw), lambda i: (i, 0)),
                  pl.BlockSpec((POOL_HALO, pw), lambda i: (jnp.maximum(i * hb - 1, 0), 0)),
                  pl.BlockSpec((tm, ysb.shape[1]), lambda i: (i, 0)),
                  pl.BlockSpec((tm, xq.shape[1]), lambda i: (i, 0)),
                  pl.BlockSpec((1,) + kv.shape[1:], lambda i: ((i * tm) // seq, 0, 0)),
                  _const_spec(gpre.shape), _const_spec(wg.shape), _const_spec(wmix.shape),
                  _const_spec(pscale.shape), _const_spec(wpo.shape), _const_spec(wsbo.shape),
                  _const_spec(wxo.shape), _const_spec(wout.shape), _const_spec(gpost.shape)],
        out_specs=pl.BlockSpec((tm, d), lambda i: (i, 0)),
        out_shape=jax.ShapeDtypeStruct((t, d), F32),
        scratch_shapes=[pltpu.VMEM((tm + POOL_HALO, pw), F32)],
        compiler_params=pltpu.CompilerParams(dimension_semantics=("arbitrary",),
                                             vmem_limit_bytes=VMEM_LIMIT),
        name="mix",
    )(x2, pin, pin, ysb, xq, kv, gpre, wg, wmix, pscale, wpo, wsbo, wxo, wout, gpost)


def _ffn_kernel(h_ref, gpre_ref, win_ref, wout_ref, gpost_ref, o_ref, a_ref, *, d_ff, chunk):
    h = h_ref[...]
    nb = _rmsnorm(h, gpre_ref[...]).astype(BF16)
    for c in range(d_ff // chunk):
        g = _dot(nb, win_ref[:, c * chunk:(c + 1) * chunk])
        u = _dot(nb, win_ref[:, d_ff + c * chunk:d_ff + (c + 1) * chunk])
        a_ref[:, c * chunk:(c + 1) * chunk] = (g * jax.nn.sigmoid(g) * u).astype(BF16)
    ff = _dot(a_ref[...], wout_ref[...])
    o_ref[...] = h + _rmsnorm(ff, gpost_ref[...])


def _ffn(h, gpre, win, wout, gpost, *, tm, chunk):
    t, d = h.shape
    d_ff = wout.shape[0]
    kern = functools.partial(_ffn_kernel, d_ff=d_ff, chunk=chunk)
    return pl.pallas_call(
        kern,
        grid=(t // tm,),
        in_specs=[pl.BlockSpec((tm, d), lambda i: (i, 0)),
                  _const_spec(gpre.shape), _const_spec(win.shape), _const_spec(wout.shape),
                  _const_spec(gpost.shape)],
        out_specs=pl.BlockSpec((tm, d), lambda i: (i, 0)),
        out_shape=jax.ShapeDtypeStruct((t, d), F32),
        scratch_shapes=[pltpu.VMEM((tm, d_ff), BF16)],
        compiler_params=pltpu.CompilerParams(dimension_semantics=("arbitrary",),
                                             vmem_limit_bytes=VMEM_LIMIT),
        name="ffn",
    )(h, gpre, win, wout, gpost)


def kernel(x, mem, norm_mix_pre, w_in, w_pool_mix, pool_scale, w_pool_o, w_sb_o, norm_mem,
           w_mem_kv, w_x_o, w_out, norm_mix_post, norm_ffn_pre, w_ffn_in, w_ffn_out,
           norm_ffn_post):
    b, s, d = x.shape
    depth = w_in.shape[0]
    pool_w = w_pool_o.shape[1]
    sb_w = w_sb_o.shape[1]
    x_w = w_x_o.shape[1]
    sb_heads = sb_w // HEAD_DIM
    x_heads = x_w // HEAD_DIM
    split = pool_w + 3 * sb_w + x_w
    tm = 512
    cm = _suffix_sum_matrix()

    h = x.reshape(b * s, d)
    for l in range(depth):
        row = lambda v: v[l].reshape(1, -1)
        w_in_l = w_in[l].astype(BF16)
        kv = _mem_kv(mem, row(norm_mem), w_mem_kv[l].astype(BF16))
        pin, qkv, xq = _in_proj(h, row(norm_mix_pre), w_in_l[:, :split], tm=tm, pool_w=pool_w,
                                sb_w=sb_w, x_w=x_w)
        ysb = _sb_attn(qkv.reshape(b, s, 3 * sb_w), cm, heads=sb_heads).reshape(b * s, sb_w)
        h = _mix(h, pin, ysb, xq, kv, row(norm_mix_pre), w_in_l[:, split:],
                 w_pool_mix[l].astype(BF16), row(pool_scale), w_pool_o[l].astype(BF16),
                 w_sb_o[l].astype(BF16), w_x_o[l].astype(BF16), w_out[l].astype(BF16),
                 row(norm_mix_post), tm=tm, seq=s, x_heads=x_heads)
        h = _ffn(h, row(norm_ffn_pre), w_ffn_in[l].astype(BF16), w_ffn_out[l].astype(BF16),
                 row(norm_ffn_post), tm=tm, chunk=256)
    return h.reshape(b, s, d)
```

```python
import functools

import jax
import jax.numpy as jnp
from jax import lax
from jax.experimental import pallas as pl
from jax.experimental.pallas import tpu as pltpu

F32 = jnp.float32
BF16 = jnp.bfloat16

RMS_EPS = 1e-6
POOL_WINDOWS = (2, 4, 8, 16)
POOL_HALO = 16
LANES = 128
HEAD_DIM = 64
SB_BLOCK = 128
SB_SPAN = 3
SB_EXIT_LOG = -104.0
VMEM_LIMIT = 56 * 1024 * 1024


def _rmsnorm(x, g):
    ms = jnp.mean(x * x, axis=-1, keepdims=True)
    return x * lax.rsqrt(ms + RMS_EPS) * g


def _dot(a, b):
    return jnp.dot(a, b, preferred_element_type=F32)


def _dot_nt(a, b):
    return lax.dot_general(a, b, (((1,), (1,)), ((), ())), preferred_element_type=F32)


def _const_spec(shape):
    nd = len(shape)
    return pl.BlockSpec(shape, lambda *_: (0,) * nd, pipeline_mode=pl.Buffered(1))


def _mem_kv_kernel(mem_ref, g_ref, w_ref, o_ref):
    n = _rmsnorm(mem_ref[0], g_ref[...])
    o_ref[0] = _dot(n.astype(BF16), w_ref[...]).astype(BF16)


def _mem_kv(mem, g, w):
    b, m, d = mem.shape
    n_out = w.shape[1]
    return pl.pallas_call(
        _mem_kv_kernel,
        grid=(b,),
        in_specs=[pl.BlockSpec((1, m, d), lambda i: (i, 0, 0)),
                  _const_spec((1, d)),
                  _const_spec((d, n_out))],
        out_specs=pl.BlockSpec((1, m, n_out), lambda i: (i, 0, 0)),
        out_shape=jax.ShapeDtypeStruct((b, m, n_out), BF16),
        compiler_params=pltpu.CompilerParams(dimension_semantics=("arbitrary",)),
        name="mem_kv",
    )(mem, g, w)


def _in_proj_kernel(x_ref, g_ref, w_ref, p_ref, qkv_ref, xq_ref, *, pool_w, sb_w):
    nb = _rmsnorm(x_ref[...], g_ref[...]).astype(BF16)
    scale = 1.0 / (HEAD_DIM ** 0.5)
    o0, o1, o2 = pool_w, pool_w + sb_w, pool_w + 3 * sb_w
    p_ref[...] = _dot(nb, w_ref[:, :o0]).astype(BF16)
    qkv_ref[:, :sb_w] = (_dot(nb, w_ref[:, o0:o1]) * scale).astype(BF16)
    qkv_ref[:, sb_w:] = _dot(nb, w_ref[:, o1:o2]).astype(BF16)
    xq_ref[...] = (_dot(nb, w_ref[:, o2:]) * scale).astype(BF16)


def _in_proj(x2, g, w, *, tm, pool_w, sb_w, x_w):
    t, d = x2.shape
    n_in = w.shape[1]
    kern = functools.partial(_in_proj_kernel, pool_w=pool_w, sb_w=sb_w)
    return pl.pallas_call(
        kern,
        grid=(t // tm,),
        in_specs=[pl.BlockSpec((tm, d), lambda i: (i, 0)),
                  _const_spec((1, d)),
                  _const_spec((d, n_in))],
        out_specs=[pl.BlockSpec((tm, pool_w), lambda i: (i, 0)),
                   pl.BlockSpec((tm, 3 * sb_w), lambda i: (i, 0)),
                   pl.BlockSpec((tm, x_w), lambda i: (i, 0))],
        out_shape=[jax.ShapeDtypeStruct((t, pool_w), BF16),
                   jax.ShapeDtypeStruct((t, 3 * sb_w), BF16),
                   jax.ShapeDtypeStruct((t, x_w), BF16)],
        compiler_params=pltpu.CompilerParams(dimension_semantics=("arbitrary",),
                                             vmem_limit_bytes=VMEM_LIMIT),
        name="in_proj",
    )(x2, g, w)


def _sb_kernel(q_ref, k_ref, v_ref, cm_ref, o_ref, qm_ref, z_ref, cat_ref, lb_ref, a_ref,
               acc_ref, carry_ref, *, heads):
    i = pl.program_id(1)
    blk = SB_BLOCK
    npair = heads // 2
    row = lax.broadcasted_iota(jnp.int32, (blk, blk), 0)
    col = lax.broadcasted_iota(jnp.int32, (blk, blk), 1)
    tri = col < row
    lo_half = col < HEAD_DIM

    for p in range(npair):
        qp = q_ref[0, :, p * LANES:(p + 1) * LANES]
        zero = jnp.zeros_like(qp)
        qm_ref[p, :blk] = jnp.where(lo_half, qp, zero)
        qm_ref[p, blk:] = jnp.where(lo_half, zero, qp)

    def visit(first, nb, diag, fresh):
        start = pl.multiple_of(first * blk, blk)
        for p in range(npair):
            kp = k_ref[0, pl.ds(start, nb * blk), p * LANES:(p + 1) * LANES]
            z_ref[p, :, :nb * blk] = _dot_nt(qm_ref[p], kp)
        for h in range(heads):
            p, hh = divmod(h, 2)
            for b in range(nb):
                z = z_ref[p, hh * blk:(hh + 1) * blk, b * blk:(b + 1) * blk]
                sp = jnp.maximum(z, 0.0) + jnp.log(1.0 + jnp.exp(-jnp.abs(z)))
                spm = jnp.where(tri, sp, 0.0) if (diag and b == nb - 1) else sp
                hi = spm.astype(BF16)
                lo = (spm - hi.astype(F32)).astype(BF16)
                r0 = (h * SB_SPAN + b) * blk
                cat_ref[r0:r0 + blk, :blk] = hi
                cat_ref[r0:r0 + blk, blk:] = lo
                lb_ref[r0:r0 + blk, :] = z - sp
        mx = None
        for h in range(heads):
            r0 = h * SB_SPAN * blk
            r = _dot(cat_ref[r0:r0 + nb * blk, :], cm_ref[...])
            c = None if fresh else carry_ref[h]
            parts = [None] * nb
            for b in reversed(range(nb)):
                excl = r[b * blk:(b + 1) * blk, :blk]
                tot = r[b * blk:(b + 1) * blk, blk:]
                arg = lb_ref[r0 + b * blk:r0 + (b + 1) * blk, :] + excl
                if c is not None:
                    arg = arg + c
                a = jnp.exp(arg)
                if diag and b == nb - 1:
                    a = jnp.where(tri, a, 0.0)
                parts[b] = a.astype(BF16)
                c = tot if c is None else c + tot
            carry_ref[h] = c
            a_ref[h * blk:(h + 1) * blk, :nb * blk] = (
                parts[0] if nb == 1 else jnp.concatenate(parts, axis=1))
            mx = c if mx is None else jnp.maximum(mx, c)
        for p in range(npair):
            vp = v_ref[0, pl.ds(start, nb * blk), p * LANES:(p + 1) * LANES]
            res = _dot(a_ref[2 * p * blk:2 * (p + 1) * blk, :nb * blk], vp)
            for hh in range(2):
                if fresh:
                    acc_ref[2 * p + hh] = res[hh * blk:(hh + 1) * blk]
                else:
                    acc_ref[2 * p + hh] += res[hh * blk:(hh + 1) * blk]
        return jnp.max(mx)

    far = i >= SB_SPAN - 1
    m0 = lax.cond(far,
                  lambda: visit(i - (SB_SPAN - 1), SB_SPAN, True, True),
                  lambda: visit(i, 1, True, True))
    j0 = jnp.where(far, i - SB_SPAN, i - 1)

    def cond(st):
        j, m = st
        return jnp.logical_and(j >= 0, m > SB_EXIT_LOG)

    def body(st):
        j, _ = st
        return j - 1, visit(j, 1, False, False)

    lax.while_loop(cond, body, (j0, m0))

    for p in range(npair):
        o = jnp.where(lo_half, acc_ref[2 * p], acc_ref[2 * p + 1])
        o_ref[0, :, p * LANES:(p + 1) * LANES] = o.astype(BF16)


def _sb_attn(qkv, cm, *, heads):
    b, s, w3 = qkv.shape
    w = w3 // 3
    blk = SB_BLOCK
    kern = functools.partial(_sb_kernel, heads=heads)
    return pl.pallas_call(
        kern,
        grid=(b, s // blk),
        in_specs=[pl.BlockSpec((1, blk, w), lambda bi, i: (bi, i, 0)),
                  pl.BlockSpec((1, s, w), lambda bi, i: (bi, 0, 1), pipeline_mode=pl.Buffered(1)),
                  pl.BlockSpec((1, s, w), lambda bi, i: (bi, 0, 2), pipeline_mode=pl.Buffered(1)),
                  _const_spec((2 * blk, 2 * blk))],
        out_specs=pl.BlockSpec((1, blk, w), lambda bi, i: (bi, i, 0)),
        out_shape=jax.ShapeDtypeStruct((b, s, w), BF16),
        scratch_shapes=[pltpu.VMEM((heads // 2, 2 * blk, LANES), BF16),
                        pltpu.VMEM((heads // 2, 2 * blk, SB_SPAN * blk), F32),
                        pltpu.VMEM((heads * SB_SPAN * blk, 2 * blk), BF16),
                        pltpu.VMEM((heads * SB_SPAN * blk, blk), F32),
                        pltpu.VMEM((heads * blk, SB_SPAN * blk), BF16),
                        pltpu.VMEM((heads, blk, LANES), F32),
                        pltpu.VMEM((heads, blk, LANES), F32)],
        compiler_params=pltpu.CompilerParams(dimension_semantics=("arbitrary", "arbitrary"),
                                             vmem_limit_bytes=VMEM_LIMIT),
        name="sb_attn",
    )(qkv, qkv, qkv, cm)


def _suffix_sum_matrix():
    blk = SB_BLOCK
    j = jnp.arange(2 * blk)[:, None] % blk
    s = jnp.arange(2 * blk)[None, :]
    m = jnp.where(s < blk, j > s, True)
    return jnp.where(m, -1.0, 0.0).astype(BF16)


def _mix_kernel(x_ref, pin_ref, halo_ref, ysb_ref, xq_ref, kv_ref, gpre_ref, wg_ref, wmix_ref,
                pscale_ref, wpo_ref, wsbo_ref, wxo_ref, wout_ref, gpost_ref, o_ref, ext_ref,
                *, seq, x_heads):
    t = pl.program_id(0)
    tm, d = x_ref.shape
    x = x_ref[...]
    nb = _rmsnorm(x, gpre_ref[...]).astype(BF16)

    tok0 = (t * tm) % seq
    halo = halo_ref[...].astype(F32)
    ext_ref[:POOL_HALO, :] = jnp.where(tok0 == 0, 0.0, halo)
    ext_ref[POOL_HALO:, :] = pin_ref[...].astype(F32)
    pos1 = tok0 + 1 + lax.broadcasted_iota(jnp.int32, (tm, 1), 0)
    groups = []
    for g, w in enumerate(POOL_WINDOWS):
        cs = slice(g * LANES, (g + 1) * LANES)
        cur = ext_ref[POOL_HALO:, cs]
        acc = cur
        for k in range(1, w):
            acc = acc + ext_ref[POOL_HALO - k:POOL_HALO - k + tm, cs]
        inv = 1.0 / jnp.minimum(pos1, w).astype(F32)
        mixed = acc * inv - cur
        groups.append(_dot(mixed.astype(BF16), wmix_ref[g]))
    ypool = jnp.concatenate(groups, axis=1) * pscale_ref[...]
    merged = jax.nn.sigmoid(_dot(nb, wg_ref[:, :d])) * _dot(ypool.astype(BF16), wpo_ref[...])

    merged += jax.nn.sigmoid(_dot(nb, wg_ref[:, d:2 * d])) * _dot(ysb_ref[...], wsbo_ref[...])

    m_len = kv_ref.shape[1]
    xw = x_heads * HEAD_DIM
    lane = lax.broadcasted_iota(jnp.int32, (tm, LANES), 1)
    lo_half = lane < HEAD_DIM
    ones = jnp.ones((m_len, LANES), BF16)
    pairs = []
    for p in range(x_heads // 2):
        cs = slice(p * LANES, (p + 1) * LANES)
        xq = xq_ref[:, cs]
        mk = kv_ref[0, :, cs]
        mv1 = jnp.concatenate([kv_ref[0, :, xw + p * LANES:xw + (p + 1) * LANES], ones], axis=1)
        outs = []
        for hh in range(2):
            keep = lo_half if hh == 0 else jnp.logical_not(lo_half)
            sc = _dot_nt(jnp.where(keep, xq, jnp.zeros_like(xq)), mk)
            e = jnp.exp(sc - jnp.max(sc, axis=-1, keepdims=True)).astype(BF16)
            r = _dot(e, mv1)
            outs.append(r[:, :LANES] / r[:, LANES:])
        pairs.append(jnp.where(lo_half, outs[0], outs[1]))
    yx = jnp.concatenate(pairs, axis=1).astype(BF16)
    merged += jax.nn.sigmoid(_dot(nb, wg_ref[:, 2 * d:])) * _dot(yx, wxo_ref[...])

    mo = _dot(merged.astype(BF16), wout_ref[...])
    o_ref[...] = x + _rmsnorm(mo, gpost_ref[...])


def _mix(x2, pin, ysb, xq, kv, gpre, wg, wmix, pscale, wpo, wsbo, wxo, wout, gpost, *, tm, seq,
         x_heads):
    t, d = x2.shape
    pw = pin.shape[1]
    hb = tm // POOL_HALO
    kern = functools.partial(_mix_kernel, seq=seq, x_heads=x_heads)
    return pl.pallas_call(
        kern,
        grid=(t // tm,),
        in_specs=[pl.BlockSpec((tm, d), lambda i: (i, 0)),
                  pl.BlockSpec((tm, pw), lambda i: (i, 0)),
                  pl.BlockSpec((POOL_HALO, pw), lambda i: (jnp.maximum(i * hb - 1, 0), 0)),
                  pl.BlockSpec((tm, ysb.shape[1]), lambda i: (i, 0)),
                  pl.BlockSpec((tm, xq.shape[1]), lambda i: (i, 0)),
                  pl.BlockSpec((1,) + kv.shape[1:], lambda i: ((i * tm) // seq, 0, 0)),
                  _const_spec(gpre.shape), _const_spec(wg.shape), _const_spec(wmix.shape),
                  _const_spec(pscale.shape), _const_spec(wpo.shape), _const_spec(wsbo.shape),
                  _const_spec(wxo.shape), _const_spec(wout.shape), _const_spec(gpost.shape)],
        out_specs=pl.BlockSpec((tm, d), lambda i: (i, 0)),
        out_shape=jax.ShapeDtypeStruct((t, d), F32),
        scratch_shapes=[pltpu.VMEM((tm + POOL_HALO, pw), F32)],
        compiler_params=pltpu.CompilerParams(dimension_semantics=("arbitrary",),
                                             vmem_limit_bytes=VMEM_LIMIT),
        name="mix",
    )(x2, pin, pin, ysb, xq, kv, gpre, wg, wmix, pscale, wpo, wsbo, wxo, wout, gpost)


def _ffn_kernel(h_ref, gpre_ref, win_ref, wout_ref, gpost_ref, o_ref, a_ref, *, d_ff, chunk):
    h = h_ref[...]
    nb = _rmsnorm(h, gpre_ref[...]).astype(BF16)
    for c in range(d_ff // chunk):
        g = _dot(nb, win_ref[:, c * chunk:(c + 1) * chunk])
        u = _dot(nb, win_ref[:, d_ff + c * chunk:d_ff + (c + 1) * chunk])
        a_ref[:, c * chunk:(c + 1) * chunk] = (g * jax.nn.sigmoid(g) * u).astype(BF16)
    ff = _dot(a_ref[...], wout_ref[...])
    o_ref[...] = h + _rmsnorm(ff, gpost_ref[...])


def _ffn(h, gpre, win, wout, gpost, *, tm, chunk):
    t, d = h.shape
    d_ff = wout.shape[0]
    kern = functools.partial(_ffn_kernel, d_ff=d_ff, chunk=chunk)
    return pl.pallas_call(
        kern,
        grid=(t // tm,),
        in_specs=[pl.BlockSpec((tm, d), lambda i: (i, 0)),
                  _const_spec(gpre.shape), _const_spec(win.shape), _const_spec(wout.shape),
                  _const_spec(gpost.shape)],
        out_specs=pl.BlockSpec((tm, d), lambda i: (i, 0)),
        out_shape=jax.ShapeDtypeStruct((t, d), F32),
        scratch_shapes=[pltpu.VMEM((tm, d_ff), BF16)],
        compiler_params=pltpu.CompilerParams(dimension_semantics=("arbitrary",),
                                             vmem_limit_bytes=VMEM_LIMIT),
        name="ffn",
    )(h, gpre, win, wout, gpost)


def kernel(x, mem, norm_mix_pre, w_in, w_pool_mix, pool_scale, w_pool_o, w_sb_o, norm_mem,
           w_mem_kv, w_x_o, w_out, norm_mix_post, norm_ffn_pre, w_ffn_in, w_ffn_out,
           norm_ffn_post):
    b, s, d = x.shape
    depth = w_in.shape[0]
    pool_w = w_pool_o.shape[1]
    sb_w = w_sb_o.shape[1]
    x_w = w_x_o.shape[1]
    sb_heads = sb_w // HEAD_DIM
    x_heads = x_w // HEAD_DIM
    split = pool_w + 3 * sb_w + x_w
    tm = 512
    cm = _suffix_sum_matrix()

    h = x.reshape(b * s, d)
    for l in range(depth):
        row = lambda v: v[l].reshape(1, -1)
        w_in_l = w_in[l].astype(BF16)
        kv = _mem_kv(mem, row(norm_mem), w_mem_kv[l].astype(BF16))
        pin, qkv, xq = _in_proj(h, row(norm_mix_pre), w_in_l[:, :split], tm=tm, pool_w=pool_w,
                                sb_w=sb_w, x_w=x_w)
        ysb = _sb_attn(qkv.reshape(b, s, 3 * sb_w), cm, heads=sb_heads).reshape(b * s, sb_w)
        h = _mix(h, pin, ysb, xq, kv, row(norm_mix_pre), w_in_l[:, split:],
                 w_pool_mix[l].astype(BF16), row(pool_scale), w_pool_o[l].astype(BF16),
                 w_sb_o[l].astype(BF16), w_x_o[l].astype(BF16), w_out[l].astype(BF16),
                 row(norm_mix_post), tm=tm, seq=s, x_heads=x_heads)
        h = _ffn(h, row(norm_ffn_pre), w_ffn_in[l].astype(BF16), w_ffn_out[l].astype(BF16),
                 row(norm_ffn_post), tm=tm, chunk=256)
    return h.reshape(b, s, d)
```

```python
import functools

import jax
import jax.numpy as jnp
import numpy as np
from jax import lax
from jax.experimental import pallas as pl
from jax.experimental.pallas import tpu as pltpu

F32 = jnp.float32
BF16 = jnp.bfloat16

RMS_EPS = 1e-6
POOL_WINDOWS = (2, 4, 8, 16)
POOL_HALO = 16
LANES = 128
HEAD_DIM = 64
SB_BLOCK = 128
SB_SPAN = 3
LOG2E = 1.4426950408889634
SB_EXIT_LOG2 = -150.0
VMEM_LIMIT = 56 * 1024 * 1024


def _rmsnorm(x, g):
    ms = jnp.mean(x * x, axis=-1, keepdims=True)
    return x * lax.rsqrt(ms + RMS_EPS) * g


def _dot(a, b):
    return jnp.dot(a, b, preferred_element_type=F32)


def _dot_nt(a, b):
    return lax.dot_general(a, b, (((1,), (1,)), ((), ())), preferred_element_type=F32)


SIGN_BIT = np.uint32(0x80000000)
BF16_BITS = np.uint32(0xFFFF0000)


def _bits(x):
    return lax.bitcast_convert_type(x, jnp.uint32)


def _from_bits(u):
    return lax.bitcast_convert_type(u, F32)


def _const_spec(shape):
    nd = len(shape)
    return pl.BlockSpec(shape, lambda *_: (0,) * nd, pipeline_mode=pl.Buffered(1))


def _mem_kv_kernel(mem_ref, g_ref, w_ref, o_ref):
    n = _rmsnorm(mem_ref[0], g_ref[...])
    o_ref[0] = _dot(n.astype(BF16), w_ref[...]).astype(BF16)


def _mem_kv(mem, g, w):
    b, m, d = mem.shape
    n_out = w.shape[1]
    return pl.pallas_call(
        _mem_kv_kernel,
        grid=(b,),
        in_specs=[pl.BlockSpec((1, m, d), lambda i: (i, 0, 0)),
                  _const_spec((1, d)),
                  _const_spec((d, n_out))],
        out_specs=pl.BlockSpec((1, m, n_out), lambda i: (i, 0, 0)),
        out_shape=jax.ShapeDtypeStruct((b, m, n_out), BF16),
        compiler_params=pltpu.CompilerParams(dimension_semantics=("arbitrary",)),
        name="mem_kv",
    )(mem, g, w)


def _in_proj_kernel(x_ref, g_ref, w_ref, p_ref, qkv_ref, xq_ref, *, pool_w, sb_w):
    nb = _rmsnorm(x_ref[...], g_ref[...]).astype(BF16)
    scale = 1.0 / (HEAD_DIM ** 0.5)
    o0, o1, o2 = pool_w, pool_w + sb_w, pool_w + 3 * sb_w
    p_ref[...] = _dot(nb, w_ref[:, :o0]).astype(BF16)
    qkv_ref[:, :sb_w] = (_dot(nb, w_ref[:, o0:o1]) * (scale * LOG2E)).astype(BF16)
    qkv_ref[:, sb_w:] = _dot(nb, w_ref[:, o1:o2]).astype(BF16)
    xq_ref[...] = (_dot(nb, w_ref[:, o2:]) * scale).astype(BF16)


def _in_proj(x2, g, w, *, tm, pool_w, sb_w, x_w):
    t, d = x2.shape
    n_in = w.shape[1]
    kern = functools.partial(_in_proj_kernel, pool_w=pool_w, sb_w=sb_w)
    return pl.pallas_call(
        kern,
        grid=(t // tm,),
        in_specs=[pl.BlockSpec((tm, d), lambda i: (i, 0)),
                  _const_spec((1, d)),
                  _const_spec((d, n_in))],
        out_specs=[pl.BlockSpec((tm, pool_w), lambda i: (i, 0)),
                   pl.BlockSpec((tm, 3 * sb_w), lambda i: (i, 0)),
                   pl.BlockSpec((tm, x_w), lambda i: (i, 0))],
        out_shape=[jax.ShapeDtypeStruct((t, pool_w), BF16),
                   jax.ShapeDtypeStruct((t, 3 * sb_w), BF16),
                   jax.ShapeDtypeStruct((t, x_w), BF16)],
        compiler_params=pltpu.CompilerParams(dimension_semantics=("arbitrary",),
                                             vmem_limit_bytes=VMEM_LIMIT),
        name="in_proj",
    )(x2, g, w)


def _sb_kernel(q_ref, k_ref, v_ref, cm_ref, o_ref, qm_ref, z_ref, cat_ref, a_ref,
               acc_ref, carry_ref, *, heads):
    i = pl.program_id(1)
    blk = SB_BLOCK
    npair = heads // 2
    row = lax.broadcasted_iota(jnp.int32, (blk, blk), 0)
    col = lax.broadcasted_iota(jnp.int32, (blk, blk), 1)
    tri = col < row
    lo_half = col < HEAD_DIM

    for p in range(npair):
        qp = q_ref[0, :, p * LANES:(p + 1) * LANES]
        zero = jnp.zeros_like(qp)
        qm_ref[p, :blk] = jnp.where(lo_half, qp, zero)
        qm_ref[p, blk:] = jnp.where(lo_half, zero, qp)

    def visit(first, nb, diag, fresh):
        start = pl.multiple_of(first * blk, blk)
        for p in range(npair):
            kp = k_ref[0, pl.ds(start, nb * blk), p * LANES:(p + 1) * LANES]
            z_ref[p, :, :nb * blk] = _dot_nt(qm_ref[p], kp)
        for h in range(heads):
            p, hh = divmod(h, 2)
            for b in range(nb):
                z = z_ref[p, hh * blk:(hh + 1) * blk, b * blk:(b + 1) * blk]
                neg_abs = _from_bits(_bits(z) | SIGN_BIT)
                sp = jnp.maximum(z, 0.0) + jnp.log2(1.0 + jnp.exp2(neg_abs))
                if diag and b == nb - 1:
                    sp = jnp.where(tri, sp, 0.0)
                hi = _from_bits(_bits(sp) & BF16_BITS)
                r0 = (h * SB_SPAN + b) * blk
                cat_ref[r0:r0 + blk, :blk] = hi.astype(BF16)
                cat_ref[r0:r0 + blk, blk:] = (sp - hi).astype(BF16)
        mx = None
        for h in range(heads):
            p, hh = divmod(h, 2)
            r0 = h * SB_SPAN * blk
            r = _dot(cat_ref[r0:r0 + nb * blk, :], cm_ref[...])
            c = None if fresh else carry_ref[h]
            parts = [None] * nb
            for b in reversed(range(nb)):
                arg = (z_ref[p, hh * blk:(hh + 1) * blk, b * blk:(b + 1) * blk]
                       + r[b * blk:(b + 1) * blk, :blk])
                if c is not None:
                    arg = arg + c
                a = jnp.exp2(arg)
                if diag and b == nb - 1:
                    a = jnp.where(tri, a, 0.0)
                parts[b] = a.astype(BF16)
                tot = r[b * blk:(b + 1) * blk, blk:]
                c = tot if c is None else c + tot
            carry_ref[h] = c
            a_ref[h * blk:(h + 1) * blk, :nb * blk] = (
                parts[0] if nb == 1 else jnp.concatenate(parts, axis=1))
            mx = c if mx is None else jnp.maximum(mx, c)
        for p in range(npair):
            vp = v_ref[0, pl.ds(start, nb * blk), p * LANES:(p + 1) * LANES]
            res = _dot(a_ref[2 * p * blk:2 * (p + 1) * blk, :nb * blk], vp)
            for hh in range(2):
                if fresh:
                    acc_ref[2 * p + hh] = res[hh * blk:(hh + 1) * blk]
                else:
                    acc_ref[2 * p + hh] += res[hh * blk:(hh + 1) * blk]
        return jnp.max(mx)

    far = i >= SB_SPAN - 1
    m0 = lax.cond(far,
                  lambda: visit(i - (SB_SPAN - 1), SB_SPAN, True, True),
                  lambda: visit(i, 1, True, True))
    j0 = jnp.where(far, i - SB_SPAN, i - 1)

    def cond(st):
        j, m = st
        return jnp.logical_and(j >= 0, m > SB_EXIT_LOG2)

    def body(st):
        j, _ = st
        return j - 1, visit(j, 1, False, False)

    lax.while_loop(cond, body, (j0, m0))

    for p in range(npair):
        o = jnp.where(lo_half, acc_ref[2 * p], acc_ref[2 * p + 1])
        o_ref[0, :, p * LANES:(p + 1) * LANES] = o.astype(BF16)


def _sb_attn(qkv, cm, *, heads):
    b, s, w3 = qkv.shape
    w = w3 // 3
    blk = SB_BLOCK
    kern = functools.partial(_sb_kernel, heads=heads)
    return pl.pallas_call(
        kern,
        grid=(b, s // blk),
        in_specs=[pl.BlockSpec((1, blk, w), lambda bi, i: (bi, i, 0)),
                  pl.BlockSpec((1, s, w), lambda bi, i: (bi, 0, 1), pipeline_mode=pl.Buffered(1)),
                  pl.BlockSpec((1, s, w), lambda bi, i: (bi, 0, 2), pipeline_mode=pl.Buffered(1)),
                  _const_spec((2 * blk, 2 * blk))],
        out_specs=pl.BlockSpec((1, blk, w), lambda bi, i: (bi, i, 0)),
        out_shape=jax.ShapeDtypeStruct((b, s, w), BF16),
        scratch_shapes=[pltpu.VMEM((heads // 2, 2 * blk, LANES), BF16),
                        pltpu.VMEM((heads // 2, 2 * blk, SB_SPAN * blk), F32),
                        pltpu.VMEM((heads * SB_SPAN * blk, 2 * blk), BF16),
                        pltpu.VMEM((heads * blk, SB_SPAN * blk), BF16),
                        pltpu.VMEM((heads, blk, LANES), F32),
                        pltpu.VMEM((heads, blk, LANES), F32)],
        compiler_params=pltpu.CompilerParams(dimension_semantics=("arbitrary", "arbitrary"),
                                             vmem_limit_bytes=VMEM_LIMIT),
        name="sb_attn",
    )(qkv, qkv, qkv, cm)


def _suffix_sum_matrix():
    blk = SB_BLOCK
    j = jnp.arange(2 * blk)[:, None] % blk
    s = jnp.arange(2 * blk)[None, :]
    return jnp.where(jnp.logical_or(s >= blk, j >= s), -1.0, 0.0).astype(BF16)


def _mix_kernel(x_ref, pin_ref, halo_ref, ysb_ref, xq_ref, kv_ref, gpre_ref, wg_ref, wmix_ref,
                pscale_ref, wpo_ref, wsbo_ref, wxo_ref, wout_ref, gpost_ref, o_ref, ext_ref,
                *, seq, x_heads):
    t = pl.program_id(0)
    tm, d = x_ref.shape
    x = x_ref[...]
    nb = _rmsnorm(x, gpre_ref[...]).astype(BF16)

    tok0 = (t * tm) % seq
    halo = halo_ref[...].astype(F32)
    ext_ref[:POOL_HALO, :] = jnp.where(tok0 == 0, 0.0, halo)
    ext_ref[POOL_HALO:, :] = pin_ref[...].astype(F32)
    pos1 = tok0 + 1 + lax.broadcasted_iota(jnp.int32, (tm, 1), 0)
    groups = []
    for g, w in enumerate(POOL_WINDOWS):
        cs = slice(g * LANES, (g + 1) * LANES)
        cur = ext_ref[POOL_HALO:, cs]
        acc = cur
        for k in range(1, w):
            acc = acc + ext_ref[POOL_HALO - k:POOL_HALO - k + tm, cs]
        inv = 1.0 / jnp.minimum(pos1, w).astype(F32)
        mixed = acc * inv - cur
        groups.append(_dot(mixed.astype(BF16), wmix_ref[g]))
    ypool = jnp.concatenate(groups, axis=1) * pscale_ref[...]
    merged = jax.nn.sigmoid(_dot(nb, wg_ref[:, :d])) * _dot(ypool.astype(BF16), wpo_ref[...])

    merged += jax.nn.sigmoid(_dot(nb, wg_ref[:, d:2 * d])) * _dot(ysb_ref[...], wsbo_ref[...])

    m_len = kv_ref.shape[1]
    xw = x_heads * HEAD_DIM
    lane = lax.broadcasted_iota(jnp.int32, (tm, LANES), 1)
    lo_half = lane < HEAD_DIM
    ones = jnp.ones((m_len, LANES), BF16)
    pairs = []
    for p in range(x_heads // 2):
        cs = slice(p * LANES, (p + 1) * LANES)
        xq = xq_ref[:, cs]
        mk = kv_ref[0, :, cs]
        mv1 = jnp.concatenate([kv_ref[0, :, xw + p * LANES:xw + (p + 1) * LANES], ones], axis=1)
        outs = []
        for hh in range(2):
            keep = lo_half if hh == 0 else jnp.logical_not(lo_half)
            sc = _dot_nt(jnp.where(keep, xq, jnp.zeros_like(xq)), mk)
            e = jnp.exp(sc - jnp.max(sc, axis=-1, keepdims=True)).astype(BF16)
            r = _dot(e, mv1)
            outs.append(r[:, :LANES] / r[:, LANES:])
        pairs.append(jnp.where(lo_half, outs[0], outs[1]))
    yx = jnp.concatenate(pairs, axis=1).astype(BF16)
    merged += jax.nn.sigmoid(_dot(nb, wg_ref[:, 2 * d:])) * _dot(yx, wxo_ref[...])

    mo = _dot(merged.astype(BF16), wout_ref[...])
    o_ref[...] = x + _rmsnorm(mo, gpost_ref[...])


def _mix(x2, pin, ysb, xq, kv, gpre, wg, wmix, pscale, wpo, wsbo, wxo, wout, gpost, *, tm, seq,
         x_heads):
    t, d = x2.shape
    pw = pin.shape[1]
    hb = tm // POOL_HALO
    kern = functools.partial(_mix_kernel, seq=seq, x_heads=x_heads)
    return pl.pallas_call(
        kern,
        grid=(t // tm,),
        in_specs=[pl.BlockSpec((tm, d), lambda i: (i, 0)),
                  pl.BlockSpec((tm, pw), lambda i: (i, 0)),
                  pl.BlockSpec((POOL_HALO, pw), lambda i: (jnp.maximum(i * hb - 1, 0), 0)),
                  pl.BlockSpec((tm, ysb.shape[1]), lambda i: (i, 0)),
                  pl.BlockSpec((tm, xq.shape[1]), lambda i: (i, 0)),
                  pl.BlockSpec((1,) + kv.shape[1:], lambda i: ((i * tm) // seq, 0, 0)),
                  _const_spec(gpre.shape), _const_spec(wg.shape), _const_spec(wmix.shape),
                  _const_spec(pscale.shape), _const_spec(wpo.shape), _const_spec(wsbo.shape),
                  _const_spec(wxo.shape), _const_spec(wout.shape), _const_spec(gpost.shape)],
        out_specs=pl.BlockSpec((tm, d), lambda i: (i, 0)),
        out_shape=jax.ShapeDtypeStruct((t, d), F32),
        scratch_shapes=[pltpu.VMEM((tm + POOL_HALO, pw), F32)],
        compiler_params=pltpu.CompilerParams(dimension_semantics=("arbitrary",),
                                             vmem_limit_bytes=VMEM_LIMIT),
        name="mix",
    )(x2, pin, pin, ysb, xq, kv, gpre, wg, wmix, pscale, wpo, wsbo, wxo, wout, gpost)


def _ffn_kernel(h_ref, gpre_ref, win_ref, wout_ref, gpost_ref, o_ref, a_ref, *, d_ff, chunk):
    h = h_ref[...]
    nb = _rmsnorm(h, gpre_ref[...]).astype(BF16)
    for c in range(d_ff // chunk):
        g = _dot(nb, win_ref[:, c * chunk:(c + 1) * chunk])
        u = _dot(nb, win_ref[:, d_ff + c * chunk:d_ff + (c + 1) * chunk])
        a_ref[:, c * chunk:(c + 1) * chunk] = (g * jax.nn.sigmoid(g) * u).astype(BF16)
    ff = _dot(a_ref[...], wout_ref[...])
    o_ref[...] = h + _rmsnorm(ff, gpost_ref[...])


def _ffn(h, gpre, win, wout, gpost, *, tm, chunk):
    t, d = h.shape
    d_ff = wout.shape[0]
    kern = functools.partial(_ffn_kernel, d_ff=d_ff, chunk=chunk)
    return pl.pallas_call(
        kern,
        grid=(t // tm,),
        in_specs=[pl.BlockSpec((tm, d), lambda i: (i, 0)),
                  _const_spec(gpre.shape), _const_spec(win.shape), _const_spec(wout.shape),
                  _const_spec(gpost.shape)],
        out_specs=pl.BlockSpec((tm, d), lambda i: (i, 0)),
        out_shape=jax.ShapeDtypeStruct((t, d), F32),
        scratch_shapes=[pltpu.VMEM((tm, d_ff), BF16)],
        compiler_params=pltpu.CompilerParams(dimension_semantics=("arbitrary",),
                                             vmem_limit_bytes=VMEM_LIMIT),
        name="ffn",
    )(h, gpre, win, wout, gpost)


def kernel(x, mem, norm_mix_pre, w_in, w_pool_mix, pool_scale, w_pool_o, w_sb_o, norm_mem,
           w_mem_kv, w_x_o, w_out, norm_mix_post, norm_ffn_pre, w_ffn_in, w_ffn_out,
           norm_ffn_post):
    b, s, d = x.shape
    depth = w_in.shape[0]
    pool_w = w_pool_o.shape[1]
    sb_w = w_sb_o.shape[1]
    x_w = w_x_o.shape[1]
    sb_heads = sb_w // HEAD_DIM
    x_heads = x_w // HEAD_DIM
    split = pool_w + 3 * sb_w + x_w
    tm = 512
    cm = _suffix_sum_matrix()

    h = x.reshape(b * s, d)
    for l in range(depth):
        row = lambda v: v[l].reshape(1, -1)
        w_in_l = w_in[l].astype(BF16)
        kv = _mem_kv(mem, row(norm_mem), w_mem_kv[l].astype(BF16))
        pin, qkv, xq = _in_proj(h, row(norm_mix_pre), w_in_l[:, :split], tm=tm, pool_w=pool_w,
                                sb_w=sb_w, x_w=x_w)
        ysb = _sb_attn(qkv.reshape(b, s, 3 * sb_w), cm, heads=sb_heads).reshape(b * s, sb_w)
        h = _mix(h, pin, ysb, xq, kv, row(norm_mix_pre), w_in_l[:, split:],
                 w_pool_mix[l].astype(BF16), row(pool_scale), w_pool_o[l].astype(BF16),
                 w_sb_o[l].astype(BF16), w_x_o[l].astype(BF16), w_out[l].astype(BF16),
                 row(norm_mix_post), tm=tm, seq=s, x_heads=x_heads)
        h = _ffn(h, row(norm_ffn_pre), w_ffn_in[l].astype(BF16), w_ffn_out[l].astype(BF16),
                 row(norm_ffn_post), tm=tm, chunk=256)
    return h.reshape(b, s, d)
```

```python
import functools

import jax
import jax.numpy as jnp
from jax import lax
from jax.experimental import pallas as pl
from jax.experimental.pallas import tpu as pltpu

F32 = jnp.float32
BF16 = jnp.bfloat16

RMS_EPS = 1e-6
POOL_WINDOWS = (2, 4, 8, 16)
POOL_HALO = 16
LANES = 128
HEAD_DIM = 64
SB_BLOCK = 128
SB_SPAN = 3
LOG2E = 1.4426950408889634
SB_EXIT_LOG2 = -150.0
VMEM_LIMIT = 56 * 1024 * 1024


def _rmsnorm(x, g):
    ms = jnp.mean(x * x, axis=-1, keepdims=True)
    return x * lax.rsqrt(ms + RMS_EPS) * g


def _dot(a, b):
    return jnp.dot(a, b, preferred_element_type=F32)


def _dot_nt(a, b):
    return lax.dot_general(a, b, (((1,), (1,)), ((), ())), preferred_element_type=F32)


def _const_spec(shape):
    nd = len(shape)
    return pl.BlockSpec(shape, lambda *_: (0,) * nd, pipeline_mode=pl.Buffered(1))


def _col_window_spec(rows, col0, ncols):
    return pl.BlockSpec((pl.Element(rows), pl.Element(ncols)), lambda *_: (0, col0),
                        pipeline_mode=pl.Buffered(1))


def _mem_kv_kernel(mem_ref, g_ref, w_ref, o_ref):
    n = _rmsnorm(mem_ref[0], g_ref[...])
    o_ref[0] = _dot(n.astype(BF16), w_ref[...]).astype(BF16)


def _mem_kv(mem, g, w):
    b, m, d = mem.shape
    n_out = w.shape[1]
    return pl.pallas_call(
        _mem_kv_kernel,
        grid=(b,),
        in_specs=[pl.BlockSpec((1, m, d), lambda i: (i, 0, 0)),
                  _const_spec((1, d)),
                  _const_spec((d, n_out))],
        out_specs=pl.BlockSpec((1, m, n_out), lambda i: (i, 0, 0)),
        out_shape=jax.ShapeDtypeStruct((b, m, n_out), BF16),
        compiler_params=pltpu.CompilerParams(dimension_semantics=("arbitrary",)),
        name="mem_kv",
    )(mem, g, w)


def _in_proj_kernel(x_ref, g_ref, w_ref, p_ref, qkv_ref, xq_ref, *, pool_w, sb_w):
    nb = _rmsnorm(x_ref[...], g_ref[...]).astype(BF16)
    scale = 1.0 / (HEAD_DIM ** 0.5)
    o0, o1, o2 = pool_w, pool_w + sb_w, pool_w + 3 * sb_w
    p_ref[...] = _dot(nb, w_ref[:, :o0]).astype(BF16)
    qkv_ref[:, :sb_w] = (_dot(nb, w_ref[:, o0:o1]) * (scale * LOG2E)).astype(BF16)
    qkv_ref[:, sb_w:] = _dot(nb, w_ref[:, o1:o2]).astype(BF16)
    xq_ref[...] = (_dot(nb, w_ref[:, o2:]) * scale).astype(BF16)


def _in_proj(x2, g, w, *, tm, pool_w, sb_w, x_w):
    t, d = x2.shape
    n_in = pool_w + 3 * sb_w + x_w
    kern = functools.partial(_in_proj_kernel, pool_w=pool_w, sb_w=sb_w)
    return pl.pallas_call(
        kern,
        grid=(t // tm,),
        in_specs=[pl.BlockSpec((tm, d), lambda i: (i, 0)),
                  _const_spec((1, d)),
                  _col_window_spec(d, 0, n_in)],
        out_specs=[pl.BlockSpec((tm, pool_w), lambda i: (i, 0)),
                   pl.BlockSpec((tm, 3 * sb_w), lambda i: (i, 0)),
                   pl.BlockSpec((tm, x_w), lambda i: (i, 0))],
        out_shape=[jax.ShapeDtypeStruct((t, pool_w), BF16),
                   jax.ShapeDtypeStruct((t, 3 * sb_w), BF16),
                   jax.ShapeDtypeStruct((t, x_w), BF16)],
        compiler_params=pltpu.CompilerParams(dimension_semantics=("arbitrary",),
                                             vmem_limit_bytes=VMEM_LIMIT),
        name="in_proj",
    )(x2, g, w)


def _sb_kernel(q_ref, k_ref, v_ref, cm_ref, o_ref, qm_ref, z_ref, cat_ref, a_ref,
               acc_ref, carry_ref, *, heads):
    i = pl.program_id(1)
    blk = SB_BLOCK
    npair = heads // 2
    row = lax.broadcasted_iota(jnp.int32, (blk, blk), 0)
    col = lax.broadcasted_iota(jnp.int32, (blk, blk), 1)
    tri = col < row
    lo_half = col < HEAD_DIM

    for p in range(npair):
        qp = q_ref[0, :, p * LANES:(p + 1) * LANES]
        zero = jnp.zeros_like(qp)
        qm_ref[p, :blk] = jnp.where(lo_half, qp, zero)
        qm_ref[p, blk:] = jnp.where(lo_half, zero, qp)

    def visit(first, nb, diag, fresh):
        start = pl.multiple_of(first * blk, blk)
        for p in range(npair):
            kp = k_ref[0, pl.ds(start, nb * blk), p * LANES:(p + 1) * LANES]
            z_ref[p, :, :nb * blk] = _dot_nt(qm_ref[p], kp)
        for h in range(heads):
            p, hh = divmod(h, 2)
            for b in range(nb):
                z = z_ref[p, hh * blk:(hh + 1) * blk, b * blk:(b + 1) * blk]
                sp = jnp.maximum(z, 0.0) + jnp.log2(1.0 + jnp.exp2(-jnp.abs(z)))
                if diag and b == nb - 1:
                    sp = jnp.where(tri, sp, 0.0)
                hi = sp.astype(BF16)
                r0 = (h * SB_SPAN + b) * blk
                cat_ref[r0:r0 + blk, :blk] = hi
                cat_ref[r0:r0 + blk, blk:] = (sp - hi.astype(F32)).astype(BF16)
        mx = None
        for h in range(heads):
            p, hh = divmod(h, 2)
            r0 = h * SB_SPAN * blk
            r = _dot(cat_ref[r0:r0 + nb * blk, :], cm_ref[...])
            c = None if fresh else carry_ref[h]
            parts = [None] * nb
            for b in reversed(range(nb)):
                arg = (z_ref[p, hh * blk:(hh + 1) * blk, b * blk:(b + 1) * blk]
                       + r[b * blk:(b + 1) * blk, :blk])
                if c is not None:
                    arg = arg + c
                a = jnp.exp2(arg)
                if diag and b == nb - 1:
                    a = jnp.where(tri, a, 0.0)
                parts[b] = a.astype(BF16)
                tot = r[b * blk:(b + 1) * blk, blk:]
                c = tot if c is None else c + tot
            carry_ref[h] = c
            a_ref[h * blk:(h + 1) * blk, :nb * blk] = (
                parts[0] if nb == 1 else jnp.concatenate(parts, axis=1))
            mx = c if mx is None else jnp.maximum(mx, c)
        for p in range(npair):
            vp = v_ref[0, pl.ds(start, nb * blk), p * LANES:(p + 1) * LANES]
            res = _dot(a_ref[2 * p * blk:2 * (p + 1) * blk, :nb * blk], vp)
            for hh in range(2):
                if fresh:
                    acc_ref[2 * p + hh] = res[hh * blk:(hh + 1) * blk]
                else:
                    acc_ref[2 * p + hh] += res[hh * blk:(hh + 1) * blk]
        return jnp.max(mx)

    far = i >= SB_SPAN - 1
    m0 = lax.cond(far,
                  lambda: visit(i - (SB_SPAN - 1), SB_SPAN, True, True),
                  lambda: visit(i, 1, True, True))
    j0 = jnp.where(far, i - SB_SPAN, i - 1)

    def cond(st):
        j, m = st
        return jnp.logical_and(j >= 0, m > SB_EXIT_LOG2)

    def body(st):
        j, _ = st
        return j - 1, visit(j, 1, False, False)

    lax.while_loop(cond, body, (j0, m0))

    for p in range(npair):
        o = jnp.where(lo_half, acc_ref[2 * p], acc_ref[2 * p + 1])
        o_ref[0, :, p * LANES:(p + 1) * LANES] = o.astype(BF16)


def _sb_attn(qkv, cm, *, heads):
    b, s, w3 = qkv.shape
    w = w3 // 3
    blk = SB_BLOCK
    kern = functools.partial(_sb_kernel, heads=heads)
    return pl.pallas_call(
        kern,
        grid=(b, s // blk),
        in_specs=[pl.BlockSpec((1, blk, w), lambda bi, i: (bi, i, 0)),
                  pl.BlockSpec((1, s, w), lambda bi, i: (bi, 0, 1), pipeline_mode=pl.Buffered(1)),
                  pl.BlockSpec((1, s, w), lambda bi, i: (bi, 0, 2), pipeline_mode=pl.Buffered(1)),
                  _const_spec((2 * blk, 2 * blk))],
        out_specs=pl.BlockSpec((1, blk, w), lambda bi, i: (bi, i, 0)),
        out_shape=jax.ShapeDtypeStruct((b, s, w), BF16),
        scratch_shapes=[pltpu.VMEM((heads // 2, 2 * blk, LANES), BF16),
                        pltpu.VMEM((heads // 2, 2 * blk, SB_SPAN * blk), F32),
                        pltpu.VMEM((heads * SB_SPAN * blk, 2 * blk), BF16),
                        pltpu.VMEM((heads * blk, SB_SPAN * blk), BF16),
                        pltpu.VMEM((heads, blk, LANES), F32),
                        pltpu.VMEM((heads, blk, LANES), F32)],
        compiler_params=pltpu.CompilerParams(dimension_semantics=("arbitrary", "arbitrary"),
                                             vmem_limit_bytes=VMEM_LIMIT),
        name="sb_attn",
    )(qkv, qkv, qkv, cm)


def _suffix_sum_matrix():
    blk = SB_BLOCK
    j = jnp.arange(2 * blk)[:, None] % blk
    s = jnp.arange(2 * blk)[None, :]
    return jnp.where(jnp.logical_or(s >= blk, j >= s), -1.0, 0.0).astype(BF16)


def _mix_kernel(x_ref, pin_ref, halo_ref, ysb_ref, xq_ref, kv_ref, gpre_ref, wg_ref, wmix_ref,
                pscale_ref, wpo_ref, wsbo_ref, wxo_ref, wout_ref, gpost_ref, o_ref, ext_ref,
                *, seq, x_heads):
    t = pl.program_id(0)
    tm, d = x_ref.shape
    x = x_ref[...]
    nb = _rmsnorm(x, gpre_ref[...]).astype(BF16)

    tok0 = (t * tm) % seq
    halo = halo_ref[...].astype(F32)
    ext_ref[:POOL_HALO, :] = jnp.where(tok0 == 0, 0.0, halo)
    ext_ref[POOL_HALO:, :] = pin_ref[...].astype(F32)
    pos1 = tok0 + 1 + lax.broadcasted_iota(jnp.int32, (tm, 1), 0)
    groups = []
    for g, w in enumerate(POOL_WINDOWS):
        cs = slice(g * LANES, (g + 1) * LANES)
        cur = ext_ref[POOL_HALO:, cs]
        acc = cur
        for k in range(1, w):
            acc = acc + ext_ref[POOL_HALO - k:POOL_HALO - k + tm, cs]
        inv = 1.0 / jnp.minimum(pos1, w).astype(F32)
        mixed = acc * inv - cur
        groups.append(_dot(mixed.astype(BF16), wmix_ref[g]))
    ypool = jnp.concatenate(groups, axis=1) * pscale_ref[...]
    merged = jax.nn.sigmoid(_dot(nb, wg_ref[:, :d])) * _dot(ypool.astype(BF16), wpo_ref[...])

    merged += jax.nn.sigmoid(_dot(nb, wg_ref[:, d:2 * d])) * _dot(ysb_ref[...], wsbo_ref[...])

    m_len = kv_ref.shape[1]
    xw = x_heads * HEAD_DIM
    lane = lax.broadcasted_iota(jnp.int32, (tm, LANES), 1)
    lo_half = lane < HEAD_DIM
    ones = jnp.ones((m_len, LANES), BF16)
    pairs = []
    for p in range(x_heads // 2):
        cs = slice(p * LANES, (p + 1) * LANES)
        xq = xq_ref[:, cs]
        mk = kv_ref[0, :, cs]
        mv1 = jnp.concatenate([kv_ref[0, :, xw + p * LANES:xw + (p + 1) * LANES], ones], axis=1)
        outs = []
        for hh in range(2):
            keep = lo_half if hh == 0 else jnp.logical_not(lo_half)
            sc = _dot_nt(jnp.where(keep, xq, jnp.zeros_like(xq)), mk)
            e = jnp.exp(sc - jnp.max(sc, axis=-1, keepdims=True)).astype(BF16)
            r = _dot(e, mv1)
            outs.append(r[:, :LANES] / r[:, LANES:])
        pairs.append(jnp.where(lo_half, outs[0], outs[1]))
    yx = jnp.concatenate(pairs, axis=1).astype(BF16)
    merged += jax.nn.sigmoid(_dot(nb, wg_ref[:, 2 * d:])) * _dot(yx, wxo_ref[...])

    mo = _dot(merged.astype(BF16), wout_ref[...])
    o_ref[...] = x + _rmsnorm(mo, gpost_ref[...])


def _mix(x2, pin, ysb, xq, kv, gpre, w_in, wmix, pscale, wpo, wsbo, wxo, wout, gpost, *, tm, seq,
         x_heads, gate_col0):
    t, d = x2.shape
    pw = pin.shape[1]
    hb = tm // POOL_HALO
    n_gate = w_in.shape[1] - gate_col0
    kern = functools.partial(_mix_kernel, seq=seq, x_heads=x_heads)
    return pl.pallas_call(
        kern,
        grid=(t // tm,),
        in_specs=[pl.BlockSpec((tm, d), lambda i: (i, 0)),
                  pl.BlockSpec((tm, pw), lambda i: (i, 0)),
                  pl.BlockSpec((POOL_HALO, pw), lambda i: (jnp.maximum(i * hb - 1, 0), 0)),
                  pl.BlockSpec((tm, ysb.shape[1]), lambda i: (i, 0)),
                  pl.BlockSpec((tm, xq.shape[1]), lambda i: (i, 0)),
                  pl.BlockSpec((1,) + kv.shape[1:], lambda i: ((i * tm) // seq, 0, 0)),
                  _const_spec(gpre.shape), _col_window_spec(d, gate_col0, n_gate),
                  _const_spec(wmix.shape),
                  _const_spec(pscale.shape), _const_spec(wpo.shape), _const_spec(wsbo.shape),
                  _const_spec(wxo.shape), _const_spec(wout.shape), _const_spec(gpost.shape)],
        out_specs=pl.BlockSpec((tm, d), lambda i: (i, 0)),
        out_shape=jax.ShapeDtypeStruct((t, d), F32),
        scratch_shapes=[pltpu.VMEM((tm + POOL_HALO, pw), F32)],
        compiler_params=pltpu.CompilerParams(dimension_semantics=("arbitrary",),
                                             vmem_limit_bytes=VMEM_LIMIT),
        name="mix",
    )(x2, pin, pin, ysb, xq, kv, gpre, w_in, wmix, pscale, wpo, wsbo, wxo, wout, gpost)


def _ffn_kernel(h_ref, gpre_ref, win_ref, wout_ref, gpost_ref, o_ref, a_ref, *, d_ff, chunk):
    h = h_ref[...]
    nb = _rmsnorm(h, gpre_ref[...]).astype(BF16)
    for c in range(d_ff // chunk):
        g = _dot(nb, win_ref[:, c * chunk:(c + 1) * chunk])
        u = _dot(nb, win_ref[:, d_ff + c * chunk:d_ff + (c + 1) * chunk])
        a_ref[:, c * chunk:(c + 1) * chunk] = (g * jax.nn.sigmoid(g) * u).astype(BF16)
    ff = _dot(a_ref[...], wout_ref[...])
    o_ref[...] = h + _rmsnorm(ff, gpost_ref[...])


def _ffn(h, gpre, win, wout, gpost, *, tm, chunk):
    t, d = h.shape
    d_ff = wout.shape[0]
    kern = functools.partial(_ffn_kernel, d_ff=d_ff, chunk=chunk)
    return pl.pallas_call(
        kern,
        grid=(t // tm,),
        in_specs=[pl.BlockSpec((tm, d), lambda i: (i, 0)),
                  _const_spec(gpre.shape), _const_spec(win.shape), _const_spec(wout.shape),
                  _const_spec(gpost.shape)],
        out_specs=pl.BlockSpec((tm, d), lambda i: (i, 0)),
        out_shape=jax.ShapeDtypeStruct((t, d), F32),
        scratch_shapes=[pltpu.VMEM((tm, d_ff), BF16)],
        compiler_params=pltpu.CompilerParams(dimension_semantics=("arbitrary",),
                                             vmem_limit_bytes=VMEM_LIMIT),
        name="ffn",
    )(h, gpre, win, wout, gpost)


def kernel(x, mem, norm_mix_pre, w_in, w_pool_mix, pool_scale, w_pool_o, w_sb_o, norm_mem,
           w_mem_kv, w_x_o, w_out, norm_mix_post, norm_ffn_pre, w_ffn_in, w_ffn_out,
           norm_ffn_post):
    b, s, d = x.shape
    depth = w_in.shape[0]
    pool_w = w_pool_o.shape[1]
    sb_w = w_sb_o.shape[1]
    x_w = w_x_o.shape[1]
    sb_heads = sb_w // HEAD_DIM
    x_heads = x_w // HEAD_DIM
    split = pool_w + 3 * sb_w + x_w
    tm = 512
    cm = _suffix_sum_matrix()

    h = x.reshape(b * s, d)
    for l in range(depth):
        row = lambda v: v[l].reshape(1, -1)
        kv = _mem_kv(mem, row(norm_mem), w_mem_kv[l])
        pin, qkv, xq = _in_proj(h, row(norm_mix_pre), w_in[l], tm=tm, pool_w=pool_w,
                                sb_w=sb_w, x_w=x_w)
        ysb = _sb_attn(qkv.reshape(b, s, 3 * sb_w), cm, heads=sb_heads).reshape(b * s, sb_w)
        h = _mix(h, pin, ysb, xq, kv, row(norm_mix_pre), w_in[l], w_pool_mix[l], row(pool_scale),
                 w_pool_o[l], w_sb_o[l], w_x_o[l], w_out[l], row(norm_mix_post), tm=tm, seq=s,
                 x_heads=x_heads, gate_col0=split)
        h = _ffn(h, row(norm_ffn_pre), w_ffn_in[l], w_ffn_out[l], row(norm_ffn_post), tm=tm,
                 chunk=256)
    return h.reshape(b, s, d)
```

```python
import functools

import jax
import jax.numpy as jnp
from jax import lax
from jax.experimental import pallas as pl
from jax.experimental.pallas import tpu as pltpu

F32 = jnp.float32
BF16 = jnp.bfloat16

RMS_EPS = 1e-6
POOL_WINDOWS = (2, 4, 8, 16)
POOL_BLK = 128
LANES = 128
HEAD_DIM = 64
SB_BLOCK = 128
SB_SPAN = 3
LOG2E = 1.4426950408889634
SB_EXIT_LOG2 = -150.0
VMEM_LIMIT = 56 * 1024 * 1024


def _rmsnorm(x, g):
    ms = jnp.mean(x * x, axis=-1, keepdims=True)
    return x * lax.rsqrt(ms + RMS_EPS) * g


def _dot(a, b):
    return jnp.dot(a, b, preferred_element_type=F32)


def _dot_nt(a, b):
    return lax.dot_general(a, b, (((1,), (1,)), ((), ())), preferred_element_type=F32)


def _const_spec(shape):
    nd = len(shape)
    return pl.BlockSpec(shape, lambda *_: (0,) * nd, pipeline_mode=pl.Buffered(1))


def _col_window_spec(rows, col0, ncols):
    return pl.BlockSpec((pl.Element(rows), pl.Element(ncols)), lambda *_: (0, col0),
                        pipeline_mode=pl.Buffered(1))


def _mem_kv_kernel(mem_ref, g_ref, w_ref, o_ref):
    n = _rmsnorm(mem_ref[0], g_ref[...])
    o_ref[0] = _dot(n.astype(BF16), w_ref[...]).astype(BF16)


def _mem_kv(mem, g, w):
    b, m, d = mem.shape
    n_out = w.shape[1]
    return pl.pallas_call(
        _mem_kv_kernel,
        grid=(b,),
        in_specs=[pl.BlockSpec((1, m, d), lambda i: (i, 0, 0)),
                  _const_spec((1, d)),
                  _const_spec((d, n_out))],
        out_specs=pl.BlockSpec((1, m, n_out), lambda i: (i, 0, 0)),
        out_shape=jax.ShapeDtypeStruct((b, m, n_out), BF16),
        compiler_params=pltpu.CompilerParams(dimension_semantics=("arbitrary",)),
        name="mem_kv",
    )(mem, g, w)


def _in_proj_kernel(x_ref, g_ref, w_ref, p_ref, qkv_ref, xq_ref, *, pool_w, sb_w):
    nb = _rmsnorm(x_ref[...], g_ref[...]).astype(BF16)
    scale = 1.0 / (HEAD_DIM ** 0.5)
    o0, o1, o2 = pool_w, pool_w + sb_w, pool_w + 3 * sb_w
    p_ref[...] = _dot(nb, w_ref[:, :o0]).astype(BF16)
    qkv_ref[:, :sb_w] = (_dot(nb, w_ref[:, o0:o1]) * (scale * LOG2E)).astype(BF16)
    qkv_ref[:, sb_w:] = _dot(nb, w_ref[:, o1:o2]).astype(BF16)
    xq_ref[...] = (_dot(nb, w_ref[:, o2:]) * scale).astype(BF16)


def _in_proj(x2, g, w, *, tm, pool_w, sb_w, x_w):
    t, d = x2.shape
    n_in = pool_w + 3 * sb_w + x_w
    kern = functools.partial(_in_proj_kernel, pool_w=pool_w, sb_w=sb_w)
    return pl.pallas_call(
        kern,
        grid=(t // tm,),
        in_specs=[pl.BlockSpec((tm, d), lambda i: (i, 0)),
                  _const_spec((1, d)),
                  _col_window_spec(d, 0, n_in)],
        out_specs=[pl.BlockSpec((tm, pool_w), lambda i: (i, 0)),
                   pl.BlockSpec((tm, 3 * sb_w), lambda i: (i, 0)),
                   pl.BlockSpec((tm, x_w), lambda i: (i, 0))],
        out_shape=[jax.ShapeDtypeStruct((t, pool_w), BF16),
                   jax.ShapeDtypeStruct((t, 3 * sb_w), BF16),
                   jax.ShapeDtypeStruct((t, x_w), BF16)],
        compiler_params=pltpu.CompilerParams(dimension_semantics=("arbitrary",),
                                             vmem_limit_bytes=VMEM_LIMIT),
        name="in_proj",
    )(x2, g, w)


def _sb_kernel(q_ref, k_ref, v_ref, cm_ref, o_ref, qm_ref, z_ref, cat_ref, a_ref,
               acc_ref, carry_ref, *, heads):
    i = pl.program_id(1)
    blk = SB_BLOCK
    npair = heads // 2
    row = lax.broadcasted_iota(jnp.int32, (blk, blk), 0)
    col = lax.broadcasted_iota(jnp.int32, (blk, blk), 1)
    tri = col < row
    lo_half = col < HEAD_DIM

    for p in range(npair):
        qp = q_ref[0, :, p * LANES:(p + 1) * LANES]
        zero = jnp.zeros_like(qp)
        qm_ref[p, :blk] = jnp.where(lo_half, qp, zero)
        qm_ref[p, blk:] = jnp.where(lo_half, zero, qp)

    def visit(first, nb, diag, fresh):
        start = pl.multiple_of(first * blk, blk)
        for p in range(npair):
            kp = k_ref[0, pl.ds(start, nb * blk), p * LANES:(p + 1) * LANES]
            z_ref[p, :, :nb * blk] = _dot_nt(qm_ref[p], kp)
        for h in range(heads):
            p, hh = divmod(h, 2)
            for b in range(nb):
                z = z_ref[p, hh * blk:(hh + 1) * blk, b * blk:(b + 1) * blk]
                sp = jnp.maximum(z, 0.0) + jnp.log2(1.0 + jnp.exp2(-jnp.abs(z)))
                if diag and b == nb - 1:
                    sp = jnp.where(tri, sp, 0.0)
                hi = sp.astype(BF16)
                r0 = (h * SB_SPAN + b) * blk
                cat_ref[r0:r0 + blk, :blk] = hi
                cat_ref[r0:r0 + blk, blk:] = (sp - hi.astype(F32)).astype(BF16)
        mx = None
        for h in range(heads):
            p, hh = divmod(h, 2)
            r0 = h * SB_SPAN * blk
            r = _dot(cat_ref[r0:r0 + nb * blk, :], cm_ref[...])
            c = None if fresh else carry_ref[h]
            parts = [None] * nb
            for b in reversed(range(nb)):
                arg = (z_ref[p, hh * blk:(hh + 1) * blk, b * blk:(b + 1) * blk]
                       + r[b * blk:(b + 1) * blk, :blk])
                if c is not None:
                    arg = arg + c
                a = jnp.exp2(arg)
                if diag and b == nb - 1:
                    a = jnp.where(tri, a, 0.0)
                parts[b] = a.astype(BF16)
                tot = r[b * blk:(b + 1) * blk, blk:]
                c = tot if c is None else c + tot
            carry_ref[h] = c
            a_ref[h * blk:(h + 1) * blk, :nb * blk] = (
                parts[0] if nb == 1 else jnp.concatenate(parts, axis=1))
            mx = c if mx is None else jnp.maximum(mx, c)
        for p in range(npair):
            vp = v_ref[0, pl.ds(start, nb * blk), p * LANES:(p + 1) * LANES]
            res = _dot(a_ref[2 * p * blk:2 * (p + 1) * blk, :nb * blk], vp)
            for hh in range(2):
                if fresh:
                    acc_ref[2 * p + hh] = res[hh * blk:(hh + 1) * blk]
                else:
                    acc_ref[2 * p + hh] += res[hh * blk:(hh + 1) * blk]
        return jnp.max(mx)

    far = i >= SB_SPAN - 1
    m0 = lax.cond(far,
                  lambda: visit(i - (SB_SPAN - 1), SB_SPAN, True, True),
                  lambda: visit(i, 1, True, True))
    j0 = jnp.where(far, i - SB_SPAN, i - 1)

    def cond(st):
        j, m = st
        return jnp.logical_and(j >= 0, m > SB_EXIT_LOG2)

    def body(st):
        j, _ = st
        return j - 1, visit(j, 1, False, False)

    lax.while_loop(cond, body, (j0, m0))

    for p in range(npair):
        o = jnp.where(lo_half, acc_ref[2 * p], acc_ref[2 * p + 1])
        o_ref[0, :, p * LANES:(p + 1) * LANES] = o.astype(BF16)


def _sb_attn(qkv, cm, *, heads):
    b, s, w3 = qkv.shape
    w = w3 // 3
    blk = SB_BLOCK
    kern = functools.partial(_sb_kernel, heads=heads)
    return pl.pallas_call(
        kern,
        grid=(b, s // blk),
        in_specs=[pl.BlockSpec((1, blk, w), lambda bi, i: (bi, i, 0)),
                  pl.BlockSpec((1, s, w), lambda bi, i: (bi, 0, 1), pipeline_mode=pl.Buffered(1)),
                  pl.BlockSpec((1, s, w), lambda bi, i: (bi, 0, 2), pipeline_mode=pl.Buffered(1)),
                  _const_spec((2 * blk, 2 * blk))],
        out_specs=pl.BlockSpec((1, blk, w), lambda bi, i: (bi, i, 0)),
        out_shape=jax.ShapeDtypeStruct((b, s, w), BF16),
        scratch_shapes=[pltpu.VMEM((heads // 2, 2 * blk, LANES), BF16),
                        pltpu.VMEM((heads // 2, 2 * blk, SB_SPAN * blk), F32),
                        pltpu.VMEM((heads * SB_SPAN * blk, 2 * blk), BF16),
                        pltpu.VMEM((heads * blk, SB_SPAN * blk), BF16),
                        pltpu.VMEM((heads, blk, LANES), F32),
                        pltpu.VMEM((heads, blk, LANES), F32)],
        compiler_params=pltpu.CompilerParams(dimension_semantics=("arbitrary", "arbitrary"),
                                             vmem_limit_bytes=VMEM_LIMIT),
        name="sb_attn",
    )(qkv, qkv, qkv, cm)


def _suffix_sum_matrix():
    blk = SB_BLOCK
    j = jnp.arange(2 * blk)[:, None] % blk
    s = jnp.arange(2 * blk)[None, :]
    return jnp.where(jnp.logical_or(s >= blk, j >= s), -1.0, 0.0).astype(BF16)


def _mix_kernel(x_ref, pin_ref, halo_ref, ysb_ref, xq_ref, kv_ref, band_ref, gpre_ref, wg_ref,
                wmix_ref, pscale_ref, wpo_ref, wsbo_ref, wxo_ref, wout_ref, gpost_ref, o_ref,
                ext_ref, *, seq, x_heads):
    t = pl.program_id(0)
    tm, d = x_ref.shape
    x = x_ref[...]
    nb = _rmsnorm(x, gpre_ref[...]).astype(BF16)

    tok0 = (t * tm) % seq
    halo = halo_ref[...]
    ext_ref[:POOL_BLK, :] = jnp.where(tok0 == 0, jnp.zeros_like(halo), halo)
    ext_ref[POOL_BLK:, :] = pin_ref[...]
    pos1 = tok0 + 1 + lax.broadcasted_iota(jnp.int32, (tm, 1), 0)
    groups = []
    for g, w in enumerate(POOL_WINDOWS):
        cs = slice(g * LANES, (g + 1) * LANES)
        inv = 1.0 / jnp.minimum(pos1, w).astype(F32)
        chunks = []
        for c in range(tm // POOL_BLK):
            wsum = _dot(band_ref[g], ext_ref[c * POOL_BLK:(c + 2) * POOL_BLK, cs])
            cur = ext_ref[(c + 1) * POOL_BLK:(c + 2) * POOL_BLK, cs].astype(F32)
            chunks.append(wsum * inv[c * POOL_BLK:(c + 1) * POOL_BLK] - cur)
        mixed = jnp.concatenate(chunks, axis=0)
        groups.append(_dot(mixed.astype(BF16), wmix_ref[g]))
    ypool = jnp.concatenate(groups, axis=1) * pscale_ref[...]
    merged = jax.nn.sigmoid(_dot(nb, wg_ref[:, :d])) * _dot(ypool.astype(BF16), wpo_ref[...])

    merged += jax.nn.sigmoid(_dot(nb, wg_ref[:, d:2 * d])) * _dot(ysb_ref[...], wsbo_ref[...])

    m_len = kv_ref.shape[1]
    xw = x_heads * HEAD_DIM
    lane = lax.broadcasted_iota(jnp.int32, (tm, LANES), 1)
    lo_half = lane < HEAD_DIM
    ones = jnp.ones((m_len, LANES), BF16)
    pairs = []
    for p in range(x_heads // 2):
        cs = slice(p * LANES, (p + 1) * LANES)
        xq = xq_ref[:, cs]
        mk = kv_ref[0, :, cs]
        mv1 = jnp.concatenate([kv_ref[0, :, xw + p * LANES:xw + (p + 1) * LANES], ones], axis=1)
        outs = []
        for hh in range(2):
            keep = lo_half if hh == 0 else jnp.logical_not(lo_half)
            sc = _dot_nt(jnp.where(keep, xq, jnp.zeros_like(xq)), mk)
            e = jnp.exp(sc - jnp.max(sc, axis=-1, keepdims=True)).astype(BF16)
            r = _dot(e, mv1)
            outs.append(r[:, :LANES] / r[:, LANES:])
        pairs.append(jnp.where(lo_half, outs[0], outs[1]))
    yx = jnp.concatenate(pairs, axis=1).astype(BF16)
    merged += jax.nn.sigmoid(_dot(nb, wg_ref[:, 2 * d:])) * _dot(yx, wxo_ref[...])

    mo = _dot(merged.astype(BF16), wout_ref[...])
    o_ref[...] = x + _rmsnorm(mo, gpost_ref[...])


def _mix(x2, pin, ysb, xq, kv, gpre, w_in, wmix, pscale, wpo, wsbo, wxo, wout, gpost, *, tm, seq,
         x_heads, gate_col0):
    t, d = x2.shape
    pw = pin.shape[1]
    hb = tm // POOL_BLK
    n_gate = w_in.shape[1] - gate_col0
    band = _pool_band_matrices()
    kern = functools.partial(_mix_kernel, seq=seq, x_heads=x_heads)
    return pl.pallas_call(
        kern,
        grid=(t // tm,),
        in_specs=[pl.BlockSpec((tm, d), lambda i: (i, 0)),
                  pl.BlockSpec((tm, pw), lambda i: (i, 0)),
                  pl.BlockSpec((POOL_BLK, pw), lambda i: (jnp.maximum(i * hb - 1, 0), 0)),
                  pl.BlockSpec((tm, ysb.shape[1]), lambda i: (i, 0)),
                  pl.BlockSpec((tm, xq.shape[1]), lambda i: (i, 0)),
                  pl.BlockSpec((1,) + kv.shape[1:], lambda i: ((i * tm) // seq, 0, 0)),
                  _const_spec(band.shape),
                  _const_spec(gpre.shape), _col_window_spec(d, gate_col0, n_gate),
                  _const_spec(wmix.shape),
                  _const_spec(pscale.shape), _const_spec(wpo.shape), _const_spec(wsbo.shape),
                  _const_spec(wxo.shape), _const_spec(wout.shape), _const_spec(gpost.shape)],
        out_specs=pl.BlockSpec((tm, d), lambda i: (i, 0)),
        out_shape=jax.ShapeDtypeStruct((t, d), F32),
        scratch_shapes=[pltpu.VMEM((tm + POOL_BLK, pw), BF16)],
        compiler_params=pltpu.CompilerParams(dimension_semantics=("arbitrary",),
                                             vmem_limit_bytes=VMEM_LIMIT),
        name="mix",
    )(x2, pin, pin, ysb, xq, kv, band, gpre, w_in, wmix, pscale, wpo, wsbo, wxo, wout, gpost)


def _pool_band_matrices():
    t = jnp.arange(POOL_BLK)[None, :, None] + POOL_BLK
    j = jnp.arange(2 * POOL_BLK)[None, None, :]
    w = jnp.asarray(POOL_WINDOWS)[:, None, None]
    return jnp.where(jnp.logical_and(j <= t, j > t - w), 1.0, 0.0).astype(BF16)


def _ffn_kernel(h_ref, gpre_ref, win_ref, wout_ref, gpost_ref, o_ref, a_ref, *, d_ff, chunk):
    h = h_ref[...]
    nb = _rmsnorm(h, gpre_ref[...]).astype(BF16)
    for c in range(d_ff // chunk):
        g = _dot(nb, win_ref[:, c * chunk:(c + 1) * chunk])
        u = _dot(nb, win_ref[:, d_ff + c * chunk:d_ff + (c + 1) * chunk])
        a_ref[:, c * chunk:(c + 1) * chunk] = (g * jax.nn.sigmoid(g) * u).astype(BF16)
    ff = _dot(a_ref[...], wout_ref[...])
    o_ref[...] = h + _rmsnorm(ff, gpost_ref[...])


def _ffn(h, gpre, win, wout, gpost, *, tm, chunk):
    t, d = h.shape
    d_ff = wout.shape[0]
    kern = functools.partial(_ffn_kernel, d_ff=d_ff, chunk=chunk)
    return pl.pallas_call(
        kern,
        grid=(t // tm,),
        in_specs=[pl.BlockSpec((tm, d), lambda i: (i, 0)),
                  _const_spec(gpre.shape), _const_spec(win.shape), _const_spec(wout.shape),
                  _const_spec(gpost.shape)],
        out_specs=pl.BlockSpec((tm, d), lambda i: (i, 0)),
        out_shape=jax.ShapeDtypeStruct((t, d), F32),
        scratch_shapes=[pltpu.VMEM((tm, d_ff), BF16)],
        compiler_params=pltpu.CompilerParams(dimension_semantics=("arbitrary",),
                                             vmem_limit_bytes=VMEM_LIMIT),
        name="ffn",
    )(h, gpre, win, wout, gpost)


def kernel(x, mem, norm_mix_pre, w_in, w_pool_mix, pool_scale, w_pool_o, w_sb_o, norm_mem,
           w_mem_kv, w_x_o, w_out, norm_mix_post, norm_ffn_pre, w_ffn_in, w_ffn_out,
           norm_ffn_post):
    b, s, d = x.shape
    depth = w_in.shape[0]
    pool_w = w_pool_o.shape[1]
    sb_w = w_sb_o.shape[1]
    x_w = w_x_o.shape[1]
    sb_heads = sb_w // HEAD_DIM
    x_heads = x_w // HEAD_DIM
    split = pool_w + 3 * sb_w + x_w
    tm = 512
    cm = _suffix_sum_matrix()

    h = x.reshape(b * s, d)
    for l in range(depth):
        row = lambda v: v[l].reshape(1, -1)
        kv = _mem_kv(mem, row(norm_mem), w_mem_kv[l])
        pin, qkv, xq = _in_proj(h, row(norm_mix_pre), w_in[l], tm=tm, pool_w=pool_w,
                                sb_w=sb_w, x_w=x_w)
        ysb = _sb_attn(qkv.reshape(b, s, 3 * sb_w), cm, heads=sb_heads).reshape(b * s, sb_w)
        h = _mix(h, pin, ysb, xq, kv, row(norm_mix_pre), w_in[l], w_pool_mix[l], row(pool_scale),
                 w_pool_o[l], w_sb_o[l], w_x_o[l], w_out[l], row(norm_mix_post), tm=tm, seq=s,
                 x_heads=x_heads, gate_col0=split)
        h = _ffn(h, row(norm_ffn_pre), w_ffn_in[l], w_ffn_out[l], row(norm_ffn_post), tm=tm,
                 chunk=256)
    return h.reshape(b, s, d)
```

```python
import functools

import jax
import jax.numpy as jnp
from jax import lax
from jax.experimental import pallas as pl
from jax.experimental.pallas import tpu as pltpu

F32 = jnp.float32
BF16 = jnp.bfloat16

RMS_EPS = 1e-6
POOL_WINDOWS = (2, 4, 8, 16)
POOL_BLK = 128
LANES = 128
HEAD_DIM = 64
SB_BLOCK = 128
SB_SPAN = 3
SB_TILE = 512
LOG2E = 1.4426950408889634
SB_EXIT_LOG2 = -150.0
SB_MASK_BIAS = -1e30
VMEM_LIMIT = 56 * 1024 * 1024


def _rmsnorm(x, g):
    ms = jnp.mean(x * x, axis=-1, keepdims=True)
    return x * lax.rsqrt(ms + RMS_EPS) * g


def _dot(a, b):
    return jnp.dot(a, b, preferred_element_type=F32)


def _dot_nt(a, b):
    return lax.dot_general(a, b, (((1,), (1,)), ((), ())), preferred_element_type=F32)


def _const_spec(shape):
    nd = len(shape)
    return pl.BlockSpec(shape, lambda *_: (0,) * nd, pipeline_mode=pl.Buffered(1))


def _col_window_spec(rows, col0, ncols):
    return pl.BlockSpec((pl.Element(rows), pl.Element(ncols)), lambda *_: (0, col0),
                        pipeline_mode=pl.Buffered(1))


def _mem_kv_kernel(mem_ref, g_ref, w_ref, o_ref):
    n = _rmsnorm(mem_ref[0], g_ref[...])
    o_ref[0] = _dot(n.astype(BF16), w_ref[...]).astype(BF16)


def _mem_kv(mem, g, w):
    b, m, d = mem.shape
    n_out = w.shape[1]
    return pl.pallas_call(
        _mem_kv_kernel,
        grid=(b,),
        in_specs=[pl.BlockSpec((1, m, d), lambda i: (i, 0, 0)),
                  _const_spec((1, d)),
                  _const_spec((d, n_out))],
        out_specs=pl.BlockSpec((1, m, n_out), lambda i: (i, 0, 0)),
        out_shape=jax.ShapeDtypeStruct((b, m, n_out), BF16),
        compiler_params=pltpu.CompilerParams(dimension_semantics=("arbitrary",)),
        name="mem_kv",
    )(mem, g, w)


def _in_proj_kernel(x_ref, g_ref, w_ref, p_ref, qkv_ref, xq_ref, *, pool_w, sb_w):
    nb = _rmsnorm(x_ref[...], g_ref[...]).astype(BF16)
    scale = 1.0 / (HEAD_DIM ** 0.5)
    o0, o1, o2 = pool_w, pool_w + sb_w, pool_w + 3 * sb_w
    p_ref[...] = _dot(nb, w_ref[:, :o0]).astype(BF16)
    qkv_ref[:, :sb_w] = (_dot(nb, w_ref[:, o0:o1]) * (scale * LOG2E)).astype(BF16)
    qkv_ref[:, sb_w:] = _dot(nb, w_ref[:, o1:o2]).astype(BF16)
    xq_ref[...] = (_dot(nb, w_ref[:, o2:]) * scale).astype(BF16)


def _in_proj(x2, g, w, *, tm, pool_w, sb_w, x_w):
    t, d = x2.shape
    n_in = pool_w + 3 * sb_w + x_w
    kern = functools.partial(_in_proj_kernel, pool_w=pool_w, sb_w=sb_w)
    return pl.pallas_call(
        kern,
        grid=(t // tm,),
        in_specs=[pl.BlockSpec((tm, d), lambda i: (i, 0)),
                  _const_spec((1, d)),
                  _col_window_spec(d, 0, n_in)],
        out_specs=[pl.BlockSpec((tm, pool_w), lambda i: (i, 0)),
                   pl.BlockSpec((tm, 3 * sb_w), lambda i: (i, 0)),
                   pl.BlockSpec((tm, x_w), lambda i: (i, 0))],
        out_shape=[jax.ShapeDtypeStruct((t, pool_w), BF16),
                   jax.ShapeDtypeStruct((t, 3 * sb_w), BF16),
                   jax.ShapeDtypeStruct((t, x_w), BF16)],
        compiler_params=pltpu.CompilerParams(dimension_semantics=("arbitrary",),
                                             vmem_limit_bytes=VMEM_LIMIT),
        name="in_proj",
    )(x2, g, w)


def _sb_kernel(q_ref, k_ref, v_ref, cm_ref, o_ref, qm_ref, z_ref, cat_ref, a_ref, acc_ref,
               carry_ref, *, heads):
    blk = SB_BLOCK
    npair = heads // 2
    n_qblk = q_ref.shape[1] // blk
    iq0 = pl.program_id(1) * n_qblk
    row = lax.broadcasted_iota(jnp.int32, (blk, blk), 0)
    col = lax.broadcasted_iota(jnp.int32, (blk, blk), 1)
    tri = col < row
    lo_half = col < HEAD_DIM

    def load_queries(kq):
        for p in range(npair):
            qp = q_ref[0, kq * blk:(kq + 1) * blk, p * LANES:(p + 1) * LANES]
            zero = jnp.zeros_like(qp)
            qm_ref[p, :blk] = jnp.where(lo_half, qp, zero)
            qm_ref[p, blk:] = jnp.where(lo_half, zero, qp)

    def visit(blocks, diag, fresh, bias=None):
        nb = len(blocks)
        starts = [pl.multiple_of(bk * blk, blk) for bk in blocks]

        def rows_of(ref, p):
            parts = [ref[0, pl.ds(st, blk), p * LANES:(p + 1) * LANES] for st in starts]
            return parts[0] if nb == 1 else jnp.concatenate(parts, axis=0)

        for p in range(npair):
            z = _dot_nt(qm_ref[p], rows_of(k_ref, p))
            for b in range(nb):
                zb = z[:, b * blk:(b + 1) * blk]
                if bias is not None and bias[b] is not None:
                    zb = zb + bias[b]
                z_ref[p, :, b * blk:(b + 1) * blk] = zb
        for h in range(heads):
            p, hh = divmod(h, 2)
            for b in range(nb):
                z = z_ref[p, hh * blk:(hh + 1) * blk, b * blk:(b + 1) * blk]
                sp = jnp.maximum(z, 0.0) + jnp.log2(1.0 + jnp.exp2(-jnp.abs(z)))
                if diag and b == nb - 1:
                    sp = jnp.where(tri, sp, 0.0)
                hi = sp.astype(BF16)
                r0 = (h * SB_SPAN + b) * blk
                cat_ref[r0:r0 + blk, :blk] = hi
                cat_ref[r0:r0 + blk, blk:] = (sp - hi.astype(F32)).astype(BF16)
        mx = None
        for h in range(heads):
            p, hh = divmod(h, 2)
            r0 = h * SB_SPAN * blk
            r = _dot(cat_ref[r0:r0 + nb * blk, :], cm_ref[...])
            c = None if fresh else carry_ref[h]
            parts = [None] * nb
            for b in reversed(range(nb)):
                arg = (z_ref[p, hh * blk:(hh + 1) * blk, b * blk:(b + 1) * blk]
                       + r[b * blk:(b + 1) * blk, :blk])
                if c is not None:
                    arg = arg + c
                a = jnp.exp2(arg)
                if diag and b == nb - 1:
                    a = jnp.where(tri, a, 0.0)
                parts[b] = a.astype(BF16)
                tot = r[b * blk:(b + 1) * blk, blk:]
                c = tot if c is None else c + tot
            carry_ref[h] = c
            a_ref[h * blk:(h + 1) * blk, :nb * blk] = (
                parts[0] if nb == 1 else jnp.concatenate(parts, axis=1))
            mx = c if mx is None else jnp.maximum(mx, c)
        for p in range(npair):
            res = _dot(a_ref[2 * p * blk:2 * (p + 1) * blk, :nb * blk], rows_of(v_ref, p))
            for hh in range(2):
                if fresh:
                    acc_ref[2 * p + hh] = res[hh * blk:(hh + 1) * blk]
                else:
                    acc_ref[2 * p + hh] += res[hh * blk:(hh + 1) * blk]
        return jnp.max(mx)

    def first_visit(iq):
        blocks, bias = [], []
        for back in range(SB_SPAN - 1, -1, -1):
            blocks.append(jnp.maximum(iq - back, 0))
            bias.append(None if back == 0 else jnp.where(iq >= back, 0.0, SB_MASK_BIAS))
        return visit(blocks, True, True, bias)

    def continue_walk(iq, m0):
        def cond(st):
            j, m = st
            return jnp.logical_and(j >= 0, m > SB_EXIT_LOG2)

        def body(st):
            j, _ = st
            return j - 1, visit([j], False, False)

        lax.while_loop(cond, body, (iq - SB_SPAN, m0))

    for kq in range(n_qblk):
        iq = iq0 + kq
        load_queries(kq)
        continue_walk(iq, first_visit(iq))
        for p in range(npair):
            o = jnp.where(lo_half, acc_ref[2 * p], acc_ref[2 * p + 1])
            o_ref[0, kq * blk:(kq + 1) * blk, p * LANES:(p + 1) * LANES] = o.astype(BF16)


def _sb_attn(qkv, *, heads):
    b, s, w3 = qkv.shape
    w = w3 // 3
    blk = SB_BLOCK
    cm = _suffix_sum_matrix()
    kern = functools.partial(_sb_kernel, heads=heads)
    return pl.pallas_call(
        kern,
        grid=(b, s // SB_TILE),
        in_specs=[pl.BlockSpec((1, SB_TILE, w), lambda bi, i: (bi, i, 0)),
                  pl.BlockSpec((1, s, w), lambda bi, i: (bi, 0, 1), pipeline_mode=pl.Buffered(1)),
                  pl.BlockSpec((1, s, w), lambda bi, i: (bi, 0, 2), pipeline_mode=pl.Buffered(1)),
                  _const_spec(cm.shape)],
        out_specs=pl.BlockSpec((1, SB_TILE, w), lambda bi, i: (bi, i, 0)),
        out_shape=jax.ShapeDtypeStruct((b, s, w), BF16),
        scratch_shapes=[pltpu.VMEM((heads // 2, 2 * blk, LANES), BF16),
                        pltpu.VMEM((heads // 2, 2 * blk, SB_SPAN * blk), F32),
                        pltpu.VMEM((heads * SB_SPAN * blk, 2 * blk), BF16),
                        pltpu.VMEM((heads * blk, SB_SPAN * blk), BF16),
                        pltpu.VMEM((heads, blk, LANES), F32),
                        pltpu.VMEM((heads, blk, LANES), F32)],
        compiler_params=pltpu.CompilerParams(dimension_semantics=("arbitrary", "arbitrary"),
                                             vmem_limit_bytes=VMEM_LIMIT),
        name="sb_attn",
    )(qkv, qkv, qkv, cm)


def _suffix_sum_matrix():
    blk = SB_BLOCK
    j = jnp.arange(2 * blk)[:, None] % blk
    s = jnp.arange(2 * blk)[None, :]
    return jnp.where(jnp.logical_or(s >= blk, j >= s), -1.0, 0.0).astype(BF16)


def _mix_kernel(x_ref, pin_ref, halo_ref, ysb_ref, xq_ref, kv_ref, band_ref, gpre_ref, wg_ref,
                wmix_ref, pscale_ref, wpo_ref, wsbo_ref, wxo_ref, wout_ref, gpost_ref, o_ref,
                ext_ref, *, seq, x_heads):
    t = pl.program_id(0)
    tm, d = x_ref.shape
    x = x_ref[...]
    nb = _rmsnorm(x, gpre_ref[...]).astype(BF16)

    tok0 = (t * tm) % seq
    halo = halo_ref[...]
    ext_ref[:POOL_BLK, :] = jnp.where(tok0 == 0, jnp.zeros_like(halo), halo)
    ext_ref[POOL_BLK:, :] = pin_ref[...]
    pos1 = tok0 + 1 + lax.broadcasted_iota(jnp.int32, (tm, 1), 0)
    groups = []
    for g, w in enumerate(POOL_WINDOWS):
        cs = slice(g * LANES, (g + 1) * LANES)
        inv = 1.0 / jnp.minimum(pos1, w).astype(F32)
        chunks = []
        for c in range(tm // POOL_BLK):
            wsum = _dot(band_ref[g], ext_ref[c * POOL_BLK:(c + 2) * POOL_BLK, cs])
            cur = ext_ref[(c + 1) * POOL_BLK:(c + 2) * POOL_BLK, cs].astype(F32)
            chunks.append(wsum * inv[c * POOL_BLK:(c + 1) * POOL_BLK] - cur)
        mixed = jnp.concatenate(chunks, axis=0)
        groups.append(_dot(mixed.astype(BF16), wmix_ref[g]))
    ypool = jnp.concatenate(groups, axis=1) * pscale_ref[...]
    merged = jax.nn.sigmoid(_dot(nb, wg_ref[:, :d])) * _dot(ypool.astype(BF16), wpo_ref[...])

    merged += jax.nn.sigmoid(_dot(nb, wg_ref[:, d:2 * d])) * _dot(ysb_ref[...], wsbo_ref[...])

    m_len = kv_ref.shape[1]
    xw = x_heads * HEAD_DIM
    lane = lax.broadcasted_iota(jnp.int32, (tm, LANES), 1)
    lo_half = lane < HEAD_DIM
    ones = jnp.ones((m_len, LANES), BF16)
    pairs = []
    for p in range(x_heads // 2):
        cs = slice(p * LANES, (p + 1) * LANES)
        xq = xq_ref[:, cs]
        mk = kv_ref[0, :, cs]
        mv1 = jnp.concatenate([kv_ref[0, :, xw + p * LANES:xw + (p + 1) * LANES], ones], axis=1)
        outs = []
        for hh in range(2):
            keep = lo_half if hh == 0 else jnp.logical_not(lo_half)
            sc = _dot_nt(jnp.where(keep, xq, jnp.zeros_like(xq)), mk)
            e = jnp.exp(sc - jnp.max(sc, axis=-1, keepdims=True)).astype(BF16)
            r = _dot(e, mv1)
            outs.append(r[:, :LANES] / r[:, LANES:])
        pairs.append(jnp.where(lo_half, outs[0], outs[1]))
    yx = jnp.concatenate(pairs, axis=1).astype(BF16)
    merged += jax.nn.sigmoid(_dot(nb, wg_ref[:, 2 * d:])) * _dot(yx, wxo_ref[...])

    mo = _dot(merged.astype(BF16), wout_ref[...])
    o_ref[...] = x + _rmsnorm(mo, gpost_ref[...])


def _mix(x2, pin, ysb, xq, kv, gpre, w_in, wmix, pscale, wpo, wsbo, wxo, wout, gpost, *, tm, seq,
         x_heads, gate_col0):
    t, d = x2.shape
    pw = pin.shape[1]
    hb = tm // POOL_BLK
    n_gate = w_in.shape[1] - gate_col0
    band = _pool_band_matrices()
    kern = functools.partial(_mix_kernel, seq=seq, x_heads=x_heads)
    return pl.pallas_call(
        kern,
        grid=(t // tm,),
        in_specs=[pl.BlockSpec((tm, d), lambda i: (i, 0)),
                  pl.BlockSpec((tm, pw), lambda i: (i, 0)),
                  pl.BlockSpec((POOL_BLK, pw), lambda i: (jnp.maximum(i * hb - 1, 0), 0)),
                  pl.BlockSpec((tm, ysb.shape[1]), lambda i: (i, 0)),
                  pl.BlockSpec((tm, xq.shape[1]), lambda i: (i, 0)),
                  pl.BlockSpec((1,) + kv.shape[1:], lambda i: ((i * tm) // seq, 0, 0)),
                  _const_spec(band.shape),
                  _const_spec(gpre.shape), _col_window_spec(d, gate_col0, n_gate),
                  _const_spec(wmix.shape),
                  _const_spec(pscale.shape), _const_spec(wpo.shape), _const_spec(wsbo.shape),
                  _const_spec(wxo.shape), _const_spec(wout.shape), _const_spec(gpost.shape)],
        out_specs=pl.BlockSpec((tm, d), lambda i: (i, 0)),
        out_shape=jax.ShapeDtypeStruct((t, d), F32),
        scratch_shapes=[pltpu.VMEM((tm + POOL_BLK, pw), BF16)],
        compiler_params=pltpu.CompilerParams(dimension_semantics=("arbitrary",),
                                             vmem_limit_bytes=VMEM_LIMIT),
        name="mix",
    )(x2, pin, pin, ysb, xq, kv, band, gpre, w_in, wmix, pscale, wpo, wsbo, wxo, wout, gpost)


def _pool_band_matrices():
    t = jnp.arange(POOL_BLK)[None, :, None] + POOL_BLK
    j = jnp.arange(2 * POOL_BLK)[None, None, :]
    w = jnp.asarray(POOL_WINDOWS)[:, None, None]
    return jnp.where(jnp.logical_and(j <= t, j > t - w), 1.0, 0.0).astype(BF16)


def _ffn_kernel(h_ref, gpre_ref, win_ref, wout_ref, gpost_ref, o_ref, a_ref, *, d_ff, chunk):
    h = h_ref[...]
    nb = _rmsnorm(h, gpre_ref[...]).astype(BF16)
    for c in range(d_ff // chunk):
        g = _dot(nb, win_ref[:, c * chunk:(c + 1) * chunk])
        u = _dot(nb, win_ref[:, d_ff + c * chunk:d_ff + (c + 1) * chunk])
        a_ref[:, c * chunk:(c + 1) * chunk] = (g * jax.nn.sigmoid(g) * u).astype(BF16)
    ff = _dot(a_ref[...], wout_ref[...])
    o_ref[...] = h + _rmsnorm(ff, gpost_ref[...])


def _ffn(h, gpre, win, wout, gpost, *, tm, chunk):
    t, d = h.shape
    d_ff = wout.shape[0]
    kern = functools.partial(_ffn_kernel, d_ff=d_ff, chunk=chunk)
    return pl.pallas_call(
        kern,
        grid=(t // tm,),
        in_specs=[pl.BlockSpec((tm, d), lambda i: (i, 0)),
                  _const_spec(gpre.shape), _const_spec(win.shape), _const_spec(wout.shape),
                  _const_spec(gpost.shape)],
        out_specs=pl.BlockSpec((tm, d), lambda i: (i, 0)),
        out_shape=jax.ShapeDtypeStruct((t, d), F32),
        scratch_shapes=[pltpu.VMEM((tm, d_ff), BF16)],
        compiler_params=pltpu.CompilerParams(dimension_semantics=("arbitrary",),
                                             vmem_limit_bytes=VMEM_LIMIT),
        name="ffn",
    )(h, gpre, win, wout, gpost)


def kernel(x, mem, norm_mix_pre, w_in, w_pool_mix, pool_scale, w_pool_o, w_sb_o, norm_mem,
           w_mem_kv, w_x_o, w_out, norm_mix_post, norm_ffn_pre, w_ffn_in, w_ffn_out,
           norm_ffn_post):
    b, s, d = x.shape
    depth = w_in.shape[0]
    pool_w = w_pool_o.shape[1]
    sb_w = w_sb_o.shape[1]
    x_w = w_x_o.shape[1]
    sb_heads = sb_w // HEAD_DIM
    x_heads = x_w // HEAD_DIM
    split = pool_w + 3 * sb_w + x_w
    tm = 512

    h = x.reshape(b * s, d)
    for l in range(depth):
        row = lambda v: v[l].reshape(1, -1)
        kv = _mem_kv(mem, row(norm_mem), w_mem_kv[l])
        pin, qkv, xq = _in_proj(h, row(norm_mix_pre), w_in[l], tm=tm, pool_w=pool_w,
                                sb_w=sb_w, x_w=x_w)
        ysb = _sb_attn(qkv.reshape(b, s, 3 * sb_w), heads=sb_heads).reshape(b * s, sb_w)
        h = _mix(h, pin, ysb, xq, kv, row(norm_mix_pre), w_in[l], w_pool_mix[l], row(pool_scale),
                 w_pool_o[l], w_sb_o[l], w_x_o[l], w_out[l], row(norm_mix_post), tm=tm, seq=s,
                 x_heads=x_heads, gate_col0=split)
        h = _ffn(h, row(norm_ffn_pre), w_ffn_in[l], w_ffn_out[l], row(norm_ffn_post), tm=tm,
                 chunk=256)
    return h.reshape(b, s, d)
```

```python
import functools

import jax
import jax.numpy as jnp
from jax import lax
from jax.experimental import pallas as pl
from jax.experimental.pallas import tpu as pltpu

F32 = jnp.float32
BF16 = jnp.bfloat16

RMS_EPS = 1e-6
POOL_WINDOWS = (2, 4, 8, 16)
POOL_BLK = 128
LANES = 128
HEAD_DIM = 64
SB_BLOCK = 128
SB_SPAN = 3
SB_TILE = 512
SB_TOP_ROWS = 64
LOG2E = 1.4426950408889634
SB_EXIT_LOG2 = -150.0
SB_MASK_BIAS = -1e30
VMEM_LIMIT = 56 * 1024 * 1024


def _rmsnorm(x, g):
    ms = jnp.mean(x * x, axis=-1, keepdims=True)
    return x * lax.rsqrt(ms + RMS_EPS) * g


def _dot(a, b):
    return jnp.dot(a, b, preferred_element_type=F32)


def _dot_nt(a, b):
    return lax.dot_general(a, b, (((1,), (1,)), ((), ())), preferred_element_type=F32)


def _const_spec(shape):
    nd = len(shape)
    return pl.BlockSpec(shape, lambda *_: (0,) * nd, pipeline_mode=pl.Buffered(1))


def _col_window_spec(rows, col0, ncols):
    return pl.BlockSpec((pl.Element(rows), pl.Element(ncols)), lambda *_: (0, col0),
                        pipeline_mode=pl.Buffered(1))


def _mem_kv_kernel(mem_ref, g_ref, w_ref, o_ref):
    n = _rmsnorm(mem_ref[0], g_ref[...])
    o_ref[0] = _dot(n.astype(BF16), w_ref[...]).astype(BF16)


def _mem_kv(mem, g, w):
    b, m, d = mem.shape
    n_out = w.shape[1]
    return pl.pallas_call(
        _mem_kv_kernel,
        grid=(b,),
        in_specs=[pl.BlockSpec((1, m, d), lambda i: (i, 0, 0)),
                  _const_spec((1, d)),
                  _const_spec((d, n_out))],
        out_specs=pl.BlockSpec((1, m, n_out), lambda i: (i, 0, 0)),
        out_shape=jax.ShapeDtypeStruct((b, m, n_out), BF16),
        compiler_params=pltpu.CompilerParams(dimension_semantics=("arbitrary",)),
        name="mem_kv",
    )(mem, g, w)


def _in_proj_kernel(x_ref, g_ref, w_ref, p_ref, qkv_ref, xq_ref, *, pool_w, sb_w):
    nb = _rmsnorm(x_ref[...], g_ref[...]).astype(BF16)
    scale = 1.0 / (HEAD_DIM ** 0.5)
    o0, o1, o2 = pool_w, pool_w + sb_w, pool_w + 3 * sb_w
    p_ref[...] = _dot(nb, w_ref[:, :o0]).astype(BF16)
    qkv_ref[:, :sb_w] = (_dot(nb, w_ref[:, o0:o1]) * (scale * LOG2E)).astype(BF16)
    qkv_ref[:, sb_w:] = _dot(nb, w_ref[:, o1:o2]).astype(BF16)
    xq_ref[...] = (_dot(nb, w_ref[:, o2:]) * scale).astype(BF16)


def _in_proj(x2, g, w, *, tm, pool_w, sb_w, x_w):
    t, d = x2.shape
    n_in = pool_w + 3 * sb_w + x_w
    kern = functools.partial(_in_proj_kernel, pool_w=pool_w, sb_w=sb_w)
    return pl.pallas_call(
        kern,
        grid=(t // tm,),
        in_specs=[pl.BlockSpec((tm, d), lambda i: (i, 0)),
                  _const_spec((1, d)),
                  _col_window_spec(d, 0, n_in)],
        out_specs=[pl.BlockSpec((tm, pool_w), lambda i: (i, 0)),
                   pl.BlockSpec((tm, 3 * sb_w), lambda i: (i, 0)),
                   pl.BlockSpec((tm, x_w), lambda i: (i, 0))],
        out_shape=[jax.ShapeDtypeStruct((t, pool_w), BF16),
                   jax.ShapeDtypeStruct((t, 3 * sb_w), BF16),
                   jax.ShapeDtypeStruct((t, x_w), BF16)],
        compiler_params=pltpu.CompilerParams(dimension_semantics=("arbitrary",),
                                             vmem_limit_bytes=VMEM_LIMIT),
        name="in_proj",
    )(x2, g, w)


def _sb_kernel(q_ref, k_ref, v_ref, cm_ref, o_ref, qm_ref, z_ref, cat_ref, a_ref, acc_ref,
               carry_ref, *, heads):
    blk = SB_BLOCK
    npair = heads // 2
    n_qblk = q_ref.shape[1] // blk
    iq0 = pl.program_id(1) * n_qblk
    row = lax.broadcasted_iota(jnp.int32, (blk, blk), 0)
    col = lax.broadcasted_iota(jnp.int32, (blk, blk), 1)
    tri = col < row
    lo_half = col < HEAD_DIM

    def load_queries(kq):
        for p in range(npair):
            qp = q_ref[0, kq * blk:(kq + 1) * blk, p * LANES:(p + 1) * LANES]
            zero = jnp.zeros_like(qp)
            qm_ref[p, :blk] = jnp.where(lo_half, qp, zero)
            qm_ref[p, blk:] = jnp.where(lo_half, zero, qp)

    def visit(entries, diag, fresh):
        nb = len(entries)
        assert all((lo, hi) == (0, blk) for _, lo, hi, _ in entries[1:])
        blk0, lo0, hi0, bias0 = entries[0]
        n0 = hi0 - lo0
        partial = n0 != blk
        full = entries[1:] if partial else entries
        first_full = 1 if partial else 0

        def kv_rows(ref, p, blocks):
            parts = [ref[0, pl.ds(pl.multiple_of(bk * blk, blk), blk), p * LANES:(p + 1) * LANES]
                     for bk in blocks]
            return parts[0] if len(parts) == 1 else jnp.concatenate(parts, axis=0)

        def biased(z, bias):
            return z if bias is None else z + bias

        for p in range(npair):
            if full:
                z = _dot_nt(qm_ref[p], kv_rows(k_ref, p, [e[0] for e in full]))
                for i, e in enumerate(full):
                    sl = first_full + i
                    z_ref[p, :, sl * blk:(sl + 1) * blk] = biased(z[:, i * blk:(i + 1) * blk], e[3])
            if partial:
                lhs = jnp.concatenate([qm_ref[p, lo0:hi0], qm_ref[p, blk + lo0:blk + hi0]], axis=0)
                z = biased(_dot_nt(lhs, kv_rows(k_ref, p, [blk0])), bias0)
                z_ref[p, lo0:hi0, :blk] = z[:n0]
                z_ref[p, blk + lo0:blk + hi0, :blk] = z[n0:]

        def rows_of(sl):
            return (lo0, hi0) if (partial and sl == 0) else (0, blk)

        for h in range(heads):
            p, hh = divmod(h, 2)
            for sl in range(nb):
                lo, hi = rows_of(sl)
                z = z_ref[p, hh * blk + lo:hh * blk + hi, sl * blk:(sl + 1) * blk]
                sp = jnp.maximum(z, 0.0) + jnp.log2(1.0 + jnp.exp2(-jnp.abs(z)))
                if diag and sl == nb - 1:
                    sp = jnp.where(tri, sp, 0.0)
                hi16 = sp.astype(BF16)
                r0 = (h * SB_SPAN + sl + 1) * blk - (hi - lo)
                cat_ref[r0:r0 + hi - lo, :blk] = hi16
                cat_ref[r0:r0 + hi - lo, blk:] = (sp - hi16.astype(F32)).astype(BF16)
        mx = None
        for h in range(heads):
            p, hh = divmod(h, 2)
            r_start = (h * SB_SPAN + 1) * blk - n0
            r_end = (h * SB_SPAN + nb) * blk
            r = _dot(cat_ref[r_start:r_end, :], cm_ref[...])
            c = None if fresh else carry_ref[h]
            for sl in reversed(range(nb)):
                lo, hi = rows_of(sl)
                off = 0 if sl == 0 else n0 + (sl - 1) * blk
                arg = (z_ref[p, hh * blk + lo:hh * blk + hi, sl * blk:(sl + 1) * blk]
                       + r[off:off + hi - lo, :blk])
                if c is not None:
                    arg = arg + c[lo:hi]
                a = jnp.exp2(arg)
                if diag and sl == nb - 1:
                    a = jnp.where(tri, a, 0.0)
                a_ref[h * blk + lo:h * blk + hi, sl * blk:(sl + 1) * blk] = a.astype(BF16)
                tot = r[off:off + hi - lo, blk:]
                if c is None:
                    c = tot
                elif (lo, hi) == (0, blk):
                    c = c + tot
                else:
                    pieces = ([c[:lo]] if lo else []) + [c[lo:hi] + tot] + ([c[hi:]] if hi < blk else [])
                    c = jnp.concatenate(pieces, axis=0)
            carry_ref[h] = c
            mx = c if mx is None else jnp.maximum(mx, c)
        for p in range(npair):
            if full:
                res = _dot(a_ref[2 * p * blk:2 * (p + 1) * blk, first_full * blk:nb * blk],
                           kv_rows(v_ref, p, [e[0] for e in full]))
                for hh in range(2):
                    if fresh:
                        acc_ref[2 * p + hh] = res[hh * blk:(hh + 1) * blk]
                    else:
                        acc_ref[2 * p + hh] += res[hh * blk:(hh + 1) * blk]
            if partial:
                lhs = jnp.concatenate([a_ref[2 * p * blk + lo0:2 * p * blk + hi0, :blk],
                                       a_ref[(2 * p + 1) * blk + lo0:(2 * p + 1) * blk + hi0, :blk]],
                                      axis=0)
                res = _dot(lhs, kv_rows(v_ref, p, [blk0]))
                acc_ref[2 * p, lo0:hi0] += res[:n0]
                acc_ref[2 * p + 1, lo0:hi0] += res[n0:]
        return jnp.max(mx), jnp.max(mx[SB_TOP_ROWS:])

    def first_visit(iq):
        entries = []
        for back in range(SB_SPAN - 1, -1, -1):
            hi = SB_TOP_ROWS if back == SB_SPAN - 1 else blk
            bias = None if back == 0 else jnp.where(iq >= back, 0.0, SB_MASK_BIAS)
            entries.append((jnp.maximum(iq - back, 0), 0, hi, bias))
        return visit(entries, True, True)

    def finish_oldest_block(iq):
        back = SB_SPAN - 1
        bias = jnp.where(iq >= back, 0.0, SB_MASK_BIAS)
        return visit([(jnp.maximum(iq - back, 0), SB_TOP_ROWS, blk, bias)], False, False)[0]

    def continue_walk(iq, m0):
        def cond(st):
            j, m = st
            return jnp.logical_and(j >= 0, m > SB_EXIT_LOG2)

        def body(st):
            j, _ = st
            return j - 1, visit([(j, 0, blk, None)], False, False)[0]

        lax.while_loop(cond, body, (iq - SB_SPAN, m0))

    for kq in range(n_qblk):
        iq = iq0 + kq
        load_queries(kq)
        m_all, m_rest = first_visit(iq)
        m_all = lax.cond(m_rest > SB_EXIT_LOG2, lambda: finish_oldest_block(iq), lambda: m_all)
        continue_walk(iq, m_all)
        for p in range(npair):
            o = jnp.where(lo_half, acc_ref[2 * p], acc_ref[2 * p + 1])
            o_ref[0, kq * blk:(kq + 1) * blk, p * LANES:(p + 1) * LANES] = o.astype(BF16)


def _sb_attn(qkv, *, heads):
    b, s, w3 = qkv.shape
    w = w3 // 3
    blk = SB_BLOCK
    cm = _suffix_sum_matrix()
    kern = functools.partial(_sb_kernel, heads=heads)
    return pl.pallas_call(
        kern,
        grid=(b, s // SB_TILE),
        in_specs=[pl.BlockSpec((1, SB_TILE, w), lambda bi, i: (bi, i, 0)),
                  pl.BlockSpec((1, s, w), lambda bi, i: (bi, 0, 1), pipeline_mode=pl.Buffered(1)),
                  pl.BlockSpec((1, s, w), lambda bi, i: (bi, 0, 2), pipeline_mode=pl.Buffered(1)),
                  _const_spec(cm.shape)],
        out_specs=pl.BlockSpec((1, SB_TILE, w), lambda bi, i: (bi, i, 0)),
        out_shape=jax.ShapeDtypeStruct((b, s, w), BF16),
        scratch_shapes=[pltpu.VMEM((heads // 2, 2 * blk, LANES), BF16),
                        pltpu.VMEM((heads // 2, 2 * blk, SB_SPAN * blk), F32),
                        pltpu.VMEM((heads * SB_SPAN * blk, 2 * blk), BF16),
                        pltpu.VMEM((heads * blk, SB_SPAN * blk), BF16),
                        pltpu.VMEM((heads, blk, LANES), F32),
                        pltpu.VMEM((heads, blk, LANES), F32)],
        compiler_params=pltpu.CompilerParams(dimension_semantics=("arbitrary", "arbitrary"),
                                             vmem_limit_bytes=VMEM_LIMIT),
        name="sb_attn",
    )(qkv, qkv, qkv, cm)


def _suffix_sum_matrix():
    blk = SB_BLOCK
    j = jnp.arange(2 * blk)[:, None] % blk
    s = jnp.arange(2 * blk)[None, :]
    return jnp.where(jnp.logical_or(s >= blk, j >= s), -1.0, 0.0).astype(BF16)


def _mix_kernel(x_ref, pin_ref, halo_ref, ysb_ref, xq_ref, kv_ref, band_ref, gpre_ref, wg_ref,
                wmix_ref, pscale_ref, wpo_ref, wsbo_ref, wxo_ref, wout_ref, gpost_ref, o_ref,
                ext_ref, *, seq, x_heads):
    t = pl.program_id(0)
    tm, d = x_ref.shape
    x = x_ref[...]
    nb = _rmsnorm(x, gpre_ref[...]).astype(BF16)

    tok0 = (t * tm) % seq
    halo = halo_ref[...]
    ext_ref[:POOL_BLK, :] = jnp.where(tok0 == 0, jnp.zeros_like(halo), halo)
    ext_ref[POOL_BLK:, :] = pin_ref[...]
    pos1 = tok0 + 1 + lax.broadcasted_iota(jnp.int32, (tm, 1), 0)
    groups = []
    for g, w in enumerate(POOL_WINDOWS):
        cs = slice(g * LANES, (g + 1) * LANES)
        inv = 1.0 / jnp.minimum(pos1, w).astype(F32)
        chunks = []
        for c in range(tm // POOL_BLK):
            wsum = _dot(band_ref[g], ext_ref[c * POOL_BLK:(c + 2) * POOL_BLK, cs])
            cur = ext_ref[(c + 1) * POOL_BLK:(c + 2) * POOL_BLK, cs].astype(F32)
            chunks.append(wsum * inv[c * POOL_BLK:(c + 1) * POOL_BLK] - cur)
        mixed = jnp.concatenate(chunks, axis=0)
        groups.append(_dot(mixed.astype(BF16), wmix_ref[g]))
    ypool = jnp.concatenate(groups, axis=1) * pscale_ref[...]
    merged = jax.nn.sigmoid(_dot(nb, wg_ref[:, :d])) * _dot(ypool.astype(BF16), wpo_ref[...])

    merged += jax.nn.sigmoid(_dot(nb, wg_ref[:, d:2 * d])) * _dot(ysb_ref[...], wsbo_ref[...])

    m_len = kv_ref.shape[1]
    xw = x_heads * HEAD_DIM
    lane = lax.broadcasted_iota(jnp.int32, (tm, LANES), 1)
    lo_half = lane < HEAD_DIM
    ones = jnp.ones((m_len, LANES), BF16)
    pairs = []
    for p in range(x_heads // 2):
        cs = slice(p * LANES, (p + 1) * LANES)
        xq = xq_ref[:, cs]
        mk = kv_ref[0, :, cs]
        mv1 = jnp.concatenate([kv_ref[0, :, xw + p * LANES:xw + (p + 1) * LANES], ones], axis=1)
        outs = []
        for hh in range(2):
            keep = lo_half if hh == 0 else jnp.logical_not(lo_half)
            sc = _dot_nt(jnp.where(keep, xq, jnp.zeros_like(xq)), mk)
            e = jnp.exp(sc - jnp.max(sc, axis=-1, keepdims=True)).astype(BF16)
            r = _dot(e, mv1)
            outs.append(r[:, :LANES] / r[:, LANES:])
        pairs.append(jnp.where(lo_half, outs[0], outs[1]))
    yx = jnp.concatenate(pairs, axis=1).astype(BF16)
    merged += jax.nn.sigmoid(_dot(nb, wg_ref[:, 2 * d:])) * _dot(yx, wxo_ref[...])

    mo = _dot(merged.astype(BF16), wout_ref[...])
    o_ref[...] = x + _rmsnorm(mo, gpost_ref[...])


def _mix(x2, pin, ysb, xq, kv, gpre, w_in, wmix, pscale, wpo, wsbo, wxo, wout, gpost, *, tm, seq,
         x_heads, gate_col0):
    t, d = x2.shape
    pw = pin.shape[1]
    hb = tm // POOL_BLK
    n_gate = w_in.shape[1] - gate_col0
    band = _pool_band_matrices()
    kern = functools.partial(_mix_kernel, seq=seq, x_heads=x_heads)
    return pl.pallas_call(
        kern,
        grid=(t // tm,),
        in_specs=[pl.BlockSpec((tm, d), lambda i: (i, 0)),
                  pl.BlockSpec((tm, pw), lambda i: (i, 0)),
                  pl.BlockSpec((POOL_BLK, pw), lambda i: (jnp.maximum(i * hb - 1, 0), 0)),
                  pl.BlockSpec((tm, ysb.shape[1]), lambda i: (i, 0)),
                  pl.BlockSpec((tm, xq.shape[1]), lambda i: (i, 0)),
                  pl.BlockSpec((1,) + kv.shape[1:], lambda i: ((i * tm) // seq, 0, 0)),
                  _const_spec(band.shape),
                  _const_spec(gpre.shape), _col_window_spec(d, gate_col0, n_gate),
                  _const_spec(wmix.shape),
                  _const_spec(pscale.shape), _const_spec(wpo.shape), _const_spec(wsbo.shape),
                  _const_spec(wxo.shape), _const_spec(wout.shape), _const_spec(gpost.shape)],
        out_specs=pl.BlockSpec((tm, d), lambda i: (i, 0)),
        out_shape=jax.ShapeDtypeStruct((t, d), F32),
        scratch_shapes=[pltpu.VMEM((tm + POOL_BLK, pw), BF16)],
        compiler_params=pltpu.CompilerParams(dimension_semantics=("arbitrary",),
                                             vmem_limit_bytes=VMEM_LIMIT),
        name="mix",
    )(x2, pin, pin, ysb, xq, kv, band, gpre, w_in, wmix, pscale, wpo, wsbo, wxo, wout, gpost)


def _pool_band_matrices():
    t = jnp.arange(POOL_BLK)[None, :, None] + POOL_BLK
    j = jnp.arange(2 * POOL_BLK)[None, None, :]
    w = jnp.asarray(POOL_WINDOWS)[:, None, None]
    return jnp.where(jnp.logical_and(j <= t, j > t - w), 1.0, 0.0).astype(BF16)


def _ffn_kernel(h_ref, gpre_ref, win_ref, wout_ref, gpost_ref, o_ref, a_ref, *, d_ff, chunk):
    h = h_ref[...]
    nb = _rmsnorm(h, gpre_ref[...]).astype(BF16)
    for c in range(d_ff // chunk):
        g = _dot(nb, win_ref[:, c * chunk:(c + 1) * chunk])
        u = _dot(nb, win_ref[:, d_ff + c * chunk:d_ff + (c + 1) * chunk])
        a_ref[:, c * chunk:(c + 1) * chunk] = (g * jax.nn.sigmoid(g) * u).astype(BF16)
    ff = _dot(a_ref[...], wout_ref[...])
    o_ref[...] = h + _rmsnorm(ff, gpost_ref[...])


def _ffn(h, gpre, win, wout, gpost, *, tm, chunk):
    t, d = h.shape
    d_ff = wout.shape[0]
    kern = functools.partial(_ffn_kernel, d_ff=d_ff, chunk=chunk)
    return pl.pallas_call(
        kern,
        grid=(t // tm,),
        in_specs=[pl.BlockSpec((tm, d), lambda i: (i, 0)),
                  _const_spec(gpre.shape), _const_spec(win.shape), _const_spec(wout.shape),
                  _const_spec(gpost.shape)],
        out_specs=pl.BlockSpec((tm, d), lambda i: (i, 0)),
        out_shape=jax.ShapeDtypeStruct((t, d), F32),
        scratch_shapes=[pltpu.VMEM((tm, d_ff), BF16)],
        compiler_params=pltpu.CompilerParams(dimension_semantics=("arbitrary",),
                                             vmem_limit_bytes=VMEM_LIMIT),
        name="ffn",
    )(h, gpre, win, wout, gpost)


def kernel(x, mem, norm_mix_pre, w_in, w_pool_mix, pool_scale, w_pool_o, w_sb_o, norm_mem,
           w_mem_kv, w_x_o, w_out, norm_mix_post, norm_ffn_pre, w_ffn_in, w_ffn_out,
           norm_ffn_post):
    b, s, d = x.shape
    depth = w_in.shape[0]
    pool_w = w_pool_o.shape[1]
    sb_w = w_sb_o.shape[1]
    x_w = w_x_o.shape[1]
    sb_heads = sb_w // HEAD_DIM
    x_heads = x_w // HEAD_DIM
    split = pool_w + 3 * sb_w + x_w
    tm = 512

    h = x.reshape(b * s, d)
    for l in range(depth):
        row = lambda v: v[l].reshape(1, -1)
        kv = _mem_kv(mem, row(norm_mem), w_mem_kv[l])
        pin, qkv, xq = _in_proj(h, row(norm_mix_pre), w_in[l], tm=tm, pool_w=pool_w,
                                sb_w=sb_w, x_w=x_w)
        ysb = _sb_attn(qkv.reshape(b, s, 3 * sb_w), heads=sb_heads).reshape(b * s, sb_w)
        h = _mix(h, pin, ysb, xq, kv, row(norm_mix_pre), w_in[l], w_pool_mix[l], row(pool_scale),
                 w_pool_o[l], w_sb_o[l], w_x_o[l], w_out[l], row(norm_mix_post), tm=tm, seq=s,
                 x_heads=x_heads, gate_col0=split)
        h = _ffn(h, row(norm_ffn_pre), w_ffn_in[l], w_ffn_out[l], row(norm_ffn_post), tm=tm,
                 chunk=256)
    return h.reshape(b, s, d)
```

```python
import functools

import jax
import jax.numpy as jnp
from jax import lax
from jax.experimental import pallas as pl
from jax.experimental.pallas import tpu as pltpu

F32 = jnp.float32
BF16 = jnp.bfloat16

RMS_EPS = 1e-6
POOL_WINDOWS = (2, 4, 8, 16)
POOL_BLK = 128
LANES = 128
HEAD_DIM = 64
SB_BLOCK = 128
SB_SPAN = 3
SB_TILE = 512
SB_TOP_ROWS = 64
LOG2E = 1.4426950408889634
SB_EXIT_LOG2 = -150.0
SB_MASK_BIAS = -1e30
VMEM_LIMIT = 56 * 1024 * 1024


def _rmsnorm(x, g):
    ms = jnp.mean(x * x, axis=-1, keepdims=True)
    return x * lax.rsqrt(ms + RMS_EPS) * g


def _dot(a, b):
    return jnp.dot(a, b, preferred_element_type=F32)


def _dot_nt(a, b):
    return lax.dot_general(a, b, (((1,), (1,)), ((), ())), preferred_element_type=F32)


def _const_spec(shape):
    nd = len(shape)
    return pl.BlockSpec(shape, lambda *_: (0,) * nd, pipeline_mode=pl.Buffered(1))


def _col_window_spec(rows, col0, ncols):
    return pl.BlockSpec((pl.Element(rows), pl.Element(ncols)), lambda *_: (0, col0),
                        pipeline_mode=pl.Buffered(1))


def _mem_kv_kernel(mem_ref, g_ref, w_ref, o_ref):
    n = _rmsnorm(mem_ref[0], g_ref[...])
    o_ref[0] = _dot(n.astype(BF16), w_ref[...]).astype(BF16)


def _mem_kv(mem, g, w):
    b, m, d = mem.shape
    n_out = w.shape[1]
    return pl.pallas_call(
        _mem_kv_kernel,
        grid=(b,),
        in_specs=[pl.BlockSpec((1, m, d), lambda i: (i, 0, 0)),
                  _const_spec((1, d)),
                  _const_spec((d, n_out))],
        out_specs=pl.BlockSpec((1, m, n_out), lambda i: (i, 0, 0)),
        out_shape=jax.ShapeDtypeStruct((b, m, n_out), BF16),
        compiler_params=pltpu.CompilerParams(dimension_semantics=("arbitrary",)),
        name="mem_kv",
    )(mem, g, w)


def _in_proj_kernel(x_ref, g_ref, w_ref, p_ref, qkv_ref, xq_ref, *, pool_w, sb_w):
    nb = _rmsnorm(x_ref[...], g_ref[...]).astype(BF16)
    scale = 1.0 / (HEAD_DIM ** 0.5)
    o0, o1, o2 = pool_w, pool_w + sb_w, pool_w + 3 * sb_w
    p_ref[...] = _dot(nb, w_ref[:, :o0]).astype(BF16)
    qkv_ref[:, :sb_w] = (_dot(nb, w_ref[:, o0:o1]) * (scale * LOG2E)).astype(BF16)
    qkv_ref[:, sb_w:] = _dot(nb, w_ref[:, o1:o2]).astype(BF16)
    xq_ref[...] = (_dot(nb, w_ref[:, o2:]) * scale).astype(BF16)


def _in_proj(x2, g, w, *, tm, pool_w, sb_w, x_w):
    t, d = x2.shape
    n_in = pool_w + 3 * sb_w + x_w
    kern = functools.partial(_in_proj_kernel, pool_w=pool_w, sb_w=sb_w)
    return pl.pallas_call(
        kern,
        grid=(t // tm,),
        in_specs=[pl.BlockSpec((tm, d), lambda i: (i, 0)),
                  _const_spec((1, d)),
                  _col_window_spec(d, 0, n_in)],
        out_specs=[pl.BlockSpec((tm, pool_w), lambda i: (i, 0)),
                   pl.BlockSpec((tm, 3 * sb_w), lambda i: (i, 0)),
                   pl.BlockSpec((tm, x_w), lambda i: (i, 0))],
        out_shape=[jax.ShapeDtypeStruct((t, pool_w), BF16),
                   jax.ShapeDtypeStruct((t, 3 * sb_w), BF16),
                   jax.ShapeDtypeStruct((t, x_w), BF16)],
        compiler_params=pltpu.CompilerParams(dimension_semantics=("arbitrary",),
                                             vmem_limit_bytes=VMEM_LIMIT),
        name="in_proj",
    )(x2, g, w)


def _sb_kernel(q_ref, k_ref, v_ref, cm_ref, o_ref, qm_ref, z_ref, cat_ref, a_ref, acc_ref,
               carry_ref, *, heads):
    blk = SB_BLOCK
    npair = heads // 2
    n_qblk = q_ref.shape[1] // blk
    iq0 = pl.program_id(1) * n_qblk
    row = lax.broadcasted_iota(jnp.int32, (blk, blk), 0)
    col = lax.broadcasted_iota(jnp.int32, (blk, blk), 1)
    tri = col < row
    lo_half = col < HEAD_DIM

    def load_queries(kq):
        for p in range(npair):
            qp = q_ref[0, kq * blk:(kq + 1) * blk, p * LANES:(p + 1) * LANES]
            zero = jnp.zeros_like(qp)
            qm_ref[p, :blk] = jnp.where(lo_half, qp, zero)
            qm_ref[p, blk:] = jnp.where(lo_half, zero, qp)

    def visit(entries, diag, fresh):
        nb = len(entries)
        assert all((lo, hi) == (0, blk) for _, lo, hi, _ in entries[1:])
        blk0, lo0, hi0, bias0 = entries[0]
        n0 = hi0 - lo0
        partial = n0 != blk
        full = entries[1:] if partial else entries
        first_full = 1 if partial else 0

        def kv_rows(ref, p, blocks):
            parts = [ref[0, pl.ds(pl.multiple_of(bk * blk, blk), blk), p * LANES:(p + 1) * LANES]
                     for bk in blocks]
            return parts[0] if len(parts) == 1 else jnp.concatenate(parts, axis=0)

        def biased(z, bias):
            return z if bias is None else z + bias

        for p in range(npair):
            if full:
                z = _dot_nt(qm_ref[p], kv_rows(k_ref, p, [e[0] for e in full]))
                for i, e in enumerate(full):
                    sl = first_full + i
                    z_ref[p, :, sl * blk:(sl + 1) * blk] = biased(z[:, i * blk:(i + 1) * blk], e[3])
            if partial:
                lhs = jnp.concatenate([qm_ref[p, lo0:hi0], qm_ref[p, blk + lo0:blk + hi0]], axis=0)
                z = biased(_dot_nt(lhs, kv_rows(k_ref, p, [blk0])), bias0)
                z_ref[p, lo0:hi0, :blk] = z[:n0]
                z_ref[p, blk + lo0:blk + hi0, :blk] = z[n0:]

        def rows_of(sl):
            return (lo0, hi0) if (partial and sl == 0) else (0, blk)

        for h in range(heads):
            p, hh = divmod(h, 2)
            for sl in range(nb):
                lo, hi = rows_of(sl)
                z = z_ref[p, hh * blk + lo:hh * blk + hi, sl * blk:(sl + 1) * blk]
                sp = jnp.maximum(z, 0.0) + jnp.log2(1.0 + jnp.exp2(-jnp.abs(z)))
                if diag and sl == nb - 1:
                    sp = jnp.where(tri, sp, 0.0)
                r0 = (h * SB_SPAN + sl + 1) * blk - (hi - lo)
                cat_ref[r0:r0 + hi - lo, :] = sp.astype(BF16)
        mx = None
        for h in range(heads):
            p, hh = divmod(h, 2)
            r_start = (h * SB_SPAN + 1) * blk - n0
            r_end = (h * SB_SPAN + nb) * blk
            r = _dot(cat_ref[r_start:r_end, :], cm_ref[...])
            c = None if fresh else carry_ref[h]
            for sl in reversed(range(nb)):
                lo, hi = rows_of(sl)
                off = 0 if sl == 0 else n0 + (sl - 1) * blk
                arg = (z_ref[p, hh * blk + lo:hh * blk + hi, sl * blk:(sl + 1) * blk]
                       + r[off:off + hi - lo, :blk])
                if c is not None:
                    arg = arg + c[lo:hi]
                a = jnp.exp2(arg)
                if diag and sl == nb - 1:
                    a = jnp.where(tri, a, 0.0)
                a_ref[h * blk + lo:h * blk + hi, sl * blk:(sl + 1) * blk] = a.astype(BF16)
                tot = r[off:off + hi - lo, blk:]
                if c is None:
                    c = tot
                elif (lo, hi) == (0, blk):
                    c = c + tot
                else:
                    pieces = ([c[:lo]] if lo else []) + [c[lo:hi] + tot] + ([c[hi:]] if hi < blk else [])
                    c = jnp.concatenate(pieces, axis=0)
            carry_ref[h] = c
            mx = c if mx is None else jnp.maximum(mx, c)
        for p in range(npair):
            if full:
                res = _dot(a_ref[2 * p * blk:2 * (p + 1) * blk, first_full * blk:nb * blk],
                           kv_rows(v_ref, p, [e[0] for e in full]))
                for hh in range(2):
                    if fresh:
                        acc_ref[2 * p + hh] = res[hh * blk:(hh + 1) * blk]
                    else:
                        acc_ref[2 * p + hh] += res[hh * blk:(hh + 1) * blk]
            if partial:
                lhs = jnp.concatenate([a_ref[2 * p * blk + lo0:2 * p * blk + hi0, :blk],
                                       a_ref[(2 * p + 1) * blk + lo0:(2 * p + 1) * blk + hi0, :blk]],
                                      axis=0)
                res = _dot(lhs, kv_rows(v_ref, p, [blk0]))
                acc_ref[2 * p, lo0:hi0] += res[:n0]
                acc_ref[2 * p + 1, lo0:hi0] += res[n0:]
        return jnp.max(mx), jnp.max(mx[SB_TOP_ROWS:])

    def first_visit(iq):
        entries = []
        for back in range(SB_SPAN - 1, -1, -1):
            hi = SB_TOP_ROWS if back == SB_SPAN - 1 else blk
            bias = None if back == 0 else jnp.where(iq >= back, 0.0, SB_MASK_BIAS)
            entries.append((jnp.maximum(iq - back, 0), 0, hi, bias))
        return visit(entries, True, True)

    def finish_oldest_block(iq):
        back = SB_SPAN - 1
        bias = jnp.where(iq >= back, 0.0, SB_MASK_BIAS)
        return visit([(jnp.maximum(iq - back, 0), SB_TOP_ROWS, blk, bias)], False, False)[0]

    def continue_walk(iq, m0):
        def cond(st):
            j, m = st
            return jnp.logical_and(j >= 0, m > SB_EXIT_LOG2)

        def body(st):
            j, _ = st
            return j - 1, visit([(j, 0, blk, None)], False, False)[0]

        lax.while_loop(cond, body, (iq - SB_SPAN, m0))

    for kq in range(n_qblk):
        iq = iq0 + kq
        load_queries(kq)
        m_all, m_rest = first_visit(iq)
        m_all = lax.cond(m_rest > SB_EXIT_LOG2, lambda: finish_oldest_block(iq), lambda: m_all)
        continue_walk(iq, m_all)
        for p in range(npair):
            o = jnp.where(lo_half, acc_ref[2 * p], acc_ref[2 * p + 1])
            o_ref[0, kq * blk:(kq + 1) * blk, p * LANES:(p + 1) * LANES] = o.astype(BF16)


def _sb_attn(qkv, *, heads):
    b, s, w3 = qkv.shape
    w = w3 // 3
    blk = SB_BLOCK
    cm = _suffix_sum_matrix()
    kern = functools.partial(_sb_kernel, heads=heads)
    return pl.pallas_call(
        kern,
        grid=(b, s // SB_TILE),
        in_specs=[pl.BlockSpec((1, SB_TILE, w), lambda bi, i: (bi, i, 0)),
                  pl.BlockSpec((1, s, w), lambda bi, i: (bi, 0, 1), pipeline_mode=pl.Buffered(1)),
                  pl.BlockSpec((1, s, w), lambda bi, i: (bi, 0, 2), pipeline_mode=pl.Buffered(1)),
                  _const_spec(cm.shape)],
        out_specs=pl.BlockSpec((1, SB_TILE, w), lambda bi, i: (bi, i, 0)),
        out_shape=jax.ShapeDtypeStruct((b, s, w), BF16),
        scratch_shapes=[pltpu.VMEM((heads // 2, 2 * blk, LANES), BF16),
                        pltpu.VMEM((heads // 2, 2 * blk, SB_SPAN * blk), F32),
                        pltpu.VMEM((heads * SB_SPAN * blk, blk), BF16),
                        pltpu.VMEM((heads * blk, SB_SPAN * blk), BF16),
                        pltpu.VMEM((heads, blk, LANES), F32),
                        pltpu.VMEM((heads, blk, LANES), F32)],
        compiler_params=pltpu.CompilerParams(dimension_semantics=("arbitrary", "arbitrary"),
                                             vmem_limit_bytes=VMEM_LIMIT),
        name="sb_attn",
    )(qkv, qkv, qkv, cm)


def _suffix_sum_matrix():
    blk = SB_BLOCK
    j = jnp.arange(blk)[:, None]
    s = jnp.arange(2 * blk)[None, :]
    return jnp.where(jnp.logical_or(s >= blk, j >= s), -1.0, 0.0).astype(BF16)


def _mix_kernel(x_ref, pin_ref, halo_ref, ysb_ref, xq_ref, kv_ref, band_ref, gpre_ref, wg_ref,
                wmix_ref, pscale_ref, wpo_ref, wsbo_ref, wxo_ref, wout_ref, gpost_ref, o_ref,
                ext_ref, *, seq, x_heads):
    t = pl.program_id(0)
    tm, d = x_ref.shape
    x = x_ref[...]
    nb = _rmsnorm(x, gpre_ref[...]).astype(BF16)

    tok0 = (t * tm) % seq
    halo = halo_ref[...]
    ext_ref[:POOL_BLK, :] = jnp.where(tok0 == 0, jnp.zeros_like(halo), halo)
    ext_ref[POOL_BLK:, :] = pin_ref[...]
    pos1 = tok0 + 1 + lax.broadcasted_iota(jnp.int32, (tm, 1), 0)
    groups = []
    for g, w in enumerate(POOL_WINDOWS):
        cs = slice(g * LANES, (g + 1) * LANES)
        inv = 1.0 / jnp.minimum(pos1, w).astype(F32)
        chunks = []
        for c in range(tm // POOL_BLK):
            wsum = _dot(band_ref[g], ext_ref[c * POOL_BLK:(c + 2) * POOL_BLK, cs])
            cur = ext_ref[(c + 1) * POOL_BLK:(c + 2) * POOL_BLK, cs].astype(F32)
            chunks.append(wsum * inv[c * POOL_BLK:(c + 1) * POOL_BLK] - cur)
        mixed = jnp.concatenate(chunks, axis=0)
        groups.append(_dot(mixed.astype(BF16), wmix_ref[g]))
    ypool = jnp.concatenate(groups, axis=1) * pscale_ref[...]
    merged = jax.nn.sigmoid(_dot(nb, wg_ref[:, :d])) * _dot(ypool.astype(BF16), wpo_ref[...])

    merged += jax.nn.sigmoid(_dot(nb, wg_ref[:, d:2 * d])) * _dot(ysb_ref[...], wsbo_ref[...])

    m_len = kv_ref.shape[1]
    xw = x_heads * HEAD_DIM
    lane = lax.broadcasted_iota(jnp.int32, (tm, LANES), 1)
    lo_half = lane < HEAD_DIM
    ones = jnp.ones((m_len, LANES), BF16)
    pairs = []
    for p in range(x_heads // 2):
        cs = slice(p * LANES, (p + 1) * LANES)
        xq = xq_ref[:, cs]
        mk = kv_ref[0, :, cs]
        mv1 = jnp.concatenate([kv_ref[0, :, xw + p * LANES:xw + (p + 1) * LANES], ones], axis=1)
        outs = []
        for hh in range(2):
            keep = lo_half if hh == 0 else jnp.logical_not(lo_half)
            sc = _dot_nt(jnp.where(keep, xq, jnp.zeros_like(xq)), mk)
            e = jnp.exp(sc - jnp.max(sc, axis=-1, keepdims=True)).astype(BF16)
            r = _dot(e, mv1)
            outs.append(r[:, :LANES] / r[:, LANES:])
        pairs.append(jnp.where(lo_half, outs[0], outs[1]))
    yx = jnp.concatenate(pairs, axis=1).astype(BF16)
    merged += jax.nn.sigmoid(_dot(nb, wg_ref[:, 2 * d:])) * _dot(yx, wxo_ref[...])

    mo = _dot(merged.astype(BF16), wout_ref[...])
    o_ref[...] = x + _rmsnorm(mo, gpost_ref[...])


def _mix(x2, pin, ysb, xq, kv, gpre, w_in, wmix, pscale, wpo, wsbo, wxo, wout, gpost, *, tm, seq,
         x_heads, gate_col0):
    t, d = x2.shape
    pw = pin.shape[1]
    hb = tm // POOL_BLK
    n_gate = w_in.shape[1] - gate_col0
    band = _pool_band_matrices()
    kern = functools.partial(_mix_kernel, seq=seq, x_heads=x_heads)
    return pl.pallas_call(
        kern,
        grid=(t // tm,),
        in_specs=[pl.BlockSpec((tm, d), lambda i: (i, 0)),
                  pl.BlockSpec((tm, pw), lambda i: (i, 0)),
                  pl.BlockSpec((POOL_BLK, pw), lambda i: (jnp.maximum(i * hb - 1, 0), 0)),
                  pl.BlockSpec((tm, ysb.shape[1]), lambda i: (i, 0)),
                  pl.BlockSpec((tm, xq.shape[1]), lambda i: (i, 0)),
                  pl.BlockSpec((1,) + kv.shape[1:], lambda i: ((i * tm) // seq, 0, 0)),
                  _const_spec(band.shape),
                  _const_spec(gpre.shape), _col_window_spec(d, gate_col0, n_gate),
                  _const_spec(wmix.shape),
                  _const_spec(pscale.shape), _const_spec(wpo.shape), _const_spec(wsbo.shape),
                  _const_spec(wxo.shape), _const_spec(wout.shape), _const_spec(gpost.shape)],
        out_specs=pl.BlockSpec((tm, d), lambda i: (i, 0)),
        out_shape=jax.ShapeDtypeStruct((t, d), F32),
        scratch_shapes=[pltpu.VMEM((tm + POOL_BLK, pw), BF16)],
        compiler_params=pltpu.CompilerParams(dimension_semantics=("arbitrary",),
                                             vmem_limit_bytes=VMEM_LIMIT),
        name="mix",
    )(x2, pin, pin, ysb, xq, kv, band, gpre, w_in, wmix, pscale, wpo, wsbo, wxo, wout, gpost)


def _pool_band_matrices():
    t = jnp.arange(POOL_BLK)[None, :, None] + POOL_BLK
    j = jnp.arange(2 * POOL_BLK)[None, None, :]
    w = jnp.asarray(POOL_WINDOWS)[:, None, None]
    return jnp.where(jnp.logical_and(j <= t, j > t - w), 1.0, 0.0).astype(BF16)


def _ffn_kernel(h_ref, gpre_ref, win_ref, wout_ref, gpost_ref, o_ref, a_ref, *, d_ff, chunk):
    h = h_ref[...]
    nb = _rmsnorm(h, gpre_ref[...]).astype(BF16)
    for c in range(d_ff // chunk):
        g = _dot(nb, win_ref[:, c * chunk:(c + 1) * chunk])
        u = _dot(nb, win_ref[:, d_ff + c * chunk:d_ff + (c + 1) * chunk])
        a_ref[:, c * chunk:(c + 1) * chunk] = (g * jax.nn.sigmoid(g) * u).astype(BF16)
    ff = _dot(a_ref[...], wout_ref[...])
    o_ref[...] = h + _rmsnorm(ff, gpost_ref[...])


def _ffn(h, gpre, win, wout, gpost, *, tm, chunk):
    t, d = h.shape
    d_ff = wout.shape[0]
    kern = functools.partial(_ffn_kernel, d_ff=d_ff, chunk=chunk)
    return pl.pallas_call(
        kern,
        grid=(t // tm,),
        in_specs=[pl.BlockSpec((tm, d), lambda i: (i, 0)),
                  _const_spec(gpre.shape), _const_spec(win.shape), _const_spec(wout.shape),
                  _const_spec(gpost.shape)],
        out_specs=pl.BlockSpec((tm, d), lambda i: (i, 0)),
        out_shape=jax.ShapeDtypeStruct((t, d), F32),
        scratch_shapes=[pltpu.VMEM((tm, d_ff), BF16)],
        compiler_params=pltpu.CompilerParams(dimension_semantics=("arbitrary",),
                                             vmem_limit_bytes=VMEM_LIMIT),
        name="ffn",
    )(h, gpre, win, wout, gpost)


def kernel(x, mem, norm_mix_pre, w_in, w_pool_mix, pool_scale, w_pool_o, w_sb_o, norm_mem,
           w_mem_kv, w_x_o, w_out, norm_mix_post, norm_ffn_pre, w_ffn_in, w_ffn_out,
           norm_ffn_post):
    b, s, d = x.shape
    depth = w_in.shape[0]
    pool_w = w_pool_o.shape[1]
    sb_w = w_sb_o.shape[1]
    x_w = w_x_o.shape[1]
    sb_heads = sb_w // HEAD_DIM
    x_heads = x_w // HEAD_DIM
    split = pool_w + 3 * sb_w + x_w
    tm = 512

    h = x.reshape(b * s, d)
    for l in range(depth):
        row = lambda v: v[l].reshape(1, -1)
        kv = _mem_kv(mem, row(norm_mem), w_mem_kv[l])
        pin, qkv, xq = _in_proj(h, row(norm_mix_pre), w_in[l], tm=tm, pool_w=pool_w,
                                sb_w=sb_w, x_w=x_w)
        ysb = _sb_attn(qkv.reshape(b, s, 3 * sb_w), heads=sb_heads).reshape(b * s, sb_w)
        h = _mix(h, pin, ysb, xq, kv, row(norm_mix_pre), w_in[l], w_pool_mix[l], row(pool_scale),
                 w_pool_o[l], w_sb_o[l], w_x_o[l], w_out[l], row(norm_mix_post), tm=tm, seq=s,
                 x_heads=x_heads, gate_col0=split)
        h = _ffn(h, row(norm_ffn_pre), w_ffn_in[l], w_ffn_out[l], row(norm_ffn_post), tm=tm,
                 chunk=256)
    return h.reshape(b, s, d)
```

```python
import functools

import jax
import jax.numpy as jnp
from jax import lax
from jax.experimental import pallas as pl
from jax.experimental.pallas import tpu as pltpu

F32 = jnp.float32
BF16 = jnp.bfloat16

RMS_EPS = 1e-6
POOL_WINDOWS = (2, 4, 8, 16)
POOL_BLK = 128
LANES = 128
HEAD_DIM = 64
SB_BLOCK = 128
SB_SPAN = 3
SB_TILE = 512
SB_TOP_ROWS = 64
LOG2E = 1.4426950408889634
SB_EXIT_LOG2 = -150.0
SB_MASK_BIAS = -1e30
VMEM_LIMIT = 56 * 1024 * 1024


def _rmsnorm(x, g):
    ms = jnp.mean(x * x, axis=-1, keepdims=True)
    return x * lax.rsqrt(ms + RMS_EPS) * g


def _dot(a, b):
    return jnp.dot(a, b, preferred_element_type=F32)


def _dot_nt(a, b):
    return lax.dot_general(a, b, (((1,), (1,)), ((), ())), preferred_element_type=F32)


def _const_spec(shape):
    nd = len(shape)
    return pl.BlockSpec(shape, lambda *_: (0,) * nd, pipeline_mode=pl.Buffered(1))


def _col_window_spec(rows, col0, ncols):
    return pl.BlockSpec((pl.Element(rows), pl.Element(ncols)), lambda *_: (0, col0),
                        pipeline_mode=pl.Buffered(1))


def _mem_kv_kernel(mem_ref, g_ref, w_ref, o_ref):
    n = _rmsnorm(mem_ref[0], g_ref[...])
    o_ref[0] = _dot(n.astype(BF16), w_ref[...]).astype(BF16)


def _mem_kv(mem, g, w):
    b, m, d = mem.shape
    n_out = w.shape[1]
    return pl.pallas_call(
        _mem_kv_kernel,
        grid=(b,),
        in_specs=[pl.BlockSpec((1, m, d), lambda i: (i, 0, 0)),
                  _const_spec((1, d)),
                  _const_spec((d, n_out))],
        out_specs=pl.BlockSpec((1, m, n_out), lambda i: (i, 0, 0)),
        out_shape=jax.ShapeDtypeStruct((b, m, n_out), BF16),
        compiler_params=pltpu.CompilerParams(dimension_semantics=("arbitrary",)),
        name="mem_kv",
    )(mem, g, w)


def _in_proj_kernel(x_ref, g_ref, w_ref, p_ref, qkv_ref, xq_ref, *, pool_w, sb_w):
    nb = _rmsnorm(x_ref[...], g_ref[...]).astype(BF16)
    scale = 1.0 / (HEAD_DIM ** 0.5)
    o0, o1, o2 = pool_w, pool_w + sb_w, pool_w + 3 * sb_w
    p_ref[...] = _dot(nb, w_ref[:, :o0]).astype(BF16)
    qkv_ref[:, :sb_w] = (_dot(nb, w_ref[:, o0:o1]) * (scale * LOG2E)).astype(BF16)
    qkv_ref[:, sb_w:] = _dot(nb, w_ref[:, o1:o2]).astype(BF16)
    xq_ref[...] = (_dot(nb, w_ref[:, o2:]) * scale).astype(BF16)


def _in_proj(x2, g, w, *, tm, pool_w, sb_w, x_w):
    t, d = x2.shape
    n_in = pool_w + 3 * sb_w + x_w
    kern = functools.partial(_in_proj_kernel, pool_w=pool_w, sb_w=sb_w)
    return pl.pallas_call(
        kern,
        grid=(t // tm,),
        in_specs=[pl.BlockSpec((tm, d), lambda i: (i, 0)),
                  _const_spec((1, d)),
                  _col_window_spec(d, 0, n_in)],
        out_specs=[pl.BlockSpec((tm, pool_w), lambda i: (i, 0)),
                   pl.BlockSpec((tm, 3 * sb_w), lambda i: (i, 0)),
                   pl.BlockSpec((tm, x_w), lambda i: (i, 0))],
        out_shape=[jax.ShapeDtypeStruct((t, pool_w), BF16),
                   jax.ShapeDtypeStruct((t, 3 * sb_w), BF16),
                   jax.ShapeDtypeStruct((t, x_w), BF16)],
        compiler_params=pltpu.CompilerParams(dimension_semantics=("arbitrary",),
                                             vmem_limit_bytes=VMEM_LIMIT),
        name="in_proj",
    )(x2, g, w)


def _sb_kernel(q_ref, k_ref, v_ref, cm_ref, o_ref, qm_all_ref, z_all_ref, cat_all_ref, a_all_ref,
               acc_all_ref, carry_all_ref, *, heads):
    blk = SB_BLOCK
    npair = heads // 2
    n_qblk = q_ref.shape[1] // blk
    iq0 = pl.program_id(1) * n_qblk
    row = lax.broadcasted_iota(jnp.int32, (blk, blk), 0)
    col = lax.broadcasted_iota(jnp.int32, (blk, blk), 1)
    tri = col < row
    lo_half = col < HEAD_DIM

    def load_queries(kq):
        for p in range(npair):
            qp = q_ref[0, kq * blk:(kq + 1) * blk, p * LANES:(p + 1) * LANES]
            zero = jnp.zeros_like(qp)
            qm_all_ref[kq, p, :blk] = jnp.where(lo_half, qp, zero)
            qm_all_ref[kq, p, blk:] = jnp.where(lo_half, zero, qp)

    def visit(kq, entries, diag, fresh):
        qm_ref, z_ref, cat_ref, a_ref, acc_ref, carry_ref = (
            r.at[kq] for r in (qm_all_ref, z_all_ref, cat_all_ref, a_all_ref, acc_all_ref,
                               carry_all_ref))
        nb = len(entries)
        assert all((lo, hi) == (0, blk) for _, lo, hi, _ in entries[1:])
        blk0, lo0, hi0, bias0 = entries[0]
        n0 = hi0 - lo0
        partial = n0 != blk
        full = entries[1:] if partial else entries
        first_full = 1 if partial else 0

        def kv_rows(ref, p, blocks):
            parts = [ref[0, pl.ds(pl.multiple_of(bk * blk, blk), blk), p * LANES:(p + 1) * LANES]
                     for bk in blocks]
            return parts[0] if len(parts) == 1 else jnp.concatenate(parts, axis=0)

        def biased(z, bias):
            return z if bias is None else z + bias

        for p in range(npair):
            if full:
                z = _dot_nt(qm_ref[p], kv_rows(k_ref, p, [e[0] for e in full]))
                for i, e in enumerate(full):
                    sl = first_full + i
                    z_ref[p, :, sl * blk:(sl + 1) * blk] = biased(z[:, i * blk:(i + 1) * blk], e[3])
            if partial:
                lhs = jnp.concatenate([qm_ref[p, lo0:hi0], qm_ref[p, blk + lo0:blk + hi0]], axis=0)
                z = biased(_dot_nt(lhs, kv_rows(k_ref, p, [blk0])), bias0)
                z_ref[p, lo0:hi0, :blk] = z[:n0]
                z_ref[p, blk + lo0:blk + hi0, :blk] = z[n0:]

        def rows_of(sl):
            return (lo0, hi0) if (partial and sl == 0) else (0, blk)

        for h in range(heads):
            p, hh = divmod(h, 2)
            for sl in range(nb):
                lo, hi = rows_of(sl)
                z = z_ref[p, hh * blk + lo:hh * blk + hi, sl * blk:(sl + 1) * blk]
                sp = jnp.maximum(z, 0.0) + jnp.log2(1.0 + jnp.exp2(-jnp.abs(z)))
                if diag and sl == nb - 1:
                    sp = jnp.where(tri, sp, 0.0)
                r0 = (h * SB_SPAN + sl + 1) * blk - (hi - lo)
                cat_ref[r0:r0 + hi - lo, :] = sp.astype(BF16)
        mx = None
        for h in range(heads):
            p, hh = divmod(h, 2)
            r_start = (h * SB_SPAN + 1) * blk - n0
            r_end = (h * SB_SPAN + nb) * blk
            r = _dot(cat_ref[r_start:r_end, :], cm_ref[...])
            c = None if fresh else carry_ref[h]
            for sl in reversed(range(nb)):
                lo, hi = rows_of(sl)
                off = 0 if sl == 0 else n0 + (sl - 1) * blk
                arg = (z_ref[p, hh * blk + lo:hh * blk + hi, sl * blk:(sl + 1) * blk]
                       + r[off:off + hi - lo, :blk])
                if c is not None:
                    arg = arg + c[lo:hi]
                a = jnp.exp2(arg)
                if diag and sl == nb - 1:
                    a = jnp.where(tri, a, 0.0)
                a_ref[h * blk + lo:h * blk + hi, sl * blk:(sl + 1) * blk] = a.astype(BF16)
                tot = r[off:off + hi - lo, blk:]
                if c is None:
                    c = tot
                elif (lo, hi) == (0, blk):
                    c = c + tot
                else:
                    pieces = ([c[:lo]] if lo else []) + [c[lo:hi] + tot] + ([c[hi:]] if hi < blk else [])
                    c = jnp.concatenate(pieces, axis=0)
            carry_ref[h] = c
            mx = c if mx is None else jnp.maximum(mx, c)
        for p in range(npair):
            if full:
                res = _dot(a_ref[2 * p * blk:2 * (p + 1) * blk, first_full * blk:nb * blk],
                           kv_rows(v_ref, p, [e[0] for e in full]))
                for hh in range(2):
                    if fresh:
                        acc_ref[2 * p + hh] = res[hh * blk:(hh + 1) * blk]
                    else:
                        acc_ref[2 * p + hh] += res[hh * blk:(hh + 1) * blk]
            if partial:
                lhs = jnp.concatenate([a_ref[2 * p * blk + lo0:2 * p * blk + hi0, :blk],
                                       a_ref[(2 * p + 1) * blk + lo0:(2 * p + 1) * blk + hi0, :blk]],
                                      axis=0)
                res = _dot(lhs, kv_rows(v_ref, p, [blk0]))
                acc_ref[2 * p, lo0:hi0] += res[:n0]
                acc_ref[2 * p + 1, lo0:hi0] += res[n0:]
        return jnp.max(mx), jnp.max(mx[SB_TOP_ROWS:])

    def first_visit(kq, iq):
        entries = []
        for back in range(SB_SPAN - 1, -1, -1):
            hi = SB_TOP_ROWS if back == SB_SPAN - 1 else blk
            bias = None if back == 0 else jnp.where(iq >= back, 0.0, SB_MASK_BIAS)
            entries.append((jnp.maximum(iq - back, 0), 0, hi, bias))
        return visit(kq, entries, True, True)

    def finish_oldest_block(kq, iq):
        back = SB_SPAN - 1
        bias = jnp.where(iq >= back, 0.0, SB_MASK_BIAS)
        return visit(kq, [(jnp.maximum(iq - back, 0), SB_TOP_ROWS, blk, bias)], False, False)[0]

    def continue_walk(kq, iq, m0):
        def cond(st):
            j, m = st
            return jnp.logical_and(j >= 0, m > SB_EXIT_LOG2)

        def body(st):
            j, _ = st
            return j - 1, visit(kq, [(j, 0, blk, None)], False, False)[0]

        lax.while_loop(cond, body, (iq - SB_SPAN, m0))

    carries = []
    for kq in range(n_qblk):
        load_queries(kq)
        carries.append(first_visit(kq, iq0 + kq))
    for kq in range(n_qblk):
        iq = iq0 + kq
        m_all, m_rest = carries[kq]
        m_all = lax.cond(m_rest > SB_EXIT_LOG2, lambda: finish_oldest_block(kq, iq), lambda: m_all)
        continue_walk(kq, iq, m_all)
        for p in range(npair):
            o = jnp.where(lo_half, acc_all_ref[kq, 2 * p], acc_all_ref[kq, 2 * p + 1])
            o_ref[0, kq * blk:(kq + 1) * blk, p * LANES:(p + 1) * LANES] = o.astype(BF16)


def _sb_attn(qkv, *, heads):
    b, s, w3 = qkv.shape
    w = w3 // 3
    blk = SB_BLOCK
    nq = SB_TILE // blk
    cm = _suffix_sum_matrix()
    kern = functools.partial(_sb_kernel, heads=heads)
    return pl.pallas_call(
        kern,
        grid=(b, s // SB_TILE),
        in_specs=[pl.BlockSpec((1, SB_TILE, w), lambda bi, i: (bi, i, 0)),
                  pl.BlockSpec((1, s, w), lambda bi, i: (bi, 0, 1), pipeline_mode=pl.Buffered(1)),
                  pl.BlockSpec((1, s, w), lambda bi, i: (bi, 0, 2), pipeline_mode=pl.Buffered(1)),
                  _const_spec(cm.shape)],
        out_specs=pl.BlockSpec((1, SB_TILE, w), lambda bi, i: (bi, i, 0)),
        out_shape=jax.ShapeDtypeStruct((b, s, w), BF16),
        scratch_shapes=[pltpu.VMEM((nq, heads // 2, 2 * blk, LANES), BF16),
                        pltpu.VMEM((nq, heads // 2, 2 * blk, SB_SPAN * blk), F32),
                        pltpu.VMEM((nq, heads * SB_SPAN * blk, blk), BF16),
                        pltpu.VMEM((nq, heads * blk, SB_SPAN * blk), BF16),
                        pltpu.VMEM((nq, heads, blk, LANES), F32),
                        pltpu.VMEM((nq, heads, blk, LANES), F32)],
        compiler_params=pltpu.CompilerParams(dimension_semantics=("arbitrary", "arbitrary"),
                                             vmem_limit_bytes=VMEM_LIMIT),
        name="sb_attn",
    )(qkv, qkv, qkv, cm)


def _suffix_sum_matrix():
    blk = SB_BLOCK
    j = jnp.arange(blk)[:, None]
    s = jnp.arange(2 * blk)[None, :]
    return jnp.where(jnp.logical_or(s >= blk, j >= s), -1.0, 0.0).astype(BF16)


def _mix_kernel(x_ref, pin_ref, halo_ref, ysb_ref, xq_ref, kv_ref, band_ref, gpre_ref, wg_ref,
                wmix_ref, pscale_ref, wpo_ref, wsbo_ref, wxo_ref, wout_ref, gpost_ref, o_ref,
                ext_ref, *, seq, x_heads):
    t = pl.program_id(0)
    tm, d = x_ref.shape
    x = x_ref[...]
    nb = _rmsnorm(x, gpre_ref[...]).astype(BF16)

    tok0 = (t * tm) % seq
    halo = halo_ref[...]
    ext_ref[:POOL_BLK, :] = jnp.where(tok0 == 0, jnp.zeros_like(halo), halo)
    ext_ref[POOL_BLK:, :] = pin_ref[...]
    pos1 = tok0 + 1 + lax.broadcasted_iota(jnp.int32, (tm, 1), 0)
    groups = []
    for g, w in enumerate(POOL_WINDOWS):
        cs = slice(g * LANES, (g + 1) * LANES)
        inv = 1.0 / jnp.minimum(pos1, w).astype(F32)
        chunks = []
        for c in range(tm // POOL_BLK):
            wsum = _dot(band_ref[g], ext_ref[c * POOL_BLK:(c + 2) * POOL_BLK, cs])
            cur = ext_ref[(c + 1) * POOL_BLK:(c + 2) * POOL_BLK, cs].astype(F32)
            chunks.append(wsum * inv[c * POOL_BLK:(c + 1) * POOL_BLK] - cur)
        mixed = jnp.concatenate(chunks, axis=0)
        groups.append(_dot(mixed.astype(BF16), wmix_ref[g]))
    ypool = jnp.concatenate(groups, axis=1) * pscale_ref[...]
    merged = jax.nn.sigmoid(_dot(nb, wg_ref[:, :d])) * _dot(ypool.astype(BF16), wpo_ref[...])

    merged += jax.nn.sigmoid(_dot(nb, wg_ref[:, d:2 * d])) * _dot(ysb_ref[...], wsbo_ref[...])

    m_len = kv_ref.shape[1]
    xw = x_heads * HEAD_DIM
    lane = lax.broadcasted_iota(jnp.int32, (tm, LANES), 1)
    lo_half = lane < HEAD_DIM
    ones = jnp.ones((m_len, LANES), BF16)
    pairs = []
    for p in range(x_heads // 2):
        cs = slice(p * LANES, (p + 1) * LANES)
        xq = xq_ref[:, cs]
        mk = kv_ref[0, :, cs]
        mv1 = jnp.concatenate([kv_ref[0, :, xw + p * LANES:xw + (p + 1) * LANES], ones], axis=1)
        outs = []
        for hh in range(2):
            keep = lo_half if hh == 0 else jnp.logical_not(lo_half)
            sc = _dot_nt(jnp.where(keep, xq, jnp.zeros_like(xq)), mk)
            e = jnp.exp(sc - jnp.max(sc, axis=-1, keepdims=True)).astype(BF16)
            r = _dot(e, mv1)
            outs.append(r[:, :LANES] / r[:, LANES:])
        pairs.append(jnp.where(lo_half, outs[0], outs[1]))
    yx = jnp.concatenate(pairs, axis=1).astype(BF16)
    merged += jax.nn.sigmoid(_dot(nb, wg_ref[:, 2 * d:])) * _dot(yx, wxo_ref[...])

    mo = _dot(merged.astype(BF16), wout_ref[...])
    o_ref[...] = x + _rmsnorm(mo, gpost_ref[...])


def _mix(x2, pin, ysb, xq, kv, gpre, w_in, wmix, pscale, wpo, wsbo, wxo, wout, gpost, *, tm, seq,
         x_heads, gate_col0):
    t, d = x2.shape
    pw = pin.shape[1]
    hb = tm // POOL_BLK
    n_gate = w_in.shape[1] - gate_col0
    band = _pool_band_matrices()
    kern = functools.partial(_mix_kernel, seq=seq, x_heads=x_heads)
    return pl.pallas_call(
        kern,
        grid=(t // tm,),
        in_specs=[pl.BlockSpec((tm, d), lambda i: (i, 0)),
                  pl.BlockSpec((tm, pw), lambda i: (i, 0)),
                  pl.BlockSpec((POOL_BLK, pw), lambda i: (jnp.maximum(i * hb - 1, 0), 0)),
                  pl.BlockSpec((tm, ysb.shape[1]), lambda i: (i, 0)),
                  pl.BlockSpec((tm, xq.shape[1]), lambda i: (i, 0)),
                  pl.BlockSpec((1,) + kv.shape[1:], lambda i: ((i * tm) // seq, 0, 0)),
                  _const_spec(band.shape),
                  _const_spec(gpre.shape), _col_window_spec(d, gate_col0, n_gate),
                  _const_spec(wmix.shape),
                  _const_spec(pscale.shape), _const_spec(wpo.shape), _const_spec(wsbo.shape),
                  _const_spec(wxo.shape), _const_spec(wout.shape), _const_spec(gpost.shape)],
        out_specs=pl.BlockSpec((tm, d), lambda i: (i, 0)),
        out_shape=jax.ShapeDtypeStruct((t, d), F32),
        scratch_shapes=[pltpu.VMEM((tm + POOL_BLK, pw), BF16)],
        compiler_params=pltpu.CompilerParams(dimension_semantics=("arbitrary",),
                                             vmem_limit_bytes=VMEM_LIMIT),
        name="mix",
    )(x2, pin, pin, ysb, xq, kv, band, gpre, w_in, wmix, pscale, wpo, wsbo, wxo, wout, gpost)


def _pool_band_matrices():
    t = jnp.arange(POOL_BLK)[None, :, None] + POOL_BLK
    j = jnp.arange(2 * POOL_BLK)[None, None, :]
    w = jnp.asarray(POOL_WINDOWS)[:, None, None]
    return jnp.where(jnp.logical_and(j <= t, j > t - w), 1.0, 0.0).astype(BF16)


def _ffn_kernel(h_ref, gpre_ref, win_ref, wout_ref, gpost_ref, o_ref, a_ref, *, d_ff, chunk):
    h = h_ref[...]
    nb = _rmsnorm(h, gpre_ref[...]).astype(BF16)
    for c in range(d_ff // chunk):
        g = _dot(nb, win_ref[:, c * chunk:(c + 1) * chunk])
        u = _dot(nb, win_ref[:, d_ff + c * chunk:d_ff + (c + 1) * chunk])
        a_ref[:, c * chunk:(c + 1) * chunk] = (g * jax.nn.sigmoid(g) * u).astype(BF16)
    ff = _dot(a_ref[...], wout_ref[...])
    o_ref[...] = h + _rmsnorm(ff, gpost_ref[...])


def _ffn(h, gpre, win, wout, gpost, *, tm, chunk):
    t, d = h.shape
    d_ff = wout.shape[0]
    kern = functools.partial(_ffn_kernel, d_ff=d_ff, chunk=chunk)
    return pl.pallas_call(
        kern,
        grid=(t // tm,),
        in_specs=[pl.BlockSpec((tm, d), lambda i: (i, 0)),
                  _const_spec(gpre.shape), _const_spec(win.shape), _const_spec(wout.shape),
                  _const_spec(gpost.shape)],
        out_specs=pl.BlockSpec((tm, d), lambda i: (i, 0)),
        out_shape=jax.ShapeDtypeStruct((t, d), F32),
        scratch_shapes=[pltpu.VMEM((tm, d_ff), BF16)],
        compiler_params=pltpu.CompilerParams(dimension_semantics=("arbitrary",),
                                             vmem_limit_bytes=VMEM_LIMIT),
        name="ffn",
    )(h, gpre, win, wout, gpost)


def kernel(x, mem, norm_mix_pre, w_in, w_pool_mix, pool_scale, w_pool_o, w_sb_o, norm_mem,
           w_mem_kv, w_x_o, w_out, norm_mix_post, norm_ffn_pre, w_ffn_in, w_ffn_out,
           norm_ffn_post):
    b, s, d = x.shape
    depth = w_in.shape[0]
    pool_w = w_pool_o.shape[1]
    sb_w = w_sb_o.shape[1]
    x_w = w_x_o.shape[1]
    sb_heads = sb_w // HEAD_DIM
    x_heads = x_w // HEAD_DIM
    split = pool_w + 3 * sb_w + x_w
    tm = 512

    h = x.reshape(b * s, d)
    for l in range(depth):
        row = lambda v: v[l].reshape(1, -1)
        kv = _mem_kv(mem, row(norm_mem), w_mem_kv[l])
        pin, qkv, xq = _in_proj(h, row(norm_mix_pre), w_in[l], tm=tm, pool_w=pool_w,
                                sb_w=sb_w, x_w=x_w)
        ysb = _sb_attn(qkv.reshape(b, s, 3 * sb_w), heads=sb_heads).reshape(b * s, sb_w)
        h = _mix(h, pin, ysb, xq, kv, row(norm_mix_pre), w_in[l], w_pool_mix[l], row(pool_scale),
                 w_pool_o[l], w_sb_o[l], w_x_o[l], w_out[l], row(norm_mix_post), tm=tm, seq=s,
                 x_heads=x_heads, gate_col0=split)
        h = _ffn(h, row(norm_ffn_pre), w_ffn_in[l], w_ffn_out[l], row(norm_ffn_post), tm=tm,
                 chunk=256)
    return h.reshape(b, s, d)
```

```python
import functools

import jax
import jax.numpy as jnp
from jax import lax
from jax.experimental import pallas as pl
from jax.experimental.pallas import tpu as pltpu

F32 = jnp.float32
BF16 = jnp.bfloat16

RMS_EPS = 1e-6
POOL_WINDOWS = (2, 4, 8, 16)
POOL_BLK = 128
LANES = 128
HEAD_DIM = 64
SB_BLOCK = 128
SB_SPAN = 3
SB_TILE = 512
SB_TOP_ROWS = 48
LOG2E = 1.4426950408889634
SB_EXIT_LOG2 = -150.0
SB_MASK_BIAS = -1e30
VMEM_LIMIT = 56 * 1024 * 1024
TM_IN_PROJ = 1024
TM_MIX = 1024
TM_FFN = 512


def _rmsnorm(x, g):
    ms = jnp.mean(x * x, axis=-1, keepdims=True)
    return x * lax.rsqrt(ms + RMS_EPS) * g


def _dot(a, b):
    return jnp.dot(a, b, preferred_element_type=F32)


def _dot_nt(a, b):
    return lax.dot_general(a, b, (((1,), (1,)), ((), ())), preferred_element_type=F32)


def _const_spec(shape):
    nd = len(shape)
    return pl.BlockSpec(shape, lambda *_: (0,) * nd, pipeline_mode=pl.Buffered(1))


def _col_window_spec(rows, col0, ncols):
    return pl.BlockSpec((pl.Element(rows), pl.Element(ncols)), lambda *_: (0, col0),
                        pipeline_mode=pl.Buffered(1))


def _mem_kv_kernel(mem_ref, g_ref, w_ref, o_ref):
    n = _rmsnorm(mem_ref[0], g_ref[...])
    o_ref[0] = _dot(n.astype(BF16), w_ref[...]).astype(BF16)


def _mem_kv(mem, g, w):
    b, m, d = mem.shape
    n_out = w.shape[1]
    return pl.pallas_call(
        _mem_kv_kernel,
        grid=(b,),
        in_specs=[pl.BlockSpec((1, m, d), lambda i: (i, 0, 0)),
                  _const_spec((1, d)),
                  _const_spec((d, n_out))],
        out_specs=pl.BlockSpec((1, m, n_out), lambda i: (i, 0, 0)),
        out_shape=jax.ShapeDtypeStruct((b, m, n_out), BF16),
        compiler_params=pltpu.CompilerParams(dimension_semantics=("arbitrary",)),
        name="mem_kv",
    )(mem, g, w)


def _in_proj_kernel(x_ref, g_ref, w_ref, p_ref, qkv_ref, xq_ref, *, pool_w, sb_w):
    nb = _rmsnorm(x_ref[...], g_ref[...]).astype(BF16)
    scale = 1.0 / (HEAD_DIM ** 0.5)
    o0, o1, o2 = pool_w, pool_w + sb_w, pool_w + 3 * sb_w
    p_ref[...] = _dot(nb, w_ref[:, :o0]).astype(BF16)
    qkv_ref[:, :sb_w] = (_dot(nb, w_ref[:, o0:o1]) * (scale * LOG2E)).astype(BF16)
    qkv_ref[:, sb_w:] = _dot(nb, w_ref[:, o1:o2]).astype(BF16)
    xq_ref[...] = (_dot(nb, w_ref[:, o2:]) * scale).astype(BF16)


def _in_proj(x2, g, w, *, tm, pool_w, sb_w, x_w):
    t, d = x2.shape
    n_in = pool_w + 3 * sb_w + x_w
    kern = functools.partial(_in_proj_kernel, pool_w=pool_w, sb_w=sb_w)
    return pl.pallas_call(
        kern,
        grid=(t // tm,),
        in_specs=[pl.BlockSpec((tm, d), lambda i: (i, 0)),
                  _const_spec((1, d)),
                  _col_window_spec(d, 0, n_in)],
        out_specs=[pl.BlockSpec((tm, pool_w), lambda i: (i, 0)),
                   pl.BlockSpec((tm, 3 * sb_w), lambda i: (i, 0)),
                   pl.BlockSpec((tm, x_w), lambda i: (i, 0))],
        out_shape=[jax.ShapeDtypeStruct((t, pool_w), BF16),
                   jax.ShapeDtypeStruct((t, 3 * sb_w), BF16),
                   jax.ShapeDtypeStruct((t, x_w), BF16)],
        compiler_params=pltpu.CompilerParams(dimension_semantics=("arbitrary",),
                                             vmem_limit_bytes=VMEM_LIMIT),
        name="in_proj",
    )(x2, g, w)


def _sb_kernel(q_ref, k_ref, v_ref, cm_ref, o_ref, qm_all_ref, z_all_ref, cat_all_ref, a_all_ref,
               acc_all_ref, carry_all_ref, *, heads):
    blk = SB_BLOCK
    npair = heads // 2
    n_qblk = q_ref.shape[1] // blk
    iq0 = pl.program_id(1) * n_qblk
    row = lax.broadcasted_iota(jnp.int32, (blk, blk), 0)
    col = lax.broadcasted_iota(jnp.int32, (blk, blk), 1)
    tri = col < row
    lo_half = col < HEAD_DIM

    def load_queries(kq):
        for p in range(npair):
            qp = q_ref[0, kq * blk:(kq + 1) * blk, p * LANES:(p + 1) * LANES]
            zero = jnp.zeros_like(qp)
            qm_all_ref[kq, p, :blk] = jnp.where(lo_half, qp, zero)
            qm_all_ref[kq, p, blk:] = jnp.where(lo_half, zero, qp)

    def visit(kq, entries, diag, fresh):
        qm_ref, z_ref, cat_ref, a_ref, acc_ref, carry_ref = (
            r.at[kq] for r in (qm_all_ref, z_all_ref, cat_all_ref, a_all_ref, acc_all_ref,
                               carry_all_ref))
        nb = len(entries)
        assert all((lo, hi) == (0, blk) for _, lo, hi, _ in entries[1:])
        blk0, lo0, hi0, bias0 = entries[0]
        n0 = hi0 - lo0
        partial = n0 != blk
        full = entries[1:] if partial else entries
        first_full = 1 if partial else 0

        def kv_rows(ref, p, blocks):
            parts = [ref[0, pl.ds(pl.multiple_of(bk * blk, blk), blk), p * LANES:(p + 1) * LANES]
                     for bk in blocks]
            return parts[0] if len(parts) == 1 else jnp.concatenate(parts, axis=0)

        def biased(z, bias):
            return z if bias is None else z + bias

        for p in range(npair):
            if full:
                z = _dot_nt(qm_ref[p], kv_rows(k_ref, p, [e[0] for e in full]))
                for i, e in enumerate(full):
                    sl = first_full + i
                    z_ref[p, :, sl * blk:(sl + 1) * blk] = biased(z[:, i * blk:(i + 1) * blk], e[3])
            if partial:
                lhs = jnp.concatenate([qm_ref[p, lo0:hi0], qm_ref[p, blk + lo0:blk + hi0]], axis=0)
                z = biased(_dot_nt(lhs, kv_rows(k_ref, p, [blk0])), bias0)
                z_ref[p, lo0:hi0, :blk] = z[:n0]
                z_ref[p, blk + lo0:blk + hi0, :blk] = z[n0:]

        def rows_of(sl):
            return (lo0, hi0) if (partial and sl == 0) else (0, blk)

        for h in range(heads):
            p, hh = divmod(h, 2)
            for sl in range(nb):
                lo, hi = rows_of(sl)
                z = z_ref[p, hh * blk + lo:hh * blk + hi, sl * blk:(sl + 1) * blk]
                sp = jnp.maximum(z, 0.0) + jnp.log2(1.0 + jnp.exp2(-jnp.abs(z)))
                if diag and sl == nb - 1:
                    sp = jnp.where(tri, sp, 0.0)
                r0 = (h * SB_SPAN + sl + 1) * blk - (hi - lo)
                cat_ref[r0:r0 + hi - lo, :] = sp.astype(BF16)
        mx = None
        for h in range(heads):
            p, hh = divmod(h, 2)
            r_start = (h * SB_SPAN + 1) * blk - n0
            r_end = (h * SB_SPAN + nb) * blk
            r = _dot(cat_ref[r_start:r_end, :], cm_ref[...])
            c = None if fresh else carry_ref[h]
            for sl in reversed(range(nb)):
                lo, hi = rows_of(sl)
                off = 0 if sl == 0 else n0 + (sl - 1) * blk
                arg = (z_ref[p, hh * blk + lo:hh * blk + hi, sl * blk:(sl + 1) * blk]
                       + r[off:off + hi - lo, :blk])
                if c is not None:
                    arg = arg + c[lo:hi]
                a = jnp.exp2(arg)
                if diag and sl == nb - 1:
                    a = jnp.where(tri, a, 0.0)
                a_ref[h * blk + lo:h * blk + hi, sl * blk:(sl + 1) * blk] = a.astype(BF16)
                tot = r[off:off + hi - lo, blk:]
                if c is None:
                    c = tot
                elif (lo, hi) == (0, blk):
                    c = c + tot
                else:
                    pieces = ([c[:lo]] if lo else []) + [c[lo:hi] + tot] + ([c[hi:]] if hi < blk else [])
                    c = jnp.concatenate(pieces, axis=0)
            carry_ref[h] = c
            mx = c if mx is None else jnp.maximum(mx, c)
        for p in range(npair):
            if full:
                res = _dot(a_ref[2 * p * blk:2 * (p + 1) * blk, first_full * blk:nb * blk],
                           kv_rows(v_ref, p, [e[0] for e in full]))
                for hh in range(2):
                    if fresh:
                        acc_ref[2 * p + hh] = res[hh * blk:(hh + 1) * blk]
                    else:
                        acc_ref[2 * p + hh] += res[hh * blk:(hh + 1) * blk]
            if partial:
                lhs = jnp.concatenate([a_ref[2 * p * blk + lo0:2 * p * blk + hi0, :blk],
                                       a_ref[(2 * p + 1) * blk + lo0:(2 * p + 1) * blk + hi0, :blk]],
                                      axis=0)
                res = _dot(lhs, kv_rows(v_ref, p, [blk0]))
                acc_ref[2 * p, lo0:hi0] += res[:n0]
                acc_ref[2 * p + 1, lo0:hi0] += res[n0:]
        return jnp.max(mx), jnp.max(mx[SB_TOP_ROWS:])

    def first_visit(kq, iq):
        entries = []
        for back in range(SB_SPAN - 1, -1, -1):
            hi = SB_TOP_ROWS if back == SB_SPAN - 1 else blk
            bias = None if back == 0 else jnp.where(iq >= back, 0.0, SB_MASK_BIAS)
            entries.append((jnp.maximum(iq - back, 0), 0, hi, bias))
        return visit(kq, entries, True, True)

    def finish_oldest_block(kq, iq):
        back = SB_SPAN - 1
        bias = jnp.where(iq >= back, 0.0, SB_MASK_BIAS)
        return visit(kq, [(jnp.maximum(iq - back, 0), SB_TOP_ROWS, blk, bias)], False, False)[0]

    def continue_walk(kq, iq, m0):
        def cond(st):
            j, m = st
            return jnp.logical_and(j >= 0, m > SB_EXIT_LOG2)

        def body(st):
            j, _ = st
            return j - 1, visit(kq, [(j, 0, blk, None)], False, False)[0]

        lax.while_loop(cond, body, (iq - SB_SPAN, m0))

    carries = []
    for kq in range(n_qblk):
        load_queries(kq)
        carries.append(first_visit(kq, iq0 + kq))
    for kq in range(n_qblk):
        iq = iq0 + kq
        m_all, m_rest = carries[kq]
        m_all = lax.cond(m_rest > SB_EXIT_LOG2, lambda: finish_oldest_block(kq, iq), lambda: m_all)
        continue_walk(kq, iq, m_all)
        for p in range(npair):
            o = jnp.where(lo_half, acc_all_ref[kq, 2 * p], acc_all_ref[kq, 2 * p + 1])
            o_ref[0, kq * blk:(kq + 1) * blk, p * LANES:(p + 1) * LANES] = o.astype(BF16)


def _sb_attn(qkv, *, heads):
    b, s, w3 = qkv.shape
    w = w3 // 3
    blk = SB_BLOCK
    nq = SB_TILE // blk
    cm = _suffix_sum_matrix()
    kern = functools.partial(_sb_kernel, heads=heads)
    return pl.pallas_call(
        kern,
        grid=(b, s // SB_TILE),
        in_specs=[pl.BlockSpec((1, SB_TILE, w), lambda bi, i: (bi, i, 0)),
                  pl.BlockSpec((1, s, w), lambda bi, i: (bi, 0, 1), pipeline_mode=pl.Buffered(1)),
                  pl.BlockSpec((1, s, w), lambda bi, i: (bi, 0, 2), pipeline_mode=pl.Buffered(1)),
                  _const_spec(cm.shape)],
        out_specs=pl.BlockSpec((1, SB_TILE, w), lambda bi, i: (bi, i, 0)),
        out_shape=jax.ShapeDtypeStruct((b, s, w), BF16),
        scratch_shapes=[pltpu.VMEM((nq, heads // 2, 2 * blk, LANES), BF16),
                        pltpu.VMEM((nq, heads // 2, 2 * blk, SB_SPAN * blk), F32),
                        pltpu.VMEM((nq, heads * SB_SPAN * blk, blk), BF16),
                        pltpu.VMEM((nq, heads * blk, SB_SPAN * blk), BF16),
                        pltpu.VMEM((nq, heads, blk, LANES), F32),
                        pltpu.VMEM((nq, heads, blk, LANES), F32)],
        compiler_params=pltpu.CompilerParams(dimension_semantics=("arbitrary", "arbitrary"),
                                             vmem_limit_bytes=VMEM_LIMIT),
        name="sb_attn",
    )(qkv, qkv, qkv, cm)


def _suffix_sum_matrix():
    blk = SB_BLOCK
    j = jnp.arange(blk)[:, None]
    s = jnp.arange(2 * blk)[None, :]
    return jnp.where(jnp.logical_or(s >= blk, j >= s), -1.0, 0.0).astype(BF16)


def _mix_kernel(x_ref, pin_ref, halo_ref, ysb_ref, xq_ref, kv_ref, band_ref, gpre_ref, wg_ref,
                wmix_ref, pscale_ref, wpo_ref, wsbo_ref, wxo_ref, wout_ref, gpost_ref, o_ref,
                ext_ref, *, seq, x_heads):
    t = pl.program_id(0)
    tm, d = x_ref.shape
    x = x_ref[...]
    nb = _rmsnorm(x, gpre_ref[...]).astype(BF16)

    tok0 = (t * tm) % seq
    halo = halo_ref[...]
    ext_ref[:POOL_BLK, :] = jnp.where(tok0 == 0, jnp.zeros_like(halo), halo)
    ext_ref[POOL_BLK:, :] = pin_ref[...]
    pos1 = tok0 + 1 + lax.broadcasted_iota(jnp.int32, (tm, 1), 0)
    groups = []
    for g, w in enumerate(POOL_WINDOWS):
        cs = slice(g * LANES, (g + 1) * LANES)
        inv = 1.0 / jnp.minimum(pos1, w).astype(F32)
        chunks = []
        for c in range(tm // POOL_BLK):
            wsum = _dot(band_ref[g], ext_ref[c * POOL_BLK:(c + 2) * POOL_BLK, cs])
            cur = ext_ref[(c + 1) * POOL_BLK:(c + 2) * POOL_BLK, cs].astype(F32)
            chunks.append(wsum * inv[c * POOL_BLK:(c + 1) * POOL_BLK] - cur)
        mixed = jnp.concatenate(chunks, axis=0)
        groups.append(_dot(mixed.astype(BF16), wmix_ref[g]))
    ypool = jnp.concatenate(groups, axis=1) * pscale_ref[...]
    merged = jax.nn.sigmoid(_dot(nb, wg_ref[:, :d])) * _dot(ypool.astype(BF16), wpo_ref[...])

    merged += jax.nn.sigmoid(_dot(nb, wg_ref[:, d:2 * d])) * _dot(ysb_ref[...], wsbo_ref[...])

    m_len = kv_ref.shape[1]
    xw = x_heads * HEAD_DIM
    lane = lax.broadcasted_iota(jnp.int32, (tm, LANES), 1)
    lo_half = lane < HEAD_DIM
    ones = jnp.ones((m_len, LANES), BF16)
    pairs = []
    for p in range(x_heads // 2):
        cs = slice(p * LANES, (p + 1) * LANES)
        xq = xq_ref[:, cs]
        mk = kv_ref[0, :, cs]
        mv1 = jnp.concatenate([kv_ref[0, :, xw + p * LANES:xw + (p + 1) * LANES], ones], axis=1)
        outs = []
        for hh in range(2):
            keep = lo_half if hh == 0 else jnp.logical_not(lo_half)
            sc = _dot_nt(jnp.where(keep, xq, jnp.zeros_like(xq)), mk)
            e = jnp.exp(sc - jnp.max(sc, axis=-1, keepdims=True)).astype(BF16)
            r = _dot(e, mv1)
            outs.append(r[:, :LANES] / r[:, LANES:])
        pairs.append(jnp.where(lo_half, outs[0], outs[1]))
    yx = jnp.concatenate(pairs, axis=1).astype(BF16)
    merged += jax.nn.sigmoid(_dot(nb, wg_ref[:, 2 * d:])) * _dot(yx, wxo_ref[...])

    mo = _dot(merged.astype(BF16), wout_ref[...])
    o_ref[...] = x + _rmsnorm(mo, gpost_ref[...])


def _mix(x2, pin, ysb, xq, kv, gpre, w_in, wmix, pscale, wpo, wsbo, wxo, wout, gpost, *, tm, seq,
         x_heads, gate_col0):
    t, d = x2.shape
    pw = pin.shape[1]
    hb = tm // POOL_BLK
    n_gate = w_in.shape[1] - gate_col0
    band = _pool_band_matrices()
    kern = functools.partial(_mix_kernel, seq=seq, x_heads=x_heads)
    return pl.pallas_call(
        kern,
        grid=(t // tm,),
        in_specs=[pl.BlockSpec((tm, d), lambda i: (i, 0)),
                  pl.BlockSpec((tm, pw), lambda i: (i, 0)),
                  pl.BlockSpec((POOL_BLK, pw), lambda i: (jnp.maximum(i * hb - 1, 0), 0)),
                  pl.BlockSpec((tm, ysb.shape[1]), lambda i: (i, 0)),
                  pl.BlockSpec((tm, xq.shape[1]), lambda i: (i, 0)),
                  pl.BlockSpec((1,) + kv.shape[1:], lambda i: ((i * tm) // seq, 0, 0)),
                  _const_spec(band.shape),
                  _const_spec(gpre.shape), _col_window_spec(d, gate_col0, n_gate),
                  _const_spec(wmix.shape),
                  _const_spec(pscale.shape), _const_spec(wpo.shape), _const_spec(wsbo.shape),
                  _const_spec(wxo.shape), _const_spec(wout.shape), _const_spec(gpost.shape)],
        out_specs=pl.BlockSpec((tm, d), lambda i: (i, 0)),
        out_shape=jax.ShapeDtypeStruct((t, d), F32),
        scratch_shapes=[pltpu.VMEM((tm + POOL_BLK, pw), BF16)],
        compiler_params=pltpu.CompilerParams(dimension_semantics=("arbitrary",),
                                             vmem_limit_bytes=VMEM_LIMIT),
        name="mix",
    )(x2, pin, pin, ysb, xq, kv, band, gpre, w_in, wmix, pscale, wpo, wsbo, wxo, wout, gpost)


def _pool_band_matrices():
    t = jnp.arange(POOL_BLK)[None, :, None] + POOL_BLK
    j = jnp.arange(2 * POOL_BLK)[None, None, :]
    w = jnp.asarray(POOL_WINDOWS)[:, None, None]
    return jnp.where(jnp.logical_and(j <= t, j > t - w), 1.0, 0.0).astype(BF16)


def _ffn_kernel(h_ref, gpre_ref, win_ref, wout_ref, gpost_ref, o_ref, a_ref, *, d_ff, chunk):
    h = h_ref[...]
    nb = _rmsnorm(h, gpre_ref[...]).astype(BF16)
    for c in range(d_ff // chunk):
        g = _dot(nb, win_ref[:, c * chunk:(c + 1) * chunk])
        u = _dot(nb, win_ref[:, d_ff + c * chunk:d_ff + (c + 1) * chunk])
        a_ref[:, c * chunk:(c + 1) * chunk] = (g * jax.nn.sigmoid(g) * u).astype(BF16)
    ff = _dot(a_ref[...], wout_ref[...])
    o_ref[...] = h + _rmsnorm(ff, gpost_ref[...])


def _ffn(h, gpre, win, wout, gpost, *, tm, chunk):
    t, d = h.shape
    d_ff = wout.shape[0]
    kern = functools.partial(_ffn_kernel, d_ff=d_ff, chunk=chunk)
    return pl.pallas_call(
        kern,
        grid=(t // tm,),
        in_specs=[pl.BlockSpec((tm, d), lambda i: (i, 0)),
                  _const_spec(gpre.shape), _const_spec(win.shape), _const_spec(wout.shape),
                  _const_spec(gpost.shape)],
        out_specs=pl.BlockSpec((tm, d), lambda i: (i, 0)),
        out_shape=jax.ShapeDtypeStruct((t, d), F32),
        scratch_shapes=[pltpu.VMEM((tm, d_ff), BF16)],
        compiler_params=pltpu.CompilerParams(dimension_semantics=("arbitrary",),
                                             vmem_limit_bytes=VMEM_LIMIT),
        name="ffn",
    )(h, gpre, win, wout, gpost)


def kernel(x, mem, norm_mix_pre, w_in, w_pool_mix, pool_scale, w_pool_o, w_sb_o, norm_mem,
           w_mem_kv, w_x_o, w_out, norm_mix_post, norm_ffn_pre, w_ffn_in, w_ffn_out,
           norm_ffn_post):
    b, s, d = x.shape
    depth = w_in.shape[0]
    pool_w = w_pool_o.shape[1]
    sb_w = w_sb_o.shape[1]
    x_w = w_x_o.shape[1]
    sb_heads = sb_w // HEAD_DIM
    x_heads = x_w // HEAD_DIM
    split = pool_w + 3 * sb_w + x_w

    h = x.reshape(b * s, d)
    for l in range(depth):
        row = lambda v: v[l].reshape(1, -1)
        kv = _mem_kv(mem, row(norm_mem), w_mem_kv[l])
        pin, qkv, xq = _in_proj(h, row(norm_mix_pre), w_in[l], tm=TM_IN_PROJ, pool_w=pool_w,
                                sb_w=sb_w, x_w=x_w)
        ysb = _sb_attn(qkv.reshape(b, s, 3 * sb_w), heads=sb_heads).reshape(b * s, sb_w)
        h = _mix(h, pin, ysb, xq, kv, row(norm_mix_pre), w_in[l], w_pool_mix[l], row(pool_scale),
                 w_pool_o[l], w_sb_o[l], w_x_o[l], w_out[l], row(norm_mix_post), tm=TM_MIX, seq=s,
                 x_heads=x_heads, gate_col0=split)
        h = _ffn(h, row(norm_ffn_pre), w_ffn_in[l], w_ffn_out[l], row(norm_ffn_post), tm=TM_FFN,
                 chunk=256)
    return h.reshape(b, s, d)
```

```python
import functools

import jax
import jax.numpy as jnp
from jax import lax
from jax.experimental import pallas as pl
from jax.experimental.pallas import tpu as pltpu

F32 = jnp.float32
BF16 = jnp.bfloat16

RMS_EPS = 1e-6
POOL_WINDOWS = (2, 4, 8, 16)
POOL_BLK = 128
LANES = 128
HEAD_DIM = 64
SB_BLOCK = 128
SB_SPAN = 3
SB_TILE = 512
SB_TOP_ROWS = 48
LOG2E = 1.4426950408889634
SB_EXIT_LOG2 = -150.0
SB_MASK_BIAS = -1e30
SB_LINEAR_Z = 64.0
VMEM_LIMIT = 56 * 1024 * 1024
TM_IN_PROJ = 2048
TM_MIX = 1024
TM_FFN = 512


def _rmsnorm(x, g):
    ms = jnp.mean(x * x, axis=-1, keepdims=True)
    return x * lax.rsqrt(ms + RMS_EPS) * g


def _dot(a, b):
    return jnp.dot(a, b, preferred_element_type=F32)


def _dot_nt(a, b):
    return lax.dot_general(a, b, (((1,), (1,)), ((), ())), preferred_element_type=F32)


def _const_spec(shape):
    nd = len(shape)
    return pl.BlockSpec(shape, lambda *_: (0,) * nd, pipeline_mode=pl.Buffered(1))


def _col_window_spec(rows, col0, ncols):
    return pl.BlockSpec((pl.Element(rows), pl.Element(ncols)), lambda *_: (0, col0),
                        pipeline_mode=pl.Buffered(1))


def _mem_kv_kernel(mem_ref, g_ref, w_ref, o_ref):
    n = _rmsnorm(mem_ref[0], g_ref[...])
    o_ref[0] = _dot(n.astype(BF16), w_ref[...]).astype(BF16)


def _mem_kv(mem, g, w):
    b, m, d = mem.shape
    n_out = w.shape[1]
    return pl.pallas_call(
        _mem_kv_kernel,
        grid=(b,),
        in_specs=[pl.BlockSpec((1, m, d), lambda i: (i, 0, 0)),
                  _const_spec((1, d)),
                  _const_spec((d, n_out))],
        out_specs=pl.BlockSpec((1, m, n_out), lambda i: (i, 0, 0)),
        out_shape=jax.ShapeDtypeStruct((b, m, n_out), BF16),
        compiler_params=pltpu.CompilerParams(dimension_semantics=("arbitrary",)),
        name="mem_kv",
    )(mem, g, w)


def _in_proj_kernel(x_ref, g_ref, w_ref, p_ref, qkv_ref, xq_ref, *, pool_w, sb_w):
    nb = _rmsnorm(x_ref[...], g_ref[...]).astype(BF16)
    scale = 1.0 / (HEAD_DIM ** 0.5)
    o0, o1, o2 = pool_w, pool_w + sb_w, pool_w + 3 * sb_w
    p_ref[...] = _dot(nb, w_ref[:, :o0]).astype(BF16)
    qkv_ref[:, :sb_w] = (_dot(nb, w_ref[:, o0:o1]) * (scale * LOG2E)).astype(BF16)
    qkv_ref[:, sb_w:] = _dot(nb, w_ref[:, o1:o2]).astype(BF16)
    xq_ref[...] = (_dot(nb, w_ref[:, o2:]) * scale).astype(BF16)


def _in_proj(x2, g, w, *, tm, pool_w, sb_w, x_w):
    t, d = x2.shape
    n_in = pool_w + 3 * sb_w + x_w
    kern = functools.partial(_in_proj_kernel, pool_w=pool_w, sb_w=sb_w)
    return pl.pallas_call(
        kern,
        grid=(t // tm,),
        in_specs=[pl.BlockSpec((tm, d), lambda i: (i, 0)),
                  _const_spec((1, d)),
                  _col_window_spec(d, 0, n_in)],
        out_specs=[pl.BlockSpec((tm, pool_w), lambda i: (i, 0)),
                   pl.BlockSpec((tm, 3 * sb_w), lambda i: (i, 0)),
                   pl.BlockSpec((tm, x_w), lambda i: (i, 0))],
        out_shape=[jax.ShapeDtypeStruct((t, pool_w), BF16),
                   jax.ShapeDtypeStruct((t, 3 * sb_w), BF16),
                   jax.ShapeDtypeStruct((t, x_w), BF16)],
        compiler_params=pltpu.CompilerParams(dimension_semantics=("arbitrary",),
                                             vmem_limit_bytes=VMEM_LIMIT),
        name="in_proj",
    )(x2, g, w)


def _sb_kernel(q_ref, k_ref, v_ref, cm_ref, o_ref, qm_all_ref, z_all_ref, cat_all_ref, a_all_ref,
               acc_all_ref, carry_all_ref, *, heads):
    blk = SB_BLOCK
    npair = heads // 2
    n_qblk = q_ref.shape[1] // blk
    iq0 = pl.program_id(1) * n_qblk
    row = lax.broadcasted_iota(jnp.int32, (blk, blk), 0)
    col = lax.broadcasted_iota(jnp.int32, (blk, blk), 1)
    tri = col < row
    lo_half = col < HEAD_DIM

    def load_queries(kq):
        for p in range(npair):
            qp = q_ref[0, kq * blk:(kq + 1) * blk, p * LANES:(p + 1) * LANES]
            zero = jnp.zeros_like(qp)
            qm_all_ref[kq, p, :blk] = jnp.where(lo_half, qp, zero)
            qm_all_ref[kq, p, blk:] = jnp.where(lo_half, zero, qp)

    def visit(kq, entries, diag, fresh):
        qm_ref, z_ref, cat_ref, a_ref, acc_ref, carry_ref = (
            r.at[kq] for r in (qm_all_ref, z_all_ref, cat_all_ref, a_all_ref, acc_all_ref,
                               carry_all_ref))
        nb = len(entries)
        assert all((lo, hi) == (0, blk) for _, lo, hi, _ in entries[1:])
        blk0, lo0, hi0, bias0 = entries[0]
        n0 = hi0 - lo0
        partial = n0 != blk
        full = entries[1:] if partial else entries
        first_full = 1 if partial else 0

        def kv_rows(ref, p, blocks):
            parts = [ref[0, pl.ds(pl.multiple_of(bk * blk, blk), blk), p * LANES:(p + 1) * LANES]
                     for bk in blocks]
            return parts[0] if len(parts) == 1 else jnp.concatenate(parts, axis=0)

        def biased(z, bias):
            return z if bias is None else z + bias

        for p in range(npair):
            if full:
                z = _dot_nt(qm_ref[p], kv_rows(k_ref, p, [e[0] for e in full]))
                for i, e in enumerate(full):
                    sl = first_full + i
                    z_ref[p, :, sl * blk:(sl + 1) * blk] = biased(z[:, i * blk:(i + 1) * blk], e[3])
            if partial:
                lhs = jnp.concatenate([qm_ref[p, lo0:hi0], qm_ref[p, blk + lo0:blk + hi0]], axis=0)
                z = biased(_dot_nt(lhs, kv_rows(k_ref, p, [blk0])), bias0)
                z_ref[p, lo0:hi0, :blk] = z[:n0]
                z_ref[p, blk + lo0:blk + hi0, :blk] = z[n0:]

        def rows_of(sl):
            return (lo0, hi0) if (partial and sl == 0) else (0, blk)

        for h in range(heads):
            p, hh = divmod(h, 2)
            for sl in range(nb):
                lo, hi = rows_of(sl)
                z = z_ref[p, hh * blk + lo:hh * blk + hi, sl * blk:(sl + 1) * blk]
                sp = jnp.where(z > SB_LINEAR_Z, z, jnp.log2(1.0 + jnp.exp2(z)))
                if diag and sl == nb - 1:
                    sp = jnp.where(tri, sp, 0.0)
                r0 = (h * SB_SPAN + sl + 1) * blk - (hi - lo)
                cat_ref[r0:r0 + hi - lo, :] = sp.astype(BF16)
        mx = None
        for h in range(heads):
            p, hh = divmod(h, 2)
            r_start = (h * SB_SPAN + 1) * blk - n0
            r_end = (h * SB_SPAN + nb) * blk
            r = _dot(cat_ref[r_start:r_end, :], cm_ref[...])
            c = None if fresh else carry_ref[h]
            for sl in reversed(range(nb)):
                lo, hi = rows_of(sl)
                off = 0 if sl == 0 else n0 + (sl - 1) * blk
                arg = (z_ref[p, hh * blk + lo:hh * blk + hi, sl * blk:(sl + 1) * blk]
                       + r[off:off + hi - lo, :blk])
                if c is not None:
                    arg = arg + c[lo:hi]
                a = jnp.exp2(arg)
                if diag and sl == nb - 1:
                    a = jnp.where(tri, a, 0.0)
                a_ref[h * blk + lo:h * blk + hi, sl * blk:(sl + 1) * blk] = a.astype(BF16)
                tot = r[off:off + hi - lo, blk:]
                if c is None:
                    c = tot
                elif (lo, hi) == (0, blk):
                    c = c + tot
                else:
                    pieces = ([c[:lo]] if lo else []) + [c[lo:hi] + tot] + ([c[hi:]] if hi < blk else [])
                    c = jnp.concatenate(pieces, axis=0)
            carry_ref[h] = c
            mx = c if mx is None else jnp.maximum(mx, c)
        for p in range(npair):
            if full:
                res = _dot(a_ref[2 * p * blk:2 * (p + 1) * blk, first_full * blk:nb * blk],
                           kv_rows(v_ref, p, [e[0] for e in full]))
                for hh in range(2):
                    if fresh:
                        acc_ref[2 * p + hh] = res[hh * blk:(hh + 1) * blk]
                    else:
                        acc_ref[2 * p + hh] += res[hh * blk:(hh + 1) * blk]
            if partial:
                lhs = jnp.concatenate([a_ref[2 * p * blk + lo0:2 * p * blk + hi0, :blk],
                                       a_ref[(2 * p + 1) * blk + lo0:(2 * p + 1) * blk + hi0, :blk]],
                                      axis=0)
                res = _dot(lhs, kv_rows(v_ref, p, [blk0]))
                acc_ref[2 * p, lo0:hi0] += res[:n0]
                acc_ref[2 * p + 1, lo0:hi0] += res[n0:]
        return jnp.max(mx), jnp.max(mx[SB_TOP_ROWS:])

    def first_visit(kq, iq):
        entries = []
        for back in range(SB_SPAN - 1, -1, -1):
            hi = SB_TOP_ROWS if back == SB_SPAN - 1 else blk
            bias = None if back == 0 else jnp.where(iq >= back, 0.0, SB_MASK_BIAS)
            entries.append((jnp.maximum(iq - back, 0), 0, hi, bias))
        return visit(kq, entries, True, True)

    def finish_oldest_block(kq, iq):
        back = SB_SPAN - 1
        bias = jnp.where(iq >= back, 0.0, SB_MASK_BIAS)
        return visit(kq, [(jnp.maximum(iq - back, 0), SB_TOP_ROWS, blk, bias)], False, False)[0]

    def continue_walk(kq, iq, m0):
        def cond(st):
            j, m = st
            return jnp.logical_and(j >= 0, m > SB_EXIT_LOG2)

        def body(st):
            j, _ = st
            return j - 1, visit(kq, [(j, 0, blk, None)], False, False)[0]

        lax.while_loop(cond, body, (iq - SB_SPAN, m0))

    carries = []
    for kq in range(n_qblk):
        load_queries(kq)
        carries.append(first_visit(kq, iq0 + kq))
    for kq in range(n_qblk):
        iq = iq0 + kq
        m_all, m_rest = carries[kq]
        m_all = lax.cond(m_rest > SB_EXIT_LOG2, lambda: finish_oldest_block(kq, iq), lambda: m_all)
        continue_walk(kq, iq, m_all)
        for p in range(npair):
            o = jnp.where(lo_half, acc_all_ref[kq, 2 * p], acc_all_ref[kq, 2 * p + 1])
            o_ref[0, kq * blk:(kq + 1) * blk, p * LANES:(p + 1) * LANES] = o.astype(BF16)


def _sb_attn(qkv, *, heads):
    b, s, w3 = qkv.shape
    w = w3 // 3
    blk = SB_BLOCK
    nq = SB_TILE // blk
    cm = _suffix_sum_matrix()
    kern = functools.partial(_sb_kernel, heads=heads)
    return pl.pallas_call(
        kern,
        grid=(b, s // SB_TILE),
        in_specs=[pl.BlockSpec((1, SB_TILE, w), lambda bi, i: (bi, i, 0)),
                  pl.BlockSpec((1, s, w), lambda bi, i: (bi, 0, 1)),
                  pl.BlockSpec((1, s, w), lambda bi, i: (bi, 0, 2)),
                  _const_spec(cm.shape)],
        out_specs=pl.BlockSpec((1, SB_TILE, w), lambda bi, i: (bi, i, 0)),
        out_shape=jax.ShapeDtypeStruct((b, s, w), BF16),
        scratch_shapes=[pltpu.VMEM((nq, heads // 2, 2 * blk, LANES), BF16),
                        pltpu.VMEM((nq, heads // 2, 2 * blk, SB_SPAN * blk), F32),
                        pltpu.VMEM((nq, heads * SB_SPAN * blk, blk), BF16),
                        pltpu.VMEM((nq, heads * blk, SB_SPAN * blk), BF16),
                        pltpu.VMEM((nq, heads, blk, LANES), F32),
                        pltpu.VMEM((nq, heads, blk, LANES), F32)],
        compiler_params=pltpu.CompilerParams(dimension_semantics=("arbitrary", "arbitrary"),
                                             vmem_limit_bytes=VMEM_LIMIT),
        name="sb_attn",
    )(qkv, qkv, qkv, cm)


def _suffix_sum_matrix():
    blk = SB_BLOCK
    j = jnp.arange(blk)[:, None]
    s = jnp.arange(2 * blk)[None, :]
    return jnp.where(jnp.logical_or(s >= blk, j >= s), -1.0, 0.0).astype(BF16)


def _mix_kernel(x_ref, pin_ref, halo_ref, ysb_ref, xq_ref, kv_ref, band_ref, gpre_ref, wg_ref,
                wmix_ref, pscale_ref, wpo_ref, wsbo_ref, wxo_ref, wout_ref, gpost_ref, o_ref,
                ext_ref, *, seq, x_heads):
    t = pl.program_id(0)
    tm, d = x_ref.shape
    x = x_ref[...]
    nb = _rmsnorm(x, gpre_ref[...]).astype(BF16)

    tok0 = (t * tm) % seq
    halo = halo_ref[...]
    ext_ref[:POOL_BLK, :] = jnp.where(tok0 == 0, jnp.zeros_like(halo), halo)
    ext_ref[POOL_BLK:, :] = pin_ref[...]
    pos1 = tok0 + 1 + lax.broadcasted_iota(jnp.int32, (tm, 1), 0)
    groups = []
    for g, w in enumerate(POOL_WINDOWS):
        cs = slice(g * LANES, (g + 1) * LANES)
        inv = 1.0 / jnp.minimum(pos1, w).astype(F32)
        chunks = []
        for c in range(tm // POOL_BLK):
            wsum = _dot(band_ref[g], ext_ref[c * POOL_BLK:(c + 2) * POOL_BLK, cs])
            cur = ext_ref[(c + 1) * POOL_BLK:(c + 2) * POOL_BLK, cs].astype(F32)
            chunks.append(wsum * inv[c * POOL_BLK:(c + 1) * POOL_BLK] - cur)
        mixed = jnp.concatenate(chunks, axis=0)
        groups.append(_dot(mixed.astype(BF16), wmix_ref[g]))
    ypool = jnp.concatenate(groups, axis=1) * pscale_ref[...]
    merged = jax.nn.sigmoid(_dot(nb, wg_ref[:, :d])) * _dot(ypool.astype(BF16), wpo_ref[...])

    merged += jax.nn.sigmoid(_dot(nb, wg_ref[:, d:2 * d])) * _dot(ysb_ref[...], wsbo_ref[...])

    m_len = kv_ref.shape[1]
    xw = x_heads * HEAD_DIM
    lane = lax.broadcasted_iota(jnp.int32, (tm, LANES), 1)
    lo_half = lane < HEAD_DIM
    ones = jnp.ones((m_len, LANES), BF16)
    pairs = []
    for p in range(x_heads // 2):
        cs = slice(p * LANES, (p + 1) * LANES)
        xq = xq_ref[:, cs]
        mk = kv_ref[0, :, cs]
        mv1 = jnp.concatenate([kv_ref[0, :, xw + p * LANES:xw + (p + 1) * LANES], ones], axis=1)
        outs = []
        for hh in range(2):
            keep = lo_half if hh == 0 else jnp.logical_not(lo_half)
            sc = _dot_nt(jnp.where(keep, xq, jnp.zeros_like(xq)), mk)
            e = jnp.exp(sc - jnp.max(sc, axis=-1, keepdims=True)).astype(BF16)
            r = _dot(e, mv1)
            outs.append(r[:, :LANES] / r[:, LANES:])
        pairs.append(jnp.where(lo_half, outs[0], outs[1]))
    yx = jnp.concatenate(pairs, axis=1).astype(BF16)
    merged += jax.nn.sigmoid(_dot(nb, wg_ref[:, 2 * d:])) * _dot(yx, wxo_ref[...])

    mo = _dot(merged.astype(BF16), wout_ref[...])
    o_ref[...] = x + _rmsnorm(mo, gpost_ref[...])


def _mix(x2, pin, ysb, xq, kv, gpre, w_in, wmix, pscale, wpo, wsbo, wxo, wout, gpost, *, tm, seq,
         x_heads, gate_col0):
    t, d = x2.shape
    pw = pin.shape[1]
    hb = tm // POOL_BLK
    n_gate = w_in.shape[1] - gate_col0
    band = _pool_band_matrices()
    kern = functools.partial(_mix_kernel, seq=seq, x_heads=x_heads)
    return pl.pallas_call(
        kern,
        grid=(t // tm,),
        in_specs=[pl.BlockSpec((tm, d), lambda i: (i, 0)),
                  pl.BlockSpec((tm, pw), lambda i: (i, 0)),
                  pl.BlockSpec((POOL_BLK, pw), lambda i: (jnp.maximum(i * hb - 1, 0), 0)),
                  pl.BlockSpec((tm, ysb.shape[1]), lambda i: (i, 0)),
                  pl.BlockSpec((tm, xq.shape[1]), lambda i: (i, 0)),
                  pl.BlockSpec((1,) + kv.shape[1:], lambda i: ((i * tm) // seq, 0, 0)),
                  _const_spec(band.shape),
                  _const_spec(gpre.shape), _col_window_spec(d, gate_col0, n_gate),
                  _const_spec(wmix.shape),
                  _const_spec(pscale.shape), _const_spec(wpo.shape), _const_spec(wsbo.shape),
                  _const_spec(wxo.shape), _const_spec(wout.shape), _const_spec(gpost.shape)],
        out_specs=pl.BlockSpec((tm, d), lambda i: (i, 0)),
        out_shape=jax.ShapeDtypeStruct((t, d), F32),
        scratch_shapes=[pltpu.VMEM((tm + POOL_BLK, pw), BF16)],
        compiler_params=pltpu.CompilerParams(dimension_semantics=("arbitrary",),
                                             vmem_limit_bytes=VMEM_LIMIT),
        name="mix",
    )(x2, pin, pin, ysb, xq, kv, band, gpre, w_in, wmix, pscale, wpo, wsbo, wxo, wout, gpost)


def _pool_band_matrices():
    t = jnp.arange(POOL_BLK)[None, :, None] + POOL_BLK
    j = jnp.arange(2 * POOL_BLK)[None, None, :]
    w = jnp.asarray(POOL_WINDOWS)[:, None, None]
    return jnp.where(jnp.logical_and(j <= t, j > t - w), 1.0, 0.0).astype(BF16)


def _ffn_kernel(h_ref, gpre_ref, win_ref, wout_ref, gpost_ref, o_ref, a_ref, *, d_ff, chunk):
    h = h_ref[...]
    nb = _rmsnorm(h, gpre_ref[...]).astype(BF16)
    for c in range(d_ff // chunk):
        g = _dot(nb, win_ref[:, c * chunk:(c + 1) * chunk])
        u = _dot(nb, win_ref[:, d_ff + c * chunk:d_ff + (c + 1) * chunk])
        a_ref[:, c * chunk:(c + 1) * chunk] = (g * jax.nn.sigmoid(g) * u).astype(BF16)
    ff = _dot(a_ref[...], wout_ref[...])
    o_ref[...] = h + _rmsnorm(ff, gpost_ref[...])


def _ffn(h, gpre, win, wout, gpost, *, tm, chunk):
    t, d = h.shape
    d_ff = wout.shape[0]
    kern = functools.partial(_ffn_kernel, d_ff=d_ff, chunk=chunk)
    return pl.pallas_call(
        kern,
        grid=(t // tm,),
        in_specs=[pl.BlockSpec((tm, d), lambda i: (i, 0)),
                  _const_spec(gpre.shape), _const_spec(win.shape), _const_spec(wout.shape),
                  _const_spec(gpost.shape)],
        out_specs=pl.BlockSpec((tm, d), lambda i: (i, 0)),
        out_shape=jax.ShapeDtypeStruct((t, d), F32),
        scratch_shapes=[pltpu.VMEM((tm, d_ff), BF16)],
        compiler_params=pltpu.CompilerParams(dimension_semantics=("arbitrary",),
                                             vmem_limit_bytes=VMEM_LIMIT),
        name="ffn",
    )(h, gpre, win, wout, gpost)


def kernel(x, mem, norm_mix_pre, w_in, w_pool_mix, pool_scale, w_pool_o, w_sb_o, norm_mem,
           w_mem_kv, w_x_o, w_out, norm_mix_post, norm_ffn_pre, w_ffn_in, w_ffn_out,
           norm_ffn_post):
    b, s, d = x.shape
    depth = w_in.shape[0]
    pool_w = w_pool_o.shape[1]
    sb_w = w_sb_o.shape[1]
    x_w = w_x_o.shape[1]
    sb_heads = sb_w // HEAD_DIM
    x_heads = x_w // HEAD_DIM
    split = pool_w + 3 * sb_w + x_w

    h = x.reshape(b * s, d)
    for l in range(depth):
        row = lambda v: v[l].reshape(1, -1)
        kv = _mem_kv(mem, row(norm_mem), w_mem_kv[l])
        pin, qkv, xq = _in_proj(h, row(norm_mix_pre), w_in[l], tm=TM_IN_PROJ, pool_w=pool_w,
                                sb_w=sb_w, x_w=x_w)
        ysb = _sb_attn(qkv.reshape(b, s, 3 * sb_w), heads=sb_heads).reshape(b * s, sb_w)
        h = _mix(h, pin, ysb, xq, kv, row(norm_mix_pre), w_in[l], w_pool_mix[l], row(pool_scale),
                 w_pool_o[l], w_sb_o[l], w_x_o[l], w_out[l], row(norm_mix_post), tm=TM_MIX, seq=s,
                 x_heads=x_heads, gate_col0=split)
        h = _ffn(h, row(norm_ffn_pre), w_ffn_in[l], w_ffn_out[l], row(norm_ffn_post), tm=TM_FFN,
                 chunk=256)
    return h.reshape(b, s, d)
```

```python
import functools

import jax
import jax.numpy as jnp
from jax import lax
from jax.experimental import pallas as pl
from jax.experimental.pallas import tpu as pltpu

F32 = jnp.float32
BF16 = jnp.bfloat16

RMS_EPS = 1e-6
POOL_WINDOWS = (2, 4, 8, 16)
POOL_BLK = 128
LANES = 128
HEAD_DIM = 64
SB_BLOCK = 128
SB_SPAN = 3
SB_TILE = 512
SB_TOP_ROWS = 48
LOG2E = 1.4426950408889634
SB_EXIT_LOG2 = -150.0
SB_MASK_BIAS = -1e30
SB_LINEAR_Z = 64.0
VMEM_LIMIT = 56 * 1024 * 1024
TM_IN_PROJ = 1024
TM_MIX = 1024
TM_FFN = 512


def _rmsnorm(x, g):
    ms = jnp.mean(x * x, axis=-1, keepdims=True)
    return x * lax.rsqrt(ms + RMS_EPS) * g


def _dot(a, b):
    return jnp.dot(a, b, preferred_element_type=F32)


def _dot_nt(a, b):
    return lax.dot_general(a, b, (((1,), (1,)), ((), ())), preferred_element_type=F32)


def _const_spec(shape):
    nd = len(shape)
    return pl.BlockSpec(shape, lambda *_: (0,) * nd, pipeline_mode=pl.Buffered(1))


def _col_window_spec(rows, col0, ncols):
    return pl.BlockSpec((pl.Element(rows), pl.Element(ncols)), lambda *_: (0, col0),
                        pipeline_mode=pl.Buffered(1))


def _mem_kv_kernel(mem_ref, g_ref, w_ref, o_ref):
    n = _rmsnorm(mem_ref[0], g_ref[...])
    o_ref[0] = _dot(n.astype(BF16), w_ref[...]).astype(BF16)


def _mem_kv(mem, g, w):
    b, m, d = mem.shape
    n_out = w.shape[1]
    return pl.pallas_call(
        _mem_kv_kernel,
        grid=(b,),
        in_specs=[pl.BlockSpec((1, m, d), lambda i: (i, 0, 0)),
                  _const_spec((1, d)),
                  _const_spec((d, n_out))],
        out_specs=pl.BlockSpec((1, m, n_out), lambda i: (i, 0, 0)),
        out_shape=jax.ShapeDtypeStruct((b, m, n_out), BF16),
        compiler_params=pltpu.CompilerParams(dimension_semantics=("arbitrary",)),
        name="mem_kv",
    )(mem, g, w)


def _in_proj_kernel(x_ref, g_ref, w_ref, p_ref, qkv_ref, xq_ref, *, pool_w, sb_w):
    nb = _rmsnorm(x_ref[...], g_ref[...]).astype(BF16)
    scale = 1.0 / (HEAD_DIM ** 0.5)
    o0, o1, o2 = pool_w, pool_w + sb_w, pool_w + 3 * sb_w
    p_ref[...] = _dot(nb, w_ref[:, :o0]).astype(BF16)
    qkv_ref[:, :sb_w] = (_dot(nb, w_ref[:, o0:o1]) * (scale * LOG2E)).astype(BF16)
    qkv_ref[:, sb_w:] = _dot(nb, w_ref[:, o1:o2]).astype(BF16)
    xq_ref[...] = (_dot(nb, w_ref[:, o2:]) * scale).astype(BF16)


def _in_proj(x2, g, w, *, tm, pool_w, sb_w, x_w):
    t, d = x2.shape
    n_in = pool_w + 3 * sb_w + x_w
    kern = functools.partial(_in_proj_kernel, pool_w=pool_w, sb_w=sb_w)
    return pl.pallas_call(
        kern,
        grid=(t // tm,),
        in_specs=[pl.BlockSpec((tm, d), lambda i: (i, 0)),
                  _const_spec((1, d)),
                  _col_window_spec(d, 0, n_in)],
        out_specs=[pl.BlockSpec((tm, pool_w), lambda i: (i, 0)),
                   pl.BlockSpec((tm, 3 * sb_w), lambda i: (i, 0)),
                   pl.BlockSpec((tm, x_w), lambda i: (i, 0))],
        out_shape=[jax.ShapeDtypeStruct((t, pool_w), BF16),
                   jax.ShapeDtypeStruct((t, 3 * sb_w), BF16),
                   jax.ShapeDtypeStruct((t, x_w), BF16)],
        compiler_params=pltpu.CompilerParams(dimension_semantics=("arbitrary",),
                                             vmem_limit_bytes=VMEM_LIMIT),
        name="in_proj",
    )(x2, g, w)


def _sb_kernel(q_ref, k_ref, v_ref, cm_ref, o_ref, qm_all_ref, z_all_ref, cat_all_ref, a_all_ref,
               acc_all_ref, carry_all_ref, *, heads):
    blk = SB_BLOCK
    npair = heads // 2
    n_qblk = q_ref.shape[1] // blk
    iq0 = pl.program_id(1) * n_qblk
    row = lax.broadcasted_iota(jnp.int32, (blk, blk), 0)
    col = lax.broadcasted_iota(jnp.int32, (blk, blk), 1)
    tri = col < row
    lo_half = col < HEAD_DIM

    def load_queries(kq):
        for p in range(npair):
            qp = q_ref[0, kq * blk:(kq + 1) * blk, p * LANES:(p + 1) * LANES]
            zero = jnp.zeros_like(qp)
            qm_all_ref[kq, p, :blk] = jnp.where(lo_half, qp, zero)
            qm_all_ref[kq, p, blk:] = jnp.where(lo_half, zero, qp)

    def visit(kq, entries, diag, fresh):
        qm_ref, z_ref, cat_ref, a_ref, acc_ref, carry_ref = (
            r.at[kq] for r in (qm_all_ref, z_all_ref, cat_all_ref, a_all_ref, acc_all_ref,
                               carry_all_ref))
        nb = len(entries)
        assert all((lo, hi) == (0, blk) for _, lo, hi, _ in entries[1:])
        blk0, lo0, hi0, bias0 = entries[0]
        n0 = hi0 - lo0
        partial = n0 != blk
        full = entries[1:] if partial else entries
        first_full = 1 if partial else 0

        def kv_rows(ref, p, blocks):
            parts = [ref[0, pl.ds(pl.multiple_of(bk * blk, blk), blk), p * LANES:(p + 1) * LANES]
                     for bk in blocks]
            return parts[0] if len(parts) == 1 else jnp.concatenate(parts, axis=0)

        def biased(z, bias):
            return z if bias is None else z + bias

        for p in range(npair):
            if full:
                z = _dot_nt(qm_ref[p], kv_rows(k_ref, p, [e[0] for e in full]))
                for i, e in enumerate(full):
                    sl = first_full + i
                    z_ref[p, :, sl * blk:(sl + 1) * blk] = biased(z[:, i * blk:(i + 1) * blk], e[3])
            if partial:
                lhs = jnp.concatenate([qm_ref[p, lo0:hi0], qm_ref[p, blk + lo0:blk + hi0]], axis=0)
                z = biased(_dot_nt(lhs, kv_rows(k_ref, p, [blk0])), bias0)
                z_ref[p, lo0:hi0, :blk] = z[:n0]
                z_ref[p, blk + lo0:blk + hi0, :blk] = z[n0:]

        def rows_of(sl):
            return (lo0, hi0) if (partial and sl == 0) else (0, blk)

        for h in range(heads):
            p, hh = divmod(h, 2)
            for sl in range(nb):
                lo, hi = rows_of(sl)
                z = z_ref[p, hh * blk + lo:hh * blk + hi, sl * blk:(sl + 1) * blk]
                sp = jnp.where(z > SB_LINEAR_Z, z, jnp.log2(1.0 + jnp.exp2(z)))
                if diag and sl == nb - 1:
                    sp = jnp.where(tri, sp, 0.0)
                r0 = (p * SB_SPAN + sl + 1) * blk - (hi - lo)
                cat_ref[r0:r0 + hi - lo, hh * blk:(hh + 1) * blk] = sp.astype(BF16)
        mx = None
        for p in range(npair):
            r_start = (p * SB_SPAN + 1) * blk - n0
            r_end = (p * SB_SPAN + nb) * blk
            r = _dot(cat_ref[r_start:r_end, :], cm_ref[...])
            for hh in range(2):
                h = 2 * p + hh
                c = None if fresh else carry_ref[h]
                for sl in reversed(range(nb)):
                    lo, hi = rows_of(sl)
                    off = 0 if sl == 0 else n0 + (sl - 1) * blk
                    incl = r[off:off + hi - lo, hh * blk:(hh + 1) * blk]
                    arg = z_ref[p, hh * blk + lo:hh * blk + hi, sl * blk:(sl + 1) * blk] + incl
                    if c is not None:
                        arg = arg + c[lo:hi]
                    a = jnp.exp2(arg)
                    if diag and sl == nb - 1:
                        a = jnp.where(tri, a, 0.0)
                    a_ref[h * blk + lo:h * blk + hi, sl * blk:(sl + 1) * blk] = a.astype(BF16)
                    tot = incl[:, :1]
                    if c is None:
                        c = tot
                    elif (lo, hi) == (0, blk):
                        c = c + tot
                    else:
                        pieces = (([c[:lo]] if lo else []) + [c[lo:hi] + tot]
                                  + ([c[hi:]] if hi < blk else []))
                        c = jnp.concatenate(pieces, axis=0)
                carry_ref[h] = c
                mx = c if mx is None else jnp.maximum(mx, c)
        for p in range(npair):
            if full:
                res = _dot(a_ref[2 * p * blk:2 * (p + 1) * blk, first_full * blk:nb * blk],
                           kv_rows(v_ref, p, [e[0] for e in full]))
                for hh in range(2):
                    if fresh:
                        acc_ref[2 * p + hh] = res[hh * blk:(hh + 1) * blk]
                    else:
                        acc_ref[2 * p + hh] += res[hh * blk:(hh + 1) * blk]
            if partial:
                lhs = jnp.concatenate([a_ref[2 * p * blk + lo0:2 * p * blk + hi0, :blk],
                                       a_ref[(2 * p + 1) * blk + lo0:(2 * p + 1) * blk + hi0, :blk]],
                                      axis=0)
                res = _dot(lhs, kv_rows(v_ref, p, [blk0]))
                acc_ref[2 * p, lo0:hi0] += res[:n0]
                acc_ref[2 * p + 1, lo0:hi0] += res[n0:]
        return jnp.max(mx), jnp.max(mx[SB_TOP_ROWS:])

    def first_visit(kq, iq):
        entries = []
        for back in range(SB_SPAN - 1, -1, -1):
            hi = SB_TOP_ROWS if back == SB_SPAN - 1 else blk
            bias = None if back == 0 else jnp.where(iq >= back, 0.0, SB_MASK_BIAS)
            entries.append((jnp.maximum(iq - back, 0), 0, hi, bias))
        return visit(kq, entries, True, True)

    def finish_oldest_block(kq, iq):
        back = SB_SPAN - 1
        bias = jnp.where(iq >= back, 0.0, SB_MASK_BIAS)
        return visit(kq, [(jnp.maximum(iq - back, 0), SB_TOP_ROWS, blk, bias)], False, False)[0]

    def continue_walk(kq, iq, m0):
        def cond(st):
            j, m = st
            return jnp.logical_and(j >= 0, m > SB_EXIT_LOG2)

        def body(st):
            j, _ = st
            return j - 1, visit(kq, [(j, 0, blk, None)], False, False)[0]

        lax.while_loop(cond, body, (iq - SB_SPAN, m0))

    carries = []
    for kq in range(n_qblk):
        load_queries(kq)
        carries.append(first_visit(kq, iq0 + kq))
    for kq in range(n_qblk):
        iq = iq0 + kq
        m_all, m_rest = carries[kq]
        m_all = lax.cond(m_rest > SB_EXIT_LOG2, lambda: finish_oldest_block(kq, iq), lambda: m_all)
        continue_walk(kq, iq, m_all)
        for p in range(npair):
            o = jnp.where(lo_half, acc_all_ref[kq, 2 * p], acc_all_ref[kq, 2 * p + 1])
            o_ref[0, kq * blk:(kq + 1) * blk, p * LANES:(p + 1) * LANES] = o.astype(BF16)


def _sb_attn(qkv, *, heads):
    b, s, w3 = qkv.shape
    w = w3 // 3
    blk = SB_BLOCK
    nq = SB_TILE // blk
    cm = _suffix_sum_matrix()
    kern = functools.partial(_sb_kernel, heads=heads)
    return pl.pallas_call(
        kern,
        grid=(b, s // SB_TILE),
        in_specs=[pl.BlockSpec((1, SB_TILE, w), lambda bi, i: (bi, i, 0)),
                  pl.BlockSpec((1, s, w), lambda bi, i: (bi, 0, 1)),
                  pl.BlockSpec((1, s, w), lambda bi, i: (bi, 0, 2)),
                  _const_spec(cm.shape)],
        out_specs=pl.BlockSpec((1, SB_TILE, w), lambda bi, i: (bi, i, 0)),
        out_shape=jax.ShapeDtypeStruct((b, s, w), BF16),
        scratch_shapes=[pltpu.VMEM((nq, heads // 2, 2 * blk, LANES), BF16),
                        pltpu.VMEM((nq, heads // 2, 2 * blk, SB_SPAN * blk), F32),
                        pltpu.VMEM((nq, heads // 2 * SB_SPAN * blk, 2 * blk), BF16),
                        pltpu.VMEM((nq, heads * blk, SB_SPAN * blk), BF16),
                        pltpu.VMEM((nq, heads, blk, LANES), F32),
                        pltpu.VMEM((nq, heads, blk, 1), F32)],
        compiler_params=pltpu.CompilerParams(dimension_semantics=("arbitrary", "arbitrary"),
                                             vmem_limit_bytes=VMEM_LIMIT),
        name="sb_attn",
    )(qkv, qkv, qkv, cm)


def _suffix_sum_matrix():
    blk = SB_BLOCK
    j = jnp.arange(2 * blk)[:, None]
    s = jnp.arange(2 * blk)[None, :]
    same_head = (j // blk) == (s // blk)
    return jnp.where(jnp.logical_and(same_head, j >= s), -1.0, 0.0).astype(BF16)


def _mix_kernel(x_ref, pin_ref, halo_ref, ysb_ref, xq_ref, kv_ref, band_ref, gpre_ref, wg_ref,
                wmix_ref, pscale_ref, wpo_ref, wsbo_ref, wxo_ref, wout_ref, gpost_ref, o_ref,
                ext_ref, *, seq, x_heads):
    t = pl.program_id(0)
    tm, d = x_ref.shape
    x = x_ref[...]
    nb = _rmsnorm(x, gpre_ref[...]).astype(BF16)

    tok0 = (t * tm) % seq
    halo = halo_ref[...]
    ext_ref[:POOL_BLK, :] = jnp.where(tok0 == 0, jnp.zeros_like(halo), halo)
    ext_ref[POOL_BLK:, :] = pin_ref[...]
    pos1 = tok0 + 1 + lax.broadcasted_iota(jnp.int32, (tm, 1), 0)
    groups = []
    for g, w in enumerate(POOL_WINDOWS):
        cs = slice(g * LANES, (g + 1) * LANES)
        inv = 1.0 / jnp.minimum(pos1, w).astype(F32)
        chunks = []
        for c in range(tm // POOL_BLK):
            wsum = _dot(band_ref[g], ext_ref[c * POOL_BLK:(c + 2) * POOL_BLK, cs])
            cur = ext_ref[(c + 1) * POOL_BLK:(c + 2) * POOL_BLK, cs].astype(F32)
            chunks.append(wsum * inv[c * POOL_BLK:(c + 1) * POOL_BLK] - cur)
        mixed = jnp.concatenate(chunks, axis=0)
        groups.append(_dot(mixed.astype(BF16), wmix_ref[g]))
    ypool = jnp.concatenate(groups, axis=1) * pscale_ref[...]
    merged = jax.nn.sigmoid(_dot(nb, wg_ref[:, :d])) * _dot(ypool.astype(BF16), wpo_ref[...])

    merged += jax.nn.sigmoid(_dot(nb, wg_ref[:, d:2 * d])) * _dot(ysb_ref[...], wsbo_ref[...])

    m_len = kv_ref.shape[1]
    xw = x_heads * HEAD_DIM
    lane = lax.broadcasted_iota(jnp.int32, (tm, LANES), 1)
    lo_half = lane < HEAD_DIM
    ones = jnp.ones((m_len, LANES), BF16)
    pairs = []
    for p in range(x_heads // 2):
        cs = slice(p * LANES, (p + 1) * LANES)
        xq = xq_ref[:, cs]
        mk = kv_ref[0, :, cs]
        mv1 = jnp.concatenate([kv_ref[0, :, xw + p * LANES:xw + (p + 1) * LANES], ones], axis=1)
        outs = []
        for hh in range(2):
            keep = lo_half if hh == 0 else jnp.logical_not(lo_half)
            sc = _dot_nt(jnp.where(keep, xq, jnp.zeros_like(xq)), mk)
            e = jnp.exp(sc - jnp.max(sc, axis=-1, keepdims=True)).astype(BF16)
            r = _dot(e, mv1)
            outs.append(r[:, :LANES] / r[:, LANES:])
        pairs.append(jnp.where(lo_half, outs[0], outs[1]))
    yx = jnp.concatenate(pairs, axis=1).astype(BF16)
    merged += jax.nn.sigmoid(_dot(nb, wg_ref[:, 2 * d:])) * _dot(yx, wxo_ref[...])

    mo = _dot(merged.astype(BF16), wout_ref[...])
    o_ref[...] = x + _rmsnorm(mo, gpost_ref[...])


def _mix(x2, pin, ysb, xq, kv, gpre, w_in, wmix, pscale, wpo, wsbo, wxo, wout, gpost, *, tm, seq,
         x_heads, gate_col0):
    t, d = x2.shape
    pw = pin.shape[1]
    hb = tm // POOL_BLK
    n_gate = w_in.shape[1] - gate_col0
    band = _pool_band_matrices()
    kern = functools.partial(_mix_kernel, seq=seq, x_heads=x_heads)
    return pl.pallas_call(
        kern,
        grid=(t // tm,),
        in_specs=[pl.BlockSpec((tm, d), lambda i: (i, 0)),
                  pl.BlockSpec((tm, pw), lambda i: (i, 0)),
                  pl.BlockSpec((POOL_BLK, pw), lambda i: (jnp.maximum(i * hb - 1, 0), 0)),
                  pl.BlockSpec((tm, ysb.shape[1]), lambda i: (i, 0)),
                  pl.BlockSpec((tm, xq.shape[1]), lambda i: (i, 0)),
                  pl.BlockSpec((1,) + kv.shape[1:], lambda i: ((i * tm) // seq, 0, 0)),
                  _const_spec(band.shape),
                  _const_spec(gpre.shape), _col_window_spec(d, gate_col0, n_gate),
                  _const_spec(wmix.shape),
                  _const_spec(pscale.shape), _const_spec(wpo.shape), _const_spec(wsbo.shape),
                  _const_spec(wxo.shape), _const_spec(wout.shape), _const_spec(gpost.shape)],
        out_specs=pl.BlockSpec((tm, d), lambda i: (i, 0)),
        out_shape=jax.ShapeDtypeStruct((t, d), F32),
        scratch_shapes=[pltpu.VMEM((tm + POOL_BLK, pw), BF16)],
        compiler_params=pltpu.CompilerParams(dimension_semantics=("arbitrary",),
                                             vmem_limit_bytes=VMEM_LIMIT),
        name="mix",
    )(x2, pin, pin, ysb, xq, kv, band, gpre, w_in, wmix, pscale, wpo, wsbo, wxo, wout, gpost)


def _pool_band_matrices():
    t = jnp.arange(POOL_BLK)[None, :, None] + POOL_BLK
    j = jnp.arange(2 * POOL_BLK)[None, None, :]
    w = jnp.asarray(POOL_WINDOWS)[:, None, None]
    return jnp.where(jnp.logical_and(j <= t, j > t - w), 1.0, 0.0).astype(BF16)


def _ffn_kernel(h_ref, gpre_ref, win_ref, wout_ref, gpost_ref, o_ref, a_ref, *, d_ff, chunk):
    h = h_ref[...]
    nb = _rmsnorm(h, gpre_ref[...]).astype(BF16)
    for c in range(d_ff // chunk):
        g = _dot(nb, win_ref[:, c * chunk:(c + 1) * chunk])
        u = _dot(nb, win_ref[:, d_ff + c * chunk:d_ff + (c + 1) * chunk])
        a_ref[:, c * chunk:(c + 1) * chunk] = (g * jax.nn.sigmoid(g) * u).astype(BF16)
    ff = _dot(a_ref[...], wout_ref[...])
    o_ref[...] = h + _rmsnorm(ff, gpost_ref[...])


def _ffn(h, gpre, win, wout, gpost, *, tm, chunk):
    t, d = h.shape
    d_ff = wout.shape[0]
    kern = functools.partial(_ffn_kernel, d_ff=d_ff, chunk=chunk)
    return pl.pallas_call(
        kern,
        grid=(t // tm,),
        in_specs=[pl.BlockSpec((tm, d), lambda i: (i, 0)),
                  _const_spec(gpre.shape), _const_spec(win.shape), _const_spec(wout.shape),
                  _const_spec(gpost.shape)],
        out_specs=pl.BlockSpec((tm, d), lambda i: (i, 0)),
        out_shape=jax.ShapeDtypeStruct((t, d), F32),
        scratch_shapes=[pltpu.VMEM((tm, d_ff), BF16)],
        compiler_params=pltpu.CompilerParams(dimension_semantics=("arbitrary",),
                                             vmem_limit_bytes=VMEM_LIMIT),
        name="ffn",
    )(h, gpre, win, wout, gpost)


def kernel(x, mem, norm_mix_pre, w_in, w_pool_mix, pool_scale, w_pool_o, w_sb_o, norm_mem,
           w_mem_kv, w_x_o, w_out, norm_mix_post, norm_ffn_pre, w_ffn_in, w_ffn_out,
           norm_ffn_post):
    b, s, d = x.shape
    depth = w_in.shape[0]
    pool_w = w_pool_o.shape[1]
    sb_w = w_sb_o.shape[1]
    x_w = w_x_o.shape[1]
    sb_heads = sb_w // HEAD_DIM
    x_heads = x_w // HEAD_DIM
    split = pool_w + 3 * sb_w + x_w

    h = x.reshape(b * s, d)
    for l in range(depth):
        row = lambda v: v[l].reshape(1, -1)
        kv = _mem_kv(mem, row(norm_mem), w_mem_kv[l])
        pin, qkv, xq = _in_proj(h, row(norm_mix_pre), w_in[l], tm=TM_IN_PROJ, pool_w=pool_w,
                                sb_w=sb_w, x_w=x_w)
        ysb = _sb_attn(qkv.reshape(b, s, 3 * sb_w), heads=sb_heads).reshape(b * s, sb_w)
        h = _mix(h, pin, ysb, xq, kv, row(norm_mix_pre), w_in[l], w_pool_mix[l], row(pool_scale),
                 w_pool_o[l], w_sb_o[l], w_x_o[l], w_out[l], row(norm_mix_post), tm=TM_MIX, seq=s,
                 x_heads=x_heads, gate_col0=split)
        h = _ffn(h, row(norm_ffn_pre), w_ffn_in[l], w_ffn_out[l], row(norm_ffn_post), tm=TM_FFN,
                 chunk=256)
    return h.reshape(b, s, d)
```

```python
import functools

import jax
import jax.numpy as jnp
from jax import lax
from jax.experimental import pallas as pl
from jax.experimental.pallas import tpu as pltpu

F32 = jnp.float32
BF16 = jnp.bfloat16

RMS_EPS = 1e-6
POOL_WINDOWS = (2, 4, 8, 16)
POOL_BLK = 128
LANES = 128
HEAD_DIM = 64
SB_BLOCK = 128
SB_SPAN = 3
SB_TILE = 512
SB_TOP_ROWS = 48
LOG2E = 1.4426950408889634
SB_EXIT_LOG2 = -150.0
SB_MASK_BIAS = -1e30
SB_LINEAR_Z = 64.0
VMEM_LIMIT = 56 * 1024 * 1024
TM_IN_PROJ = 1024
TM_MIX = 1024
TM_FFN = 512


def _rmsnorm(x, g):
    ms = jnp.mean(x * x, axis=-1, keepdims=True)
    return x * lax.rsqrt(ms + RMS_EPS) * g


def _dot(a, b):
    return jnp.dot(a, b, preferred_element_type=F32)


def _dot_nt(a, b):
    return lax.dot_general(a, b, (((1,), (1,)), ((), ())), preferred_element_type=F32)


def _const_spec(shape):
    nd = len(shape)
    return pl.BlockSpec(shape, lambda *_: (0,) * nd, pipeline_mode=pl.Buffered(1))


def _col_window_spec(rows, col0, ncols):
    return pl.BlockSpec((pl.Element(rows), pl.Element(ncols)), lambda *_: (0, col0),
                        pipeline_mode=pl.Buffered(1))


def _mem_kv_kernel(mem_ref, g_ref, w_ref, o_ref):
    n = _rmsnorm(mem_ref[0], g_ref[...])
    o_ref[0] = _dot(n.astype(BF16), w_ref[...]).astype(BF16)


def _mem_kv(mem, g, w):
    b, m, d = mem.shape
    n_out = w.shape[1]
    return pl.pallas_call(
        _mem_kv_kernel,
        grid=(b,),
        in_specs=[pl.BlockSpec((1, m, d), lambda i: (i, 0, 0)),
                  _const_spec((1, d)),
                  _const_spec((d, n_out))],
        out_specs=pl.BlockSpec((1, m, n_out), lambda i: (i, 0, 0)),
        out_shape=jax.ShapeDtypeStruct((b, m, n_out), BF16),
        compiler_params=pltpu.CompilerParams(dimension_semantics=("arbitrary",)),
        name="mem_kv",
    )(mem, g, w)


def _in_proj_kernel(x_ref, g_ref, w_ref, p_ref, qkv_ref, xq_ref, *, pool_w, sb_w):
    nb = _rmsnorm(x_ref[...], g_ref[...]).astype(BF16)
    scale = 1.0 / (HEAD_DIM ** 0.5)
    o0, o1, o2 = pool_w, pool_w + sb_w, pool_w + 3 * sb_w
    p_ref[...] = _dot(nb, w_ref[:, :o0]).astype(BF16)
    qkv_ref[:, :sb_w] = (_dot(nb, w_ref[:, o0:o1]) * (scale * LOG2E)).astype(BF16)
    qkv_ref[:, sb_w:] = _dot(nb, w_ref[:, o1:o2]).astype(BF16)
    xq_ref[...] = (_dot(nb, w_ref[:, o2:]) * scale).astype(BF16)


def _in_proj(x2, g, w, *, tm, pool_w, sb_w, x_w):
    t, d = x2.shape
    n_in = pool_w + 3 * sb_w + x_w
    kern = functools.partial(_in_proj_kernel, pool_w=pool_w, sb_w=sb_w)
    return pl.pallas_call(
        kern,
        grid=(t // tm,),
        in_specs=[pl.BlockSpec((tm, d), lambda i: (i, 0)),
                  _const_spec((1, d)),
                  _col_window_spec(d, 0, n_in)],
        out_specs=[pl.BlockSpec((tm, pool_w), lambda i: (i, 0)),
                   pl.BlockSpec((tm, 3 * sb_w), lambda i: (i, 0)),
                   pl.BlockSpec((tm, x_w), lambda i: (i, 0))],
        out_shape=[jax.ShapeDtypeStruct((t, pool_w), BF16),
                   jax.ShapeDtypeStruct((t, 3 * sb_w), BF16),
                   jax.ShapeDtypeStruct((t, x_w), BF16)],
        compiler_params=pltpu.CompilerParams(dimension_semantics=("arbitrary",),
                                             vmem_limit_bytes=VMEM_LIMIT),
        name="in_proj",
    )(x2, g, w)


def _sb_kernel(q_ref, k_ref, v_ref, cm_ref, o_ref, qm_all_ref, z_all_ref, cat_all_ref, a_all_ref,
               acc_all_ref, carry_all_ref, *, heads):
    blk = SB_BLOCK
    npair = heads // 2
    n_qblk = q_ref.shape[1] // blk
    iq0 = pl.program_id(1) * n_qblk
    row = lax.broadcasted_iota(jnp.int32, (blk, blk), 0)
    col = lax.broadcasted_iota(jnp.int32, (blk, blk), 1)
    tri = col < row
    lo_half = col < HEAD_DIM

    def load_queries(kq):
        for p in range(npair):
            qp = q_ref[0, kq * blk:(kq + 1) * blk, p * LANES:(p + 1) * LANES]
            zero = jnp.zeros_like(qp)
            qm_all_ref[kq, p, :blk] = jnp.where(lo_half, qp, zero)
            qm_all_ref[kq, p, blk:] = jnp.where(lo_half, zero, qp)

    def visit(kq, entries, diag, fresh):
        qm_ref, z_ref, cat_ref, a_ref, acc_ref, carry_ref = (
            r.at[kq] for r in (qm_all_ref, z_all_ref, cat_all_ref, a_all_ref, acc_all_ref,
                               carry_all_ref))
        nb = len(entries)
        assert all((lo, hi) == (0, blk) for _, lo, hi, _ in entries[1:])
        blk0, lo0, hi0, bias0 = entries[0]
        n0 = hi0 - lo0
        partial = n0 != blk
        full = entries[1:] if partial else entries
        first_full = 1 if partial else 0

        def kv_rows(ref, p, blocks):
            parts = [ref[0, pl.ds(pl.multiple_of(bk * blk, blk), blk), p * LANES:(p + 1) * LANES]
                     for bk in blocks]
            return parts[0] if len(parts) == 1 else jnp.concatenate(parts, axis=0)

        def biased(z, bias):
            return z if bias is None else z + bias

        for p in range(npair):
            if full:
                z = _dot_nt(qm_ref[p], kv_rows(k_ref, p, [e[0] for e in full]))
                for i, e in enumerate(full):
                    sl = first_full + i
                    z_ref[p, :, sl * blk:(sl + 1) * blk] = biased(z[:, i * blk:(i + 1) * blk], e[3])
            if partial:
                lhs = jnp.concatenate([qm_ref[p, lo0:hi0], qm_ref[p, blk + lo0:blk + hi0]], axis=0)
                z = biased(_dot_nt(lhs, kv_rows(k_ref, p, [blk0])), bias0)
                z_ref[p, lo0:hi0, :blk] = z[:n0]
                z_ref[p, blk + lo0:blk + hi0, :blk] = z[n0:]

        def rows_of(sl):
            return (lo0, hi0) if (partial and sl == 0) else (0, blk)

        for h in range(heads):
            p, hh = divmod(h, 2)
            for sl in range(nb):
                lo, hi = rows_of(sl)
                z = z_ref[p, hh * blk + lo:hh * blk + hi, sl * blk:(sl + 1) * blk]
                sp = jnp.where(z > SB_LINEAR_Z, z, jnp.log2(1.0 + jnp.exp2(z)))
                if diag and sl == nb - 1:
                    sp = jnp.where(tri, sp, 0.0)
                r0 = (h * SB_SPAN + sl + 1) * blk - (hi - lo)
                cat_ref[r0:r0 + hi - lo, :] = sp.astype(BF16)
        mx = None
        for h in range(heads):
            p, hh = divmod(h, 2)
            r_start = (h * SB_SPAN + 1) * blk - n0
            r_end = (h * SB_SPAN + nb) * blk
            r = _dot(cat_ref[r_start:r_end, :], cm_ref[...])
            c = None if fresh else carry_ref[h]
            for sl in reversed(range(nb)):
                lo, hi = rows_of(sl)
                off = 0 if sl == 0 else n0 + (sl - 1) * blk
                arg = (z_ref[p, hh * blk + lo:hh * blk + hi, sl * blk:(sl + 1) * blk]
                       + r[off:off + hi - lo, :blk])
                if c is not None:
                    arg = arg + c[lo:hi]
                a = jnp.exp2(arg)
                if diag and sl == nb - 1:
                    a = jnp.where(tri, a, 0.0)
                a_ref[h * blk + lo:h * blk + hi, sl * blk:(sl + 1) * blk] = a.astype(BF16)
                tot = r[off:off + hi - lo, blk:]
                if c is None:
                    c = tot
                elif (lo, hi) == (0, blk):
                    c = c + tot
                else:
                    pieces = ([c[:lo]] if lo else []) + [c[lo:hi] + tot] + ([c[hi:]] if hi < blk else [])
                    c = jnp.concatenate(pieces, axis=0)
            carry_ref[h] = c
            mx = c if mx is None else jnp.maximum(mx, c)
        for p in range(npair):
            if full:
                res = _dot(a_ref[2 * p * blk:2 * (p + 1) * blk, first_full * blk:nb * blk],
                           kv_rows(v_ref, p, [e[0] for e in full]))
                for hh in range(2):
                    if fresh:
                        acc_ref[2 * p + hh] = res[hh * blk:(hh + 1) * blk]
                    else:
                        acc_ref[2 * p + hh] += res[hh * blk:(hh + 1) * blk]
            if partial:
                lhs = jnp.concatenate([a_ref[2 * p * blk + lo0:2 * p * blk + hi0, :blk],
                                       a_ref[(2 * p + 1) * blk + lo0:(2 * p + 1) * blk + hi0, :blk]],
                                      axis=0)
                res = _dot(lhs, kv_rows(v_ref, p, [blk0]))
                acc_ref[2 * p, lo0:hi0] += res[:n0]
                acc_ref[2 * p + 1, lo0:hi0] += res[n0:]
        return jnp.max(mx), jnp.max(mx[SB_TOP_ROWS:])

    def first_visit(kq, iq):
        entries = []
        for back in range(SB_SPAN - 1, -1, -1):
            hi = SB_TOP_ROWS if back == SB_SPAN - 1 else blk
            bias = None if back == 0 else jnp.where(iq >= back, 0.0, SB_MASK_BIAS)
            entries.append((jnp.maximum(iq - back, 0), 0, hi, bias))
        return visit(kq, entries, True, True)

    def finish_oldest_block(kq, iq):
        back = SB_SPAN - 1
        bias = jnp.where(iq >= back, 0.0, SB_MASK_BIAS)
        return visit(kq, [(jnp.maximum(iq - back, 0), SB_TOP_ROWS, blk, bias)], False, False)[0]

    def continue_walk(kq, iq, m0):
        def cond(st):
            j, m = st
            return jnp.logical_and(j >= 0, m > SB_EXIT_LOG2)

        def body(st):
            j, _ = st
            return j - 1, visit(kq, [(j, 0, blk, None)], False, False)[0]

        lax.while_loop(cond, body, (iq - SB_SPAN, m0))

    carries = []
    for kq in range(n_qblk):
        load_queries(kq)
        carries.append(first_visit(kq, iq0 + kq))
    for kq in range(n_qblk):
        iq = iq0 + kq
        m_all, m_rest = carries[kq]
        m_all = lax.cond(m_rest > SB_EXIT_LOG2, lambda: finish_oldest_block(kq, iq), lambda: m_all)
        continue_walk(kq, iq, m_all)
        for p in range(npair):
            o = jnp.where(lo_half, acc_all_ref[kq, 2 * p], acc_all_ref[kq, 2 * p + 1])
            o_ref[0, kq * blk:(kq + 1) * blk, p * LANES:(p + 1) * LANES] = o.astype(BF16)


def _sb_attn(qkv, *, heads):
    b, s, w3 = qkv.shape
    w = w3 // 3
    blk = SB_BLOCK
    nq = SB_TILE // blk
    cm = _suffix_sum_matrix()
    kern = functools.partial(_sb_kernel, heads=heads)
    return pl.pallas_call(
        kern,
        grid=(b, s // SB_TILE),
        in_specs=[pl.BlockSpec((1, SB_TILE, w), lambda bi, i: (bi, i, 0)),
                  pl.BlockSpec((1, s, w), lambda bi, i: (bi, 0, 1)),
                  pl.BlockSpec((1, s, w), lambda bi, i: (bi, 0, 2)),
                  _const_spec(cm.shape)],
        out_specs=pl.BlockSpec((1, SB_TILE, w), lambda bi, i: (bi, i, 0)),
        out_shape=jax.ShapeDtypeStruct((b, s, w), BF16),
        scratch_shapes=[pltpu.VMEM((nq, heads // 2, 2 * blk, LANES), BF16),
                        pltpu.VMEM((nq, heads // 2, 2 * blk, SB_SPAN * blk), F32),
                        pltpu.VMEM((nq, heads * SB_SPAN * blk, blk), BF16),
                        pltpu.VMEM((nq, heads * blk, SB_SPAN * blk), BF16),
                        pltpu.VMEM((nq, heads, blk, LANES), F32),
                        pltpu.VMEM((nq, heads, blk, LANES), F32)],
        compiler_params=pltpu.CompilerParams(dimension_semantics=("arbitrary", "arbitrary"),
                                             vmem_limit_bytes=VMEM_LIMIT),
        name="sb_attn",
    )(qkv, qkv, qkv, cm)


def _suffix_sum_matrix():
    blk = SB_BLOCK
    j = jnp.arange(blk)[:, None]
    s = jnp.arange(2 * blk)[None, :]
    return jnp.where(jnp.logical_or(s >= blk, j >= s), -1.0, 0.0).astype(BF16)


def _mix_kernel(x_ref, pin_ref, halo_ref, ysb_ref, xq_ref, kv_ref, band_ref, gpre_ref, wg_ref,
                wmix_ref, pscale_ref, wpo_ref, wsbo_ref, wxo_ref, wout_ref, gpost_ref, o_ref,
                ext_ref, *, seq, x_heads):
    t = pl.program_id(0)
    tm, d = x_ref.shape
    x = x_ref[...]
    nb = _rmsnorm(x, gpre_ref[...]).astype(BF16)

    tok0 = (t * tm) % seq
    halo = halo_ref[...]
    ext_ref[:POOL_BLK, :] = jnp.where(tok0 == 0, jnp.zeros_like(halo), halo)
    ext_ref[POOL_BLK:, :] = pin_ref[...]
    pos1 = tok0 + 1 + lax.broadcasted_iota(jnp.int32, (tm, 1), 0)
    groups = []
    for g, w in enumerate(POOL_WINDOWS):
        cs = slice(g * LANES, (g + 1) * LANES)
        inv = 1.0 / jnp.minimum(pos1, w).astype(F32)
        chunks = []
        for c in range(tm // POOL_BLK):
            wsum = _dot(band_ref[g], ext_ref[c * POOL_BLK:(c + 2) * POOL_BLK, cs])
            cur = ext_ref[(c + 1) * POOL_BLK:(c + 2) * POOL_BLK, cs].astype(F32)
            chunks.append(wsum * inv[c * POOL_BLK:(c + 1) * POOL_BLK] - cur)
        mixed = jnp.concatenate(chunks, axis=0)
        groups.append(_dot(mixed.astype(BF16), wmix_ref[g]))
    ypool = jnp.concatenate(groups, axis=1) * pscale_ref[...]
    merged = jax.nn.sigmoid(_dot(nb, wg_ref[:, :d])) * _dot(ypool.astype(BF16), wpo_ref[...])

    merged += jax.nn.sigmoid(_dot(nb, wg_ref[:, d:2 * d])) * _dot(ysb_ref[...], wsbo_ref[...])

    m_len = kv_ref.shape[1]
    xw = x_heads * HEAD_DIM
    lane = lax.broadcasted_iota(jnp.int32, (tm, LANES), 1)
    lo_half = lane < HEAD_DIM
    ones = jnp.ones((m_len, LANES), BF16)
    pairs = []
    for p in range(x_heads // 2):
        cs = slice(p * LANES, (p + 1) * LANES)
        xq = xq_ref[:, cs]
        mk = kv_ref[0, :, cs]
        mv1 = jnp.concatenate([kv_ref[0, :, xw + p * LANES:xw + (p + 1) * LANES], ones], axis=1)
        outs = []
        for hh in range(2):
            keep = lo_half if hh == 0 else jnp.logical_not(lo_half)
            sc = _dot_nt(jnp.where(keep, xq, jnp.zeros_like(xq)), mk)
            e = jnp.exp(sc - jnp.max(sc, axis=-1, keepdims=True)).astype(BF16)
            r = _dot(e, mv1)
            outs.append(r[:, :LANES] / r[:, LANES:])
        pairs.append(jnp.where(lo_half, outs[0], outs[1]))
    yx = jnp.concatenate(pairs, axis=1).astype(BF16)
    merged += jax.nn.sigmoid(_dot(nb, wg_ref[:, 2 * d:])) * _dot(yx, wxo_ref[...])

    mo = _dot(merged.astype(BF16), wout_ref[...])
    o_ref[...] = x + _rmsnorm(mo, gpost_ref[...])


def _mix(x2, pin, ysb, xq, kv, gpre, w_in, wmix, pscale, wpo, wsbo, wxo, wout, gpost, *, tm, seq,
         x_heads, gate_col0):
    t, d = x2.shape
    pw = pin.shape[1]
    hb = tm // POOL_BLK
    n_gate = w_in.shape[1] - gate_col0
    band = _pool_band_matrices()
    kern = functools.partial(_mix_kernel, seq=seq, x_heads=x_heads)
    return pl.pallas_call(
        kern,
        grid=(t // tm,),
        in_specs=[pl.BlockSpec((tm, d), lambda i: (i, 0)),
                  pl.BlockSpec((tm, pw), lambda i: (i, 0)),
                  pl.BlockSpec((POOL_BLK, pw), lambda i: (jnp.maximum(i * hb - 1, 0), 0)),
                  pl.BlockSpec((tm, ysb.shape[1]), lambda i: (i, 0)),
                  pl.BlockSpec((tm, xq.shape[1]), lambda i: (i, 0)),
                  pl.BlockSpec((1,) + kv.shape[1:], lambda i: ((i * tm) // seq, 0, 0)),
                  _const_spec(band.shape),
                  _const_spec(gpre.shape), _col_window_spec(d, gate_col0, n_gate),
                  _const_spec(wmix.shape),
                  _const_spec(pscale.shape), _const_spec(wpo.shape), _const_spec(wsbo.shape),
                  _const_spec(wxo.shape), _const_spec(wout.shape), _const_spec(gpost.shape)],
        out_specs=pl.BlockSpec((tm, d), lambda i: (i, 0)),
        out_shape=jax.ShapeDtypeStruct((t, d), F32),
        scratch_shapes=[pltpu.VMEM((tm + POOL_BLK, pw), BF16)],
        compiler_params=pltpu.CompilerParams(dimension_semantics=("arbitrary",),
                                             vmem_limit_bytes=VMEM_LIMIT),
        name="mix",
    )(x2, pin, pin, ysb, xq, kv, band, gpre, w_in, wmix, pscale, wpo, wsbo, wxo, wout, gpost)


def _pool_band_matrices():
    t = jnp.arange(POOL_BLK)[None, :, None] + POOL_BLK
    j = jnp.arange(2 * POOL_BLK)[None, None, :]
    w = jnp.asarray(POOL_WINDOWS)[:, None, None]
    return jnp.where(jnp.logical_and(j <= t, j > t - w), 1.0, 0.0).astype(BF16)


def _ffn_kernel(h_ref, gpre_ref, win_hbm, wout_hbm, gpost_ref, o_ref, win_ref, wout_ref, a_ref,
                sem, *, d_ff, chunk):
    n_chunks = d_ff // chunk

    def gate_copy(c):
        cols = pl.ds(c * chunk, chunk)
        return pltpu.make_async_copy(win_hbm.at[:, cols], win_ref.at[:, cols], sem.at[2 * c])

    def up_copy(c):
        cols = pl.ds(d_ff + c * chunk, chunk)
        return pltpu.make_async_copy(win_hbm.at[:, cols], win_ref.at[:, cols], sem.at[2 * c + 1])

    def out_copy():
        return pltpu.make_async_copy(wout_hbm, wout_ref, sem.at[2 * n_chunks])

    def tile(first_step):
        h = h_ref[...]
        nb = _rmsnorm(h, gpre_ref[...]).astype(BF16)
        for c in range(n_chunks):
            if first_step:
                gate_copy(c).wait()
                up_copy(c).wait()
            g = _dot(nb, win_ref[:, c * chunk:(c + 1) * chunk])
            u = _dot(nb, win_ref[:, d_ff + c * chunk:d_ff + (c + 1) * chunk])
            a_ref[:, c * chunk:(c + 1) * chunk] = (g * jax.nn.sigmoid(g) * u).astype(BF16)
        if first_step:
            out_copy().wait()
        ff = _dot(a_ref[...], wout_ref[...])
        o_ref[...] = h + _rmsnorm(ff, gpost_ref[...])

    @pl.when(pl.program_id(0) == 0)
    def _():
        for c in range(n_chunks):
            gate_copy(c).start()
            up_copy(c).start()
        out_copy().start()
        tile(True)

    @pl.when(pl.program_id(0) > 0)
    def _():
        tile(False)


def _ffn(h, gpre, win, wout, gpost, *, tm, chunk):
    t, d = h.shape
    d_ff = wout.shape[0]
    kern = functools.partial(_ffn_kernel, d_ff=d_ff, chunk=chunk)
    return pl.pallas_call(
        kern,
        grid=(t // tm,),
        in_specs=[pl.BlockSpec((tm, d), lambda i: (i, 0)),
                  _const_spec(gpre.shape),
                  pl.BlockSpec(memory_space=pl.ANY), pl.BlockSpec(memory_space=pl.ANY),
                  _const_spec(gpost.shape)],
        out_specs=pl.BlockSpec((tm, d), lambda i: (i, 0)),
        out_shape=jax.ShapeDtypeStruct((t, d), F32),
        scratch_shapes=[pltpu.VMEM(win.shape, win.dtype),
                        pltpu.VMEM(wout.shape, wout.dtype),
                        pltpu.VMEM((tm, d_ff), BF16),
                        pltpu.SemaphoreType.DMA((2 * (d_ff // chunk) + 1,))],
        compiler_params=pltpu.CompilerParams(dimension_semantics=("arbitrary",),
                                             vmem_limit_bytes=VMEM_LIMIT),
        name="ffn",
    )(h, gpre, win, wout, gpost)


def kernel(x, mem, norm_mix_pre, w_in, w_pool_mix, pool_scale, w_pool_o, w_sb_o, norm_mem,
           w_mem_kv, w_x_o, w_out, norm_mix_post, norm_ffn_pre, w_ffn_in, w_ffn_out,
           norm_ffn_post):
    b, s, d = x.shape
    depth = w_in.shape[0]
    pool_w = w_pool_o.shape[1]
    sb_w = w_sb_o.shape[1]
    x_w = w_x_o.shape[1]
    sb_heads = sb_w // HEAD_DIM
    x_heads = x_w // HEAD_DIM
    split = pool_w + 3 * sb_w + x_w

    h = x.reshape(b * s, d)
    for l in range(depth):
        row = lambda v: v[l].reshape(1, -1)
        kv = _mem_kv(mem, row(norm_mem), w_mem_kv[l])
        pin, qkv, xq = _in_proj(h, row(norm_mix_pre), w_in[l], tm=TM_IN_PROJ, pool_w=pool_w,
                                sb_w=sb_w, x_w=x_w)
        ysb = _sb_attn(qkv.reshape(b, s, 3 * sb_w), heads=sb_heads).reshape(b * s, sb_w)
        h = _mix(h, pin, ysb, xq, kv, row(norm_mix_pre), w_in[l], w_pool_mix[l], row(pool_scale),
                 w_pool_o[l], w_sb_o[l], w_x_o[l], w_out[l], row(norm_mix_post), tm=TM_MIX, seq=s,
                 x_heads=x_heads, gate_col0=split)
        h = _ffn(h, row(norm_ffn_pre), w_ffn_in[l], w_ffn_out[l], row(norm_ffn_post), tm=TM_FFN,
                 chunk=256)
    return h.reshape(b, s, d)
```

```python
import functools

import jax
import jax.numpy as jnp
from jax import lax
from jax.experimental import pallas as pl
from jax.experimental.pallas import tpu as pltpu

F32 = jnp.float32
BF16 = jnp.bfloat16

RMS_EPS = 1e-6
POOL_WINDOWS = (2, 4, 8, 16)
POOL_BLK = 128
LANES = 128
HEAD_DIM = 64
SB_BLOCK = 128
SB_SPAN = 3
SB_TILE = 512
SB_TOP_ROWS = 48
LOG2E = 1.4426950408889634
SB_EXIT_LOG2 = -150.0
SB_MASK_BIAS = -1e30
SB_LINEAR_Z = 64.0
VMEM_LIMIT = 56 * 1024 * 1024
TM_IN_PROJ = 1024
TM_MIX = 1024
TM_FFN = 512


def _rmsnorm(x, g):
    ms = jnp.mean(x * x, axis=-1, keepdims=True)
    return x * lax.rsqrt(ms + RMS_EPS) * g


def _dot(a, b):
    return jnp.dot(a, b, preferred_element_type=F32)


def _dot_nt(a, b):
    return lax.dot_general(a, b, (((1,), (1,)), ((), ())), preferred_element_type=F32)


def _const_spec(shape):
    nd = len(shape)
    return pl.BlockSpec(shape, lambda *_: (0,) * nd, pipeline_mode=pl.Buffered(1))


def _col_window_spec(rows, col0, ncols):
    return pl.BlockSpec((pl.Element(rows), pl.Element(ncols)), lambda *_: (0, col0),
                        pipeline_mode=pl.Buffered(1))


def _mem_kv_kernel(mem_ref, g_ref, w_ref, o_ref):
    n = _rmsnorm(mem_ref[0], g_ref[...])
    o_ref[0] = _dot(n.astype(BF16), w_ref[...]).astype(BF16)


def _mem_kv(mem, g, w):
    b, m, d = mem.shape
    n_out = w.shape[1]
    return pl.pallas_call(
        _mem_kv_kernel,
        grid=(b,),
        in_specs=[pl.BlockSpec((1, m, d), lambda i: (i, 0, 0)),
                  _const_spec((1, d)),
                  _const_spec((d, n_out))],
        out_specs=pl.BlockSpec((1, m, n_out), lambda i: (i, 0, 0)),
        out_shape=jax.ShapeDtypeStruct((b, m, n_out), BF16),
        compiler_params=pltpu.CompilerParams(dimension_semantics=("arbitrary",)),
        name="mem_kv",
    )(mem, g, w)


def _in_proj_kernel(x_ref, g_ref, w_ref, p_ref, qv_ref, kt_ref, xq_ref, wkt_ref, *, pool_w, sb_w):
    scale = 1.0 / (HEAD_DIM ** 0.5)
    o0 = pool_w
    oq, ok, ov, ox = o0, o0 + sb_w, o0 + 2 * sb_w, o0 + 3 * sb_w

    @pl.when(pl.program_id(0) == 0)
    def _():
        wkt_ref[...] = w_ref[:, ok:ov].T

    nb = _rmsnorm(x_ref[...], g_ref[...]).astype(BF16)
    p_ref[...] = _dot(nb, w_ref[:, :o0]).astype(BF16)
    qv_ref[:, :sb_w] = (_dot(nb, w_ref[:, oq:ok]) * (scale * LOG2E)).astype(BF16)
    qv_ref[:, sb_w:] = _dot(nb, w_ref[:, ov:ox]).astype(BF16)
    kt_ref[...] = _dot_nt(wkt_ref[...], nb).astype(BF16)
    xq_ref[...] = (_dot(nb, w_ref[:, ox:]) * scale).astype(BF16)


def _in_proj(x2, g, w, *, tm, pool_w, sb_w, x_w):
    t, d = x2.shape
    n_in = pool_w + 3 * sb_w + x_w
    kern = functools.partial(_in_proj_kernel, pool_w=pool_w, sb_w=sb_w)
    return pl.pallas_call(
        kern,
        grid=(t // tm,),
        in_specs=[pl.BlockSpec((tm, d), lambda i: (i, 0)),
                  _const_spec((1, d)),
                  _col_window_spec(d, 0, n_in)],
        out_specs=[pl.BlockSpec((tm, pool_w), lambda i: (i, 0)),
                   pl.BlockSpec((tm, 2 * sb_w), lambda i: (i, 0)),
                   pl.BlockSpec((sb_w, tm), lambda i: (0, i)),
                   pl.BlockSpec((tm, x_w), lambda i: (i, 0))],
        out_shape=[jax.ShapeDtypeStruct((t, pool_w), BF16),
                   jax.ShapeDtypeStruct((t, 2 * sb_w), BF16),
                   jax.ShapeDtypeStruct((sb_w, t), BF16),
                   jax.ShapeDtypeStruct((t, x_w), BF16)],
        scratch_shapes=[pltpu.VMEM((sb_w, d), w.dtype)],
        compiler_params=pltpu.CompilerParams(dimension_semantics=("arbitrary",),
                                             vmem_limit_bytes=VMEM_LIMIT),
        name="in_proj",
    )(x2, g, w)


def _sb_kernel(q_ref, kt_ref, v_ref, cm_ref, o_ref, qm_all_ref, z_all_ref, cat_all_ref, a_all_ref,
               acc_all_ref, carry_all_ref, *, heads):
    blk = SB_BLOCK
    npair = heads // 2
    n_qblk = q_ref.shape[1] // blk
    iq0 = pl.program_id(1) * n_qblk
    row = lax.broadcasted_iota(jnp.int32, (blk, blk), 0)
    col = lax.broadcasted_iota(jnp.int32, (blk, blk), 1)
    tri = col < row
    lo_half = col < HEAD_DIM

    def load_queries(kq):
        for p in range(npair):
            qp = q_ref[0, kq * blk:(kq + 1) * blk, p * LANES:(p + 1) * LANES]
            zero = jnp.zeros_like(qp)
            qm_all_ref[kq, p, :blk] = jnp.where(lo_half, qp, zero)
            qm_all_ref[kq, p, blk:] = jnp.where(lo_half, zero, qp)

    def visit(kq, entries, diag, fresh):
        qm_ref, z_ref, cat_ref, a_ref, acc_ref, carry_ref = (
            r.at[kq] for r in (qm_all_ref, z_all_ref, cat_all_ref, a_all_ref, acc_all_ref,
                               carry_all_ref))
        nb = len(entries)
        assert all((lo, hi) == (0, blk) for _, lo, hi, _ in entries[1:])
        blk0, lo0, hi0, bias0 = entries[0]
        n0 = hi0 - lo0
        partial = n0 != blk
        full = entries[1:] if partial else entries
        first_full = 1 if partial else 0

        def kv_rows(ref, p, blocks):
            parts = [ref[0, pl.ds(pl.multiple_of(bk * blk, blk), blk), p * LANES:(p + 1) * LANES]
                     for bk in blocks]
            return parts[0] if len(parts) == 1 else jnp.concatenate(parts, axis=0)

        def kt_cols(p, blocks):
            parts = [kt_ref[p * LANES:(p + 1) * LANES, pl.ds(pl.multiple_of(bk * blk, blk), blk)]
                     for bk in blocks]
            return parts[0] if len(parts) == 1 else jnp.concatenate(parts, axis=1)

        def biased(z, bias):
            return z if bias is None else z + bias

        for p in range(npair):
            if full:
                z = _dot(qm_ref[p], kt_cols(p, [e[0] for e in full]))
                for i, e in enumerate(full):
                    sl = first_full + i
                    z_ref[p, :, sl * blk:(sl + 1) * blk] = biased(z[:, i * blk:(i + 1) * blk], e[3])
            if partial:
                lhs = jnp.concatenate([qm_ref[p, lo0:hi0], qm_ref[p, blk + lo0:blk + hi0]], axis=0)
                z = biased(_dot(lhs, kt_cols(p, [blk0])), bias0)
                z_ref[p, lo0:hi0, :blk] = z[:n0]
                z_ref[p, blk + lo0:blk + hi0, :blk] = z[n0:]

        def rows_of(sl):
            return (lo0, hi0) if (partial and sl == 0) else (0, blk)

        for h in range(heads):
            p, hh = divmod(h, 2)
            for sl in range(nb):
                lo, hi = rows_of(sl)
                z = z_ref[p, hh * blk + lo:hh * blk + hi, sl * blk:(sl + 1) * blk]
                sp = jnp.where(z > SB_LINEAR_Z, z, jnp.log2(1.0 + jnp.exp2(z)))
                if diag and sl == nb - 1:
                    sp = jnp.where(tri, sp, 0.0)
                r0 = (h * SB_SPAN + sl + 1) * blk - (hi - lo)
                cat_ref[r0:r0 + hi - lo, :] = sp.astype(BF16)
        mx = None
        for h in range(heads):
            p, hh = divmod(h, 2)
            r_start = (h * SB_SPAN + 1) * blk - n0
            r_end = (h * SB_SPAN + nb) * blk
            r = _dot(cat_ref[r_start:r_end, :], cm_ref[...])
            c = None if fresh else carry_ref[h]
            for sl in reversed(range(nb)):
                lo, hi = rows_of(sl)
                off = 0 if sl == 0 else n0 + (sl - 1) * blk
                arg = (z_ref[p, hh * blk + lo:hh * blk + hi, sl * blk:(sl + 1) * blk]
                       + r[off:off + hi - lo, :blk])
                if c is not None:
                    arg = arg + c[lo:hi]
                a = jnp.exp2(arg)
                if diag and sl == nb - 1:
                    a = jnp.where(tri, a, 0.0)
                a_ref[h * blk + lo:h * blk + hi, sl * blk:(sl + 1) * blk] = a.astype(BF16)
                tot = r[off:off + hi - lo, blk:]
                if c is None:
                    c = tot
                elif (lo, hi) == (0, blk):
                    c = c + tot
                else:
                    pieces = ([c[:lo]] if lo else []) + [c[lo:hi] + tot] + ([c[hi:]] if hi < blk else [])
                    c = jnp.concatenate(pieces, axis=0)
            carry_ref[h] = c
            mx = c if mx is None else jnp.maximum(mx, c)
        for p in range(npair):
            if full:
                res = _dot(a_ref[2 * p * blk:2 * (p + 1) * blk, first_full * blk:nb * blk],
                           kv_rows(v_ref, p, [e[0] for e in full]))
                for hh in range(2):
                    if fresh:
                        acc_ref[2 * p + hh] = res[hh * blk:(hh + 1) * blk]
                    else:
                        acc_ref[2 * p + hh] += res[hh * blk:(hh + 1) * blk]
            if partial:
                lhs = jnp.concatenate([a_ref[2 * p * blk + lo0:2 * p * blk + hi0, :blk],
                                       a_ref[(2 * p + 1) * blk + lo0:(2 * p + 1) * blk + hi0, :blk]],
                                      axis=0)
                res = _dot(lhs, kv_rows(v_ref, p, [blk0]))
                acc_ref[2 * p, lo0:hi0] += res[:n0]
                acc_ref[2 * p + 1, lo0:hi0] += res[n0:]
        return jnp.max(mx), jnp.max(mx[SB_TOP_ROWS:])

    def first_visit(kq, iq):
        entries = []
        for back in range(SB_SPAN - 1, -1, -1):
            hi = SB_TOP_ROWS if back == SB_SPAN - 1 else blk
            bias = None if back == 0 else jnp.where(iq >= back, 0.0, SB_MASK_BIAS)
            entries.append((jnp.maximum(iq - back, 0), 0, hi, bias))
        return visit(kq, entries, True, True)

    def finish_oldest_block(kq, iq):
        back = SB_SPAN - 1
        bias = jnp.where(iq >= back, 0.0, SB_MASK_BIAS)
        return visit(kq, [(jnp.maximum(iq - back, 0), SB_TOP_ROWS, blk, bias)], False, False)[0]

    def continue_walk(kq, iq, m0):
        def cond(st):
            j, m = st
            return jnp.logical_and(j >= 0, m > SB_EXIT_LOG2)

        def body(st):
            j, _ = st
            return j - 1, visit(kq, [(j, 0, blk, None)], False, False)[0]

        lax.while_loop(cond, body, (iq - SB_SPAN, m0))

    carries = []
    for kq in range(n_qblk):
        load_queries(kq)
        carries.append(first_visit(kq, iq0 + kq))
    for kq in range(n_qblk):
        iq = iq0 + kq
        m_all, m_rest = carries[kq]
        m_all = lax.cond(m_rest > SB_EXIT_LOG2, lambda: finish_oldest_block(kq, iq), lambda: m_all)
        continue_walk(kq, iq, m_all)
        for p in range(npair):
            o = jnp.where(lo_half, acc_all_ref[kq, 2 * p], acc_all_ref[kq, 2 * p + 1])
            o_ref[0, kq * blk:(kq + 1) * blk, p * LANES:(p + 1) * LANES] = o.astype(BF16)


def _sb_attn(qv, kt, *, heads):
    b, s, w2 = qv.shape
    w = w2 // 2
    blk = SB_BLOCK
    nq = SB_TILE // blk
    cm = _suffix_sum_matrix()
    kern = functools.partial(_sb_kernel, heads=heads)
    return pl.pallas_call(
        kern,
        grid=(b, s // SB_TILE),
        in_specs=[pl.BlockSpec((1, SB_TILE, w), lambda bi, i: (bi, i, 0)),
                  pl.BlockSpec((w, s), lambda bi, i: (0, bi)),
                  pl.BlockSpec((1, s, w), lambda bi, i: (bi, 0, 1)),
                  _const_spec(cm.shape)],
        out_specs=pl.BlockSpec((1, SB_TILE, w), lambda bi, i: (bi, i, 0)),
        out_shape=jax.ShapeDtypeStruct((b, s, w), BF16),
        scratch_shapes=[pltpu.VMEM((nq, heads // 2, 2 * blk, LANES), BF16),
                        pltpu.VMEM((nq, heads // 2, 2 * blk, SB_SPAN * blk), F32),
                        pltpu.VMEM((nq, heads * SB_SPAN * blk, blk), BF16),
                        pltpu.VMEM((nq, heads * blk, SB_SPAN * blk), BF16),
                        pltpu.VMEM((nq, heads, blk, LANES), F32),
                        pltpu.VMEM((nq, heads, blk, LANES), F32)],
        compiler_params=pltpu.CompilerParams(dimension_semantics=("arbitrary", "arbitrary"),
                                             vmem_limit_bytes=VMEM_LIMIT),
        name="sb_attn",
    )(qv, kt, qv, cm)


def _suffix_sum_matrix():
    blk = SB_BLOCK
    j = jnp.arange(blk)[:, None]
    s = jnp.arange(2 * blk)[None, :]
    return jnp.where(jnp.logical_or(s >= blk, j >= s), -1.0, 0.0).astype(BF16)


def _mix_kernel(x_ref, pin_ref, halo_ref, ysb_ref, xq_ref, kv_ref, band_ref, gpre_ref, wg_ref,
                wmix_ref, pscale_ref, wpo_ref, wsbo_ref, wxo_ref, wout_ref, gpost_ref, o_ref,
                ext_ref, *, seq, x_heads):
    t = pl.program_id(0)
    tm, d = x_ref.shape
    x = x_ref[...]
    nb = _rmsnorm(x, gpre_ref[...]).astype(BF16)

    tok0 = (t * tm) % seq
    halo = halo_ref[...]
    ext_ref[:POOL_BLK, :] = jnp.where(tok0 == 0, jnp.zeros_like(halo), halo)
    ext_ref[POOL_BLK:, :] = pin_ref[...]
    pos1 = tok0 + 1 + lax.broadcasted_iota(jnp.int32, (tm, 1), 0)
    groups = []
    for g, w in enumerate(POOL_WINDOWS):
        cs = slice(g * LANES, (g + 1) * LANES)
        inv = 1.0 / jnp.minimum(pos1, w).astype(F32)
        chunks = []
        for c in range(tm // POOL_BLK):
            wsum = _dot(band_ref[g], ext_ref[c * POOL_BLK:(c + 2) * POOL_BLK, cs])
            cur = ext_ref[(c + 1) * POOL_BLK:(c + 2) * POOL_BLK, cs].astype(F32)
            chunks.append(wsum * inv[c * POOL_BLK:(c + 1) * POOL_BLK] - cur)
        mixed = jnp.concatenate(chunks, axis=0)
        groups.append(_dot(mixed.astype(BF16), wmix_ref[g]))
    ypool = jnp.concatenate(groups, axis=1) * pscale_ref[...]
    merged = jax.nn.sigmoid(_dot(nb, wg_ref[:, :d])) * _dot(ypool.astype(BF16), wpo_ref[...])

    merged += jax.nn.sigmoid(_dot(nb, wg_ref[:, d:2 * d])) * _dot(ysb_ref[...], wsbo_ref[...])

    m_len = kv_ref.shape[1]
    xw = x_heads * HEAD_DIM
    lane = lax.broadcasted_iota(jnp.int32, (tm, LANES), 1)
    lo_half = lane < HEAD_DIM
    ones = jnp.ones((m_len, LANES), BF16)
    pairs = []
    for p in range(x_heads // 2):
        cs = slice(p * LANES, (p + 1) * LANES)
        xq = xq_ref[:, cs]
        mk = kv_ref[0, :, cs]
        mv1 = jnp.concatenate([kv_ref[0, :, xw + p * LANES:xw + (p + 1) * LANES], ones], axis=1)
        outs = []
        for hh in range(2):
            keep = lo_half if hh == 0 else jnp.logical_not(lo_half)
            sc = _dot_nt(jnp.where(keep, xq, jnp.zeros_like(xq)), mk)
            e = jnp.exp(sc - jnp.max(sc, axis=-1, keepdims=True)).astype(BF16)
            r = _dot(e, mv1)
            outs.append(r[:, :LANES] / r[:, LANES:])
        pairs.append(jnp.where(lo_half, outs[0], outs[1]))
    yx = jnp.concatenate(pairs, axis=1).astype(BF16)
    merged += jax.nn.sigmoid(_dot(nb, wg_ref[:, 2 * d:])) * _dot(yx, wxo_ref[...])

    mo = _dot(merged.astype(BF16), wout_ref[...])
    o_ref[...] = x + _rmsnorm(mo, gpost_ref[...])


def _mix(x2, pin, ysb, xq, kv, gpre, w_in, wmix, pscale, wpo, wsbo, wxo, wout, gpost, *, tm, seq,
         x_heads, gate_col0):
    t, d = x2.shape
    pw = pin.shape[1]
    hb = tm // POOL_BLK
    n_gate = w_in.shape[1] - gate_col0
    band = _pool_band_matrices()
    kern = functools.partial(_mix_kernel, seq=seq, x_heads=x_heads)
    return pl.pallas_call(
        kern,
        grid=(t // tm,),
        in_specs=[pl.BlockSpec((tm, d), lambda i: (i, 0)),
                  pl.BlockSpec((tm, pw), lambda i: (i, 0)),
                  pl.BlockSpec((POOL_BLK, pw), lambda i: (jnp.maximum(i * hb - 1, 0), 0)),
                  pl.BlockSpec((tm, ysb.shape[1]), lambda i: (i, 0)),
                  pl.BlockSpec((tm, xq.shape[1]), lambda i: (i, 0)),
                  pl.BlockSpec((1,) + kv.shape[1:], lambda i: ((i * tm) // seq, 0, 0)),
                  _const_spec(band.shape),
                  _const_spec(gpre.shape), _col_window_spec(d, gate_col0, n_gate),
                  _const_spec(wmix.shape),
                  _const_spec(pscale.shape), _const_spec(wpo.shape), _const_spec(wsbo.shape),
                  _const_spec(wxo.shape), _const_spec(wout.shape), _const_spec(gpost.shape)],
        out_specs=pl.BlockSpec((tm, d), lambda i: (i, 0)),
        out_shape=jax.ShapeDtypeStruct((t, d), F32),
        scratch_shapes=[pltpu.VMEM((tm + POOL_BLK, pw), BF16)],
        compiler_params=pltpu.CompilerParams(dimension_semantics=("arbitrary",),
                                             vmem_limit_bytes=VMEM_LIMIT),
        name="mix",
    )(x2, pin, pin, ysb, xq, kv, band, gpre, w_in, wmix, pscale, wpo, wsbo, wxo, wout, gpost)


def _pool_band_matrices():
    t = jnp.arange(POOL_BLK)[None, :, None] + POOL_BLK
    j = jnp.arange(2 * POOL_BLK)[None, None, :]
    w = jnp.asarray(POOL_WINDOWS)[:, None, None]
    return jnp.where(jnp.logical_and(j <= t, j > t - w), 1.0, 0.0).astype(BF16)


def _ffn_kernel(h_ref, gpre_ref, win_ref, wout_ref, gpost_ref, o_ref, a_ref, *, d_ff, chunk):
    h = h_ref[...]
    nb = _rmsnorm(h, gpre_ref[...]).astype(BF16)
    for c in range(d_ff // chunk):
        g = _dot(nb, win_ref[:, c * chunk:(c + 1) * chunk])
        u = _dot(nb, win_ref[:, d_ff + c * chunk:d_ff + (c + 1) * chunk])
        a_ref[:, c * chunk:(c + 1) * chunk] = (g * jax.nn.sigmoid(g) * u).astype(BF16)
    ff = _dot(a_ref[...], wout_ref[...])
    o_ref[...] = h + _rmsnorm(ff, gpost_ref[...])


def _ffn(h, gpre, win, wout, gpost, *, tm, chunk):
    t, d = h.shape
    d_ff = wout.shape[0]
    kern = functools.partial(_ffn_kernel, d_ff=d_ff, chunk=chunk)
    return pl.pallas_call(
        kern,
        grid=(t // tm,),
        in_specs=[pl.BlockSpec((tm, d), lambda i: (i, 0)),
                  _const_spec(gpre.shape), _const_spec(win.shape), _const_spec(wout.shape),
                  _const_spec(gpost.shape)],
        out_specs=pl.BlockSpec((tm, d), lambda i: (i, 0)),
        out_shape=jax.ShapeDtypeStruct((t, d), F32),
        scratch_shapes=[pltpu.VMEM((tm, d_ff), BF16)],
        compiler_params=pltpu.CompilerParams(dimension_semantics=("arbitrary",),
                                             vmem_limit_bytes=VMEM_LIMIT),
        name="ffn",
    )(h, gpre, win, wout, gpost)


def kernel(x, mem, norm_mix_pre, w_in, w_pool_mix, pool_scale, w_pool_o, w_sb_o, norm_mem,
           w_mem_kv, w_x_o, w_out, norm_mix_post, norm_ffn_pre, w_ffn_in, w_ffn_out,
           norm_ffn_post):
    b, s, d = x.shape
    depth = w_in.shape[0]
    pool_w = w_pool_o.shape[1]
    sb_w = w_sb_o.shape[1]
    x_w = w_x_o.shape[1]
    sb_heads = sb_w // HEAD_DIM
    x_heads = x_w // HEAD_DIM
    split = pool_w + 3 * sb_w + x_w

    h = x.reshape(b * s, d)
    for l in range(depth):
        row = lambda v: v[l].reshape(1, -1)
        kv = _mem_kv(mem, row(norm_mem), w_mem_kv[l])
        pin, qv, kt, xq = _in_proj(h, row(norm_mix_pre), w_in[l], tm=TM_IN_PROJ, pool_w=pool_w,
                                   sb_w=sb_w, x_w=x_w)
        ysb = _sb_attn(qv.reshape(b, s, 2 * sb_w), kt, heads=sb_heads).reshape(b * s, sb_w)
        h = _mix(h, pin, ysb, xq, kv, row(norm_mix_pre), w_in[l], w_pool_mix[l], row(pool_scale),
                 w_pool_o[l], w_sb_o[l], w_x_o[l], w_out[l], row(norm_mix_post), tm=TM_MIX, seq=s,
                 x_heads=x_heads, gate_col0=split)
        h = _ffn(h, row(norm_ffn_pre), w_ffn_in[l], w_ffn_out[l], row(norm_ffn_post), tm=TM_FFN,
                 chunk=256)
    return h.reshape(b, s, d)
```

```python
import functools

import jax
import jax.numpy as jnp
from jax import lax
from jax.experimental import pallas as pl
from jax.experimental.pallas import tpu as pltpu

F32 = jnp.float32
BF16 = jnp.bfloat16

RMS_EPS = 1e-6
POOL_WINDOWS = (2, 4, 8, 16)
POOL_BLK = 128
LANES = 128
HEAD_DIM = 64
SB_BLOCK = 128
SB_SPAN = 3
SB_TILE = 512
SB_TOP_ROWS = 48
LOG2E = 1.4426950408889634
SB_EXIT_LOG2 = -150.0
SB_MASK_BIAS = -1e30
SB_LINEAR_Z = 64.0
VMEM_LIMIT = 56 * 1024 * 1024
VMEM_LIMIT_MIX = 59 * 1024 * 1024
TM_IN_PROJ = 1024
TM_MIX = 1024
TM_FFN = 512


def _rmsnorm(x, g):
    ms = jnp.mean(x * x, axis=-1, keepdims=True)
    return x * lax.rsqrt(ms + RMS_EPS) * g


def _dot(a, b):
    return jnp.dot(a, b, preferred_element_type=F32)


def _dot_nt(a, b):
    return lax.dot_general(a, b, (((1,), (1,)), ((), ())), preferred_element_type=F32)


def _const_spec(shape):
    nd = len(shape)
    return pl.BlockSpec(shape, lambda *_: (0,) * nd, pipeline_mode=pl.Buffered(1))


def _col_window_spec(rows, col0, ncols):
    return pl.BlockSpec((pl.Element(rows), pl.Element(ncols)), lambda *_: (0, col0),
                        pipeline_mode=pl.Buffered(1))


def _mem_kv_kernel(mem_ref, g_ref, w_ref, o_ref):
    n = _rmsnorm(mem_ref[0], g_ref[...])
    o_ref[0] = _dot(n.astype(BF16), w_ref[...]).astype(BF16)


def _mem_kv(mem, g, w):
    b, m, d = mem.shape
    n_out = w.shape[1]
    return pl.pallas_call(
        _mem_kv_kernel,
        grid=(b,),
        in_specs=[pl.BlockSpec((1, m, d), lambda i: (i, 0, 0)),
                  _const_spec((1, d)),
                  _const_spec((d, n_out))],
        out_specs=pl.BlockSpec((1, m, n_out), lambda i: (i, 0, 0)),
        out_shape=jax.ShapeDtypeStruct((b, m, n_out), BF16),
        compiler_params=pltpu.CompilerParams(dimension_semantics=("arbitrary",)),
        name="mem_kv",
    )(mem, g, w)


def _in_proj_kernel(x_ref, g_ref, w_ref, nb_ref, p_ref, qv_ref, kt_ref, xq_ref, wkt_ref, *, pool_w,
                    sb_w):
    scale = 1.0 / (HEAD_DIM ** 0.5)
    o0 = pool_w
    oq, ok, ov, ox = o0, o0 + sb_w, o0 + 2 * sb_w, o0 + 3 * sb_w

    @pl.when(pl.program_id(0) == 0)
    def _():
        wkt_ref[...] = w_ref[:, ok:ov].T

    nb = _rmsnorm(x_ref[...], g_ref[...]).astype(BF16)
    nb_ref[...] = nb
    p_ref[...] = _dot(nb, w_ref[:, :o0]).astype(BF16)
    qv_ref[:, :sb_w] = (_dot(nb, w_ref[:, oq:ok]) * (scale * LOG2E)).astype(BF16)
    qv_ref[:, sb_w:] = _dot(nb, w_ref[:, ov:ox]).astype(BF16)
    kt_ref[...] = _dot_nt(wkt_ref[...], nb).astype(BF16)
    xq_ref[...] = (_dot(nb, w_ref[:, ox:]) * scale).astype(BF16)


def _in_proj(x2, g, w, *, tm, pool_w, sb_w, x_w):
    t, d = x2.shape
    n_in = pool_w + 3 * sb_w + x_w
    kern = functools.partial(_in_proj_kernel, pool_w=pool_w, sb_w=sb_w)
    return pl.pallas_call(
        kern,
        grid=(t // tm,),
        in_specs=[pl.BlockSpec((tm, d), lambda i: (i, 0)),
                  _const_spec((1, d)),
                  _col_window_spec(d, 0, n_in)],
        out_specs=[pl.BlockSpec((tm, d), lambda i: (i, 0)),
                   pl.BlockSpec((tm, pool_w), lambda i: (i, 0)),
                   pl.BlockSpec((tm, 2 * sb_w), lambda i: (i, 0)),
                   pl.BlockSpec((sb_w, tm), lambda i: (0, i)),
                   pl.BlockSpec((tm, x_w), lambda i: (i, 0))],
        out_shape=[jax.ShapeDtypeStruct((t, d), BF16),
                   jax.ShapeDtypeStruct((t, pool_w), BF16),
                   jax.ShapeDtypeStruct((t, 2 * sb_w), BF16),
                   jax.ShapeDtypeStruct((sb_w, t), BF16),
                   jax.ShapeDtypeStruct((t, x_w), BF16)],
        scratch_shapes=[pltpu.VMEM((sb_w, d), w.dtype)],
        compiler_params=pltpu.CompilerParams(dimension_semantics=("arbitrary",),
                                             vmem_limit_bytes=VMEM_LIMIT),
        name="in_proj",
    )(x2, g, w)


def _sb_kernel(q_ref, kt_ref, v_ref, cm_ref, o_ref, qm_all_ref, z_all_ref, cat_all_ref, a_all_ref,
               acc_all_ref, carry_all_ref, *, heads):
    blk = SB_BLOCK
    npair = heads // 2
    n_qblk = q_ref.shape[1] // blk
    iq0 = pl.program_id(1) * n_qblk
    row = lax.broadcasted_iota(jnp.int32, (blk, blk), 0)
    col = lax.broadcasted_iota(jnp.int32, (blk, blk), 1)
    tri = col < row
    lo_half = col < HEAD_DIM

    def load_queries(kq):
        for p in range(npair):
            qp = q_ref[0, kq * blk:(kq + 1) * blk, p * LANES:(p + 1) * LANES]
            zero = jnp.zeros_like(qp)
            qm_all_ref[kq, p, :blk] = jnp.where(lo_half, qp, zero)
            qm_all_ref[kq, p, blk:] = jnp.where(lo_half, zero, qp)

    def visit(kq, entries, diag, fresh):
        qm_ref, z_ref, cat_ref, a_ref, acc_ref, carry_ref = (
            r.at[kq] for r in (qm_all_ref, z_all_ref, cat_all_ref, a_all_ref, acc_all_ref,
                               carry_all_ref))
        nb = len(entries)
        assert all((lo, hi) == (0, blk) for _, lo, hi, _ in entries[1:])
        blk0, lo0, hi0, bias0 = entries[0]
        n0 = hi0 - lo0
        partial = n0 != blk
        full = entries[1:] if partial else entries
        first_full = 1 if partial else 0

        def kv_rows(ref, p, blocks):
            parts = [ref[0, pl.ds(pl.multiple_of(bk * blk, blk), blk), p * LANES:(p + 1) * LANES]
                     for bk in blocks]
            return parts[0] if len(parts) == 1 else jnp.concatenate(parts, axis=0)

        def kt_cols(p, blocks):
            parts = [kt_ref[p * LANES:(p + 1) * LANES, pl.ds(pl.multiple_of(bk * blk, blk), blk)]
                     for bk in blocks]
            return parts[0] if len(parts) == 1 else jnp.concatenate(parts, axis=1)

        def biased(z, bias):
            return z if bias is None else z + bias

        for p in range(npair):
            if full:
                z = _dot(qm_ref[p], kt_cols(p, [e[0] for e in full]))
                for i, e in enumerate(full):
                    sl = first_full + i
                    z_ref[p, :, sl * blk:(sl + 1) * blk] = biased(z[:, i * blk:(i + 1) * blk], e[3])
            if partial:
                lhs = jnp.concatenate([qm_ref[p, lo0:hi0], qm_ref[p, blk + lo0:blk + hi0]], axis=0)
                z = biased(_dot(lhs, kt_cols(p, [blk0])), bias0)
                z_ref[p, lo0:hi0, :blk] = z[:n0]
                z_ref[p, blk + lo0:blk + hi0, :blk] = z[n0:]

        def rows_of(sl):
            return (lo0, hi0) if (partial and sl == 0) else (0, blk)

        for h in range(heads):
            p, hh = divmod(h, 2)
            for sl in range(nb):
                lo, hi = rows_of(sl)
                z = z_ref[p, hh * blk + lo:hh * blk + hi, sl * blk:(sl + 1) * blk]
                sp = jnp.where(z > SB_LINEAR_Z, z, jnp.log2(1.0 + jnp.exp2(z)))
                if diag and sl == nb - 1:
                    sp = jnp.where(tri, sp, 0.0)
                r0 = (h * SB_SPAN + sl + 1) * blk - (hi - lo)
                cat_ref[r0:r0 + hi - lo, :] = sp.astype(BF16)
        mx = None
        for h in range(heads):
            p, hh = divmod(h, 2)
            r_start = (h * SB_SPAN + 1) * blk - n0
            r_end = (h * SB_SPAN + nb) * blk
            r = _dot(cat_ref[r_start:r_end, :], cm_ref[...])
            c = None if fresh else carry_ref[h]
            for sl in reversed(range(nb)):
                lo, hi = rows_of(sl)
                off = 0 if sl == 0 else n0 + (sl - 1) * blk
                arg = (z_ref[p, hh * blk + lo:hh * blk + hi, sl * blk:(sl + 1) * blk]
                       + r[off:off + hi - lo, :blk])
                if c is not None:
                    arg = arg + c[lo:hi]
                a = jnp.exp2(arg)
                if diag and sl == nb - 1:
                    a = jnp.where(tri, a, 0.0)
                a_ref[h * blk + lo:h * blk + hi, sl * blk:(sl + 1) * blk] = a.astype(BF16)
                tot = r[off:off + hi - lo, blk:]
                if c is None:
                    c = tot
                elif (lo, hi) == (0, blk):
                    c = c + tot
                else:
                    pieces = ([c[:lo]] if lo else []) + [c[lo:hi] + tot] + ([c[hi:]] if hi < blk else [])
                    c = jnp.concatenate(pieces, axis=0)
            carry_ref[h] = c
            mx = c if mx is None else jnp.maximum(mx, c)
        for p in range(npair):
            if full:
                res = _dot(a_ref[2 * p * blk:2 * (p + 1) * blk, first_full * blk:nb * blk],
                           kv_rows(v_ref, p, [e[0] for e in full]))
                for hh in range(2):
                    if fresh:
                        acc_ref[2 * p + hh] = res[hh * blk:(hh + 1) * blk]
                    else:
                        acc_ref[2 * p + hh] += res[hh * blk:(hh + 1) * blk]
            if partial:
                lhs = jnp.concatenate([a_ref[2 * p * blk + lo0:2 * p * blk + hi0, :blk],
                                       a_ref[(2 * p + 1) * blk + lo0:(2 * p + 1) * blk + hi0, :blk]],
                                      axis=0)
                res = _dot(lhs, kv_rows(v_ref, p, [blk0]))
                acc_ref[2 * p, lo0:hi0] += res[:n0]
                acc_ref[2 * p + 1, lo0:hi0] += res[n0:]
        return jnp.max(mx), jnp.max(mx[SB_TOP_ROWS:])

    def first_visit(kq, iq):
        entries = []
        for back in range(SB_SPAN - 1, -1, -1):
            hi = SB_TOP_ROWS if back == SB_SPAN - 1 else blk
            bias = None if back == 0 else jnp.where(iq >= back, 0.0, SB_MASK_BIAS)
            entries.append((jnp.maximum(iq - back, 0), 0, hi, bias))
        return visit(kq, entries, True, True)

    def finish_oldest_block(kq, iq):
        back = SB_SPAN - 1
        bias = jnp.where(iq >= back, 0.0, SB_MASK_BIAS)
        return visit(kq, [(jnp.maximum(iq - back, 0), SB_TOP_ROWS, blk, bias)], False, False)[0]

    def continue_walk(kq, iq, m0):
        def cond(st):
            j, m = st
            return jnp.logical_and(j >= 0, m > SB_EXIT_LOG2)

        def body(st):
            j, _ = st
            return j - 1, visit(kq, [(j, 0, blk, None)], False, False)[0]

        lax.while_loop(cond, body, (iq - SB_SPAN, m0))

    carries = []
    for kq in range(n_qblk):
        load_queries(kq)
        carries.append(first_visit(kq, iq0 + kq))
    for kq in range(n_qblk):
        iq = iq0 + kq
        m_all, m_rest = carries[kq]
        m_all = lax.cond(m_rest > SB_EXIT_LOG2, lambda: finish_oldest_block(kq, iq), lambda: m_all)
        continue_walk(kq, iq, m_all)
        for p in range(npair):
            o = jnp.where(lo_half, acc_all_ref[kq, 2 * p], acc_all_ref[kq, 2 * p + 1])
            o_ref[0, kq * blk:(kq + 1) * blk, p * LANES:(p + 1) * LANES] = o.astype(BF16)


def _sb_attn(qv, kt, *, heads):
    b, s, w2 = qv.shape
    w = w2 // 2
    blk = SB_BLOCK
    nq = SB_TILE // blk
    cm = _suffix_sum_matrix()
    kern = functools.partial(_sb_kernel, heads=heads)
    return pl.pallas_call(
        kern,
        grid=(b, s // SB_TILE),
        in_specs=[pl.BlockSpec((1, SB_TILE, w), lambda bi, i: (bi, i, 0)),
                  pl.BlockSpec((w, s), lambda bi, i: (0, bi)),
                  pl.BlockSpec((1, s, w), lambda bi, i: (bi, 0, 1)),
                  _const_spec(cm.shape)],
        out_specs=pl.BlockSpec((1, SB_TILE, w), lambda bi, i: (bi, i, 0)),
        out_shape=jax.ShapeDtypeStruct((b, s, w), BF16),
        scratch_shapes=[pltpu.VMEM((nq, heads // 2, 2 * blk, LANES), BF16),
                        pltpu.VMEM((nq, heads // 2, 2 * blk, SB_SPAN * blk), F32),
                        pltpu.VMEM((nq, heads * SB_SPAN * blk, blk), BF16),
                        pltpu.VMEM((nq, heads * blk, SB_SPAN * blk), BF16),
                        pltpu.VMEM((nq, heads, blk, LANES), F32),
                        pltpu.VMEM((nq, heads, blk, LANES), F32)],
        compiler_params=pltpu.CompilerParams(dimension_semantics=("arbitrary", "arbitrary"),
                                             vmem_limit_bytes=VMEM_LIMIT),
        name="sb_attn",
    )(qv, kt, qv, cm)


def _suffix_sum_matrix():
    blk = SB_BLOCK
    j = jnp.arange(blk)[:, None]
    s = jnp.arange(2 * blk)[None, :]
    return jnp.where(jnp.logical_or(s >= blk, j >= s), -1.0, 0.0).astype(BF16)


def _mix_kernel(x_ref, nb_ref, pin_ref, halo_ref, ysb_ref, xq_ref, kv_ref, band_ref, wg_ref,
                wmix_ref, pscale_ref, wpo_ref, wsbo_ref, wxo_ref, wout_ref, gpost_ref, o_ref,
                ext_ref, *, seq, x_heads):
    t = pl.program_id(0)
    tm, d = x_ref.shape
    x = x_ref[...]
    nb = nb_ref[...]

    tok0 = (t * tm) % seq
    halo = halo_ref[...]
    ext_ref[:POOL_BLK, :] = jnp.where(tok0 == 0, jnp.zeros_like(halo), halo)
    ext_ref[POOL_BLK:, :] = pin_ref[...]
    pos1 = tok0 + 1 + lax.broadcasted_iota(jnp.int32, (tm, 1), 0)
    groups = []
    for g, w in enumerate(POOL_WINDOWS):
        cs = slice(g * LANES, (g + 1) * LANES)
        inv = 1.0 / jnp.minimum(pos1, w).astype(F32)
        chunks = []
        for c in range(tm // POOL_BLK):
            wsum = _dot(band_ref[g], ext_ref[c * POOL_BLK:(c + 2) * POOL_BLK, cs])
            cur = ext_ref[(c + 1) * POOL_BLK:(c + 2) * POOL_BLK, cs].astype(F32)
            chunks.append(wsum * inv[c * POOL_BLK:(c + 1) * POOL_BLK] - cur)
        mixed = jnp.concatenate(chunks, axis=0)
        groups.append(_dot(mixed.astype(BF16), wmix_ref[g]))
    ypool = jnp.concatenate(groups, axis=1) * pscale_ref[...]
    merged = jax.nn.sigmoid(_dot(nb, wg_ref[:, :d])) * _dot(ypool.astype(BF16), wpo_ref[...])

    merged += jax.nn.sigmoid(_dot(nb, wg_ref[:, d:2 * d])) * _dot(ysb_ref[...], wsbo_ref[...])

    m_len = kv_ref.shape[1]
    xw = x_heads * HEAD_DIM
    lane = lax.broadcasted_iota(jnp.int32, (tm, LANES), 1)
    lo_half = lane < HEAD_DIM
    ones = jnp.ones((m_len, LANES), BF16)
    pairs = []
    for p in range(x_heads // 2):
        cs = slice(p * LANES, (p + 1) * LANES)
        xq = xq_ref[:, cs]
        mk = kv_ref[0, :, cs]
        mv1 = jnp.concatenate([kv_ref[0, :, xw + p * LANES:xw + (p + 1) * LANES], ones], axis=1)
        outs = []
        for hh in range(2):
            keep = lo_half if hh == 0 else jnp.logical_not(lo_half)
            sc = _dot_nt(jnp.where(keep, xq, jnp.zeros_like(xq)), mk)
            e = jnp.exp(sc - jnp.max(sc, axis=-1, keepdims=True)).astype(BF16)
            r = _dot(e, mv1)
            outs.append(r[:, :LANES] / r[:, LANES:])
        pairs.append(jnp.where(lo_half, outs[0], outs[1]))
    yx = jnp.concatenate(pairs, axis=1).astype(BF16)
    merged += jax.nn.sigmoid(_dot(nb, wg_ref[:, 2 * d:])) * _dot(yx, wxo_ref[...])

    mo = _dot(merged.astype(BF16), wout_ref[...])
    o_ref[...] = x + _rmsnorm(mo, gpost_ref[...])


def _mix(x2, nb, pin, ysb, xq, kv, w_in, wmix, pscale, wpo, wsbo, wxo, wout, gpost, *, tm, seq,
         x_heads, gate_col0):
    t, d = x2.shape
    pw = pin.shape[1]
    hb = tm // POOL_BLK
    n_gate = w_in.shape[1] - gate_col0
    band = _pool_band_matrices()
    kern = functools.partial(_mix_kernel, seq=seq, x_heads=x_heads)
    return pl.pallas_call(
        kern,
        grid=(t // tm,),
        in_specs=[pl.BlockSpec((tm, d), lambda i: (i, 0)),
                  pl.BlockSpec((tm, d), lambda i: (i, 0)),
                  pl.BlockSpec((tm, pw), lambda i: (i, 0)),
                  pl.BlockSpec((POOL_BLK, pw), lambda i: (jnp.maximum(i * hb - 1, 0), 0)),
                  pl.BlockSpec((tm, ysb.shape[1]), lambda i: (i, 0)),
                  pl.BlockSpec((tm, xq.shape[1]), lambda i: (i, 0)),
                  pl.BlockSpec((1,) + kv.shape[1:], lambda i: ((i * tm) // seq, 0, 0)),
                  _const_spec(band.shape),
                  _col_window_spec(d, gate_col0, n_gate),
                  _const_spec(wmix.shape),
                  _const_spec(pscale.shape), _const_spec(wpo.shape), _const_spec(wsbo.shape),
                  _const_spec(wxo.shape), _const_spec(wout.shape), _const_spec(gpost.shape)],
        out_specs=pl.BlockSpec((tm, d), lambda i: (i, 0)),
        out_shape=jax.ShapeDtypeStruct((t, d), F32),
        scratch_shapes=[pltpu.VMEM((tm + POOL_BLK, pw), BF16)],
        compiler_params=pltpu.CompilerParams(dimension_semantics=("arbitrary",),
                                             vmem_limit_bytes=VMEM_LIMIT_MIX),
        name="mix",
    )(x2, nb, pin, pin, ysb, xq, kv, band, w_in, wmix, pscale, wpo, wsbo, wxo, wout, gpost)


def _pool_band_matrices():
    t = jnp.arange(POOL_BLK)[None, :, None] + POOL_BLK
    j = jnp.arange(2 * POOL_BLK)[None, None, :]
    w = jnp.asarray(POOL_WINDOWS)[:, None, None]
    return jnp.where(jnp.logical_and(j <= t, j > t - w), 1.0, 0.0).astype(BF16)


def _ffn_kernel(h_ref, gpre_ref, win_ref, wout_ref, gpost_ref, o_ref, a_ref, *, d_ff, chunk):
    h = h_ref[...]
    nb = _rmsnorm(h, gpre_ref[...]).astype(BF16)
    for c in range(d_ff // chunk):
        g = _dot(nb, win_ref[:, c * chunk:(c + 1) * chunk])
        u = _dot(nb, win_ref[:, d_ff + c * chunk:d_ff + (c + 1) * chunk])
        a_ref[:, c * chunk:(c + 1) * chunk] = (g * jax.nn.sigmoid(g) * u).astype(BF16)
    ff = _dot(a_ref[...], wout_ref[...])
    o_ref[...] = h + _rmsnorm(ff, gpost_ref[...])


def _ffn(h, gpre, win, wout, gpost, *, tm, chunk):
    t, d = h.shape
    d_ff = wout.shape[0]
    kern = functools.partial(_ffn_kernel, d_ff=d_ff, chunk=chunk)
    return pl.pallas_call(
        kern,
        grid=(t // tm,),
        in_specs=[pl.BlockSpec((tm, d), lambda i: (i, 0)),
                  _const_spec(gpre.shape), _const_spec(win.shape), _const_spec(wout.shape),
                  _const_spec(gpost.shape)],
        out_specs=pl.BlockSpec((tm, d), lambda i: (i, 0)),
        out_shape=jax.ShapeDtypeStruct((t, d), F32),
        scratch_shapes=[pltpu.VMEM((tm, d_ff), BF16)],
        compiler_params=pltpu.CompilerParams(dimension_semantics=("arbitrary",),
                                             vmem_limit_bytes=VMEM_LIMIT),
        name="ffn",
    )(h, gpre, win, wout, gpost)


def kernel(x, mem, norm_mix_pre, w_in, w_pool_mix, pool_scale, w_pool_o, w_sb_o, norm_mem,
           w_mem_kv, w_x_o, w_out, norm_mix_post, norm_ffn_pre, w_ffn_in, w_ffn_out,
           norm_ffn_post):
    b, s, d = x.shape
    depth = w_in.shape[0]
    pool_w = w_pool_o.shape[1]
    sb_w = w_sb_o.shape[1]
    x_w = w_x_o.shape[1]
    sb_heads = sb_w // HEAD_DIM
    x_heads = x_w // HEAD_DIM
    split = pool_w + 3 * sb_w + x_w

    h = x.reshape(b * s, d)
    for l in range(depth):
        row = lambda v: v[l].reshape(1, -1)
        kv = _mem_kv(mem, row(norm_mem), w_mem_kv[l])
        nb, pin, qv, kt, xq = _in_proj(h, row(norm_mix_pre), w_in[l], tm=TM_IN_PROJ, pool_w=pool_w,
                                       sb_w=sb_w, x_w=x_w)
        ysb = _sb_attn(qv.reshape(b, s, 2 * sb_w), kt, heads=sb_heads).reshape(b * s, sb_w)
        h = _mix(h, nb, pin, ysb, xq, kv, w_in[l], w_pool_mix[l], row(pool_scale),
                 w_pool_o[l], w_sb_o[l], w_x_o[l], w_out[l], row(norm_mix_post), tm=TM_MIX, seq=s,
                 x_heads=x_heads, gate_col0=split)
        h = _ffn(h, row(norm_ffn_pre), w_ffn_in[l], w_ffn_out[l], row(norm_ffn_post), tm=TM_FFN,
                 chunk=256)
    return h.reshape(b, s, d)
```

```python
import functools

import jax
import jax.numpy as jnp
from jax import lax
from jax.experimental import pallas as pl
from jax.experimental.pallas import tpu as pltpu

F32 = jnp.float32
BF16 = jnp.bfloat16

RMS_EPS = 1e-6
POOL_WINDOWS = (2, 4, 8, 16)
POOL_BLK = 128
LANES = 128
HEAD_DIM = 64
SB_BLOCK = 128
SB_SPAN = 3
SB_TILE = 512
SB_TOP_ROWS = 48
LOG2E = 1.4426950408889634
SB_EXIT_LOG2 = -150.0
SB_MASK_BIAS = -1e30
SB_LINEAR_Z = 64.0
VMEM_LIMIT = 56 * 1024 * 1024
TM_IN_PROJ = 1024
TM_MIX = 1024
TM_FFN = 512


def _rmsnorm(x, g):
    ms = jnp.mean(x * x, axis=-1, keepdims=True)
    return x * lax.rsqrt(ms + RMS_EPS) * g


def _dot(a, b):
    return jnp.dot(a, b, preferred_element_type=F32)


def _dot_nt(a, b):
    return lax.dot_general(a, b, (((1,), (1,)), ((), ())), preferred_element_type=F32)


def _const_spec(shape):
    nd = len(shape)
    return pl.BlockSpec(shape, lambda *_: (0,) * nd, pipeline_mode=pl.Buffered(1))


def _col_window_spec(rows, col0, ncols):
    return pl.BlockSpec((pl.Element(rows), pl.Element(ncols)), lambda *_: (0, col0),
                        pipeline_mode=pl.Buffered(1))


def _mem_kv_kernel(mem_ref, g_ref, w_ref, o_ref):
    n = _rmsnorm(mem_ref[0], g_ref[...])
    o_ref[0] = _dot(n.astype(BF16), w_ref[...]).astype(BF16)


def _mem_kv(mem, g, w):
    b, m, d = mem.shape
    n_out = w.shape[1]
    return pl.pallas_call(
        _mem_kv_kernel,
        grid=(b,),
        in_specs=[pl.BlockSpec((1, m, d), lambda i: (i, 0, 0)),
                  _const_spec((1, d)),
                  _const_spec((d, n_out))],
        out_specs=pl.BlockSpec((1, m, n_out), lambda i: (i, 0, 0)),
        out_shape=jax.ShapeDtypeStruct((b, m, n_out), BF16),
        compiler_params=pltpu.CompilerParams(dimension_semantics=("arbitrary",)),
        name="mem_kv",
    )(mem, g, w)


def _in_proj_kernel(x_ref, g_ref, w_ref, p_ref, qv_ref, kt_ref, xq_ref, wkt_ref, *, pool_w, sb_w):
    scale = 1.0 / (HEAD_DIM ** 0.5)
    o0 = pool_w
    oq, ok, ov, ox = o0, o0 + sb_w, o0 + 2 * sb_w, o0 + 3 * sb_w

    @pl.when(pl.program_id(0) == 0)
    def _():
        wkt_ref[...] = w_ref[:, ok:ov].T

    nb = _rmsnorm(x_ref[...], g_ref[...]).astype(BF16)
    p_ref[...] = _dot(nb, w_ref[:, :o0]).astype(BF16)
    qv_ref[:, :sb_w] = (_dot(nb, w_ref[:, oq:ok]) * (scale * LOG2E)).astype(BF16)
    qv_ref[:, sb_w:] = _dot(nb, w_ref[:, ov:ox]).astype(BF16)
    kt_ref[...] = _dot_nt(wkt_ref[...], nb).astype(BF16)
    xq_ref[...] = (_dot(nb, w_ref[:, ox:]) * scale).astype(BF16)


def _in_proj(x2, g, w, *, tm, pool_w, sb_w, x_w):
    t, d = x2.shape
    n_in = pool_w + 3 * sb_w + x_w
    kern = functools.partial(_in_proj_kernel, pool_w=pool_w, sb_w=sb_w)
    return pl.pallas_call(
        kern,
        grid=(t // tm,),
        in_specs=[pl.BlockSpec((tm, d), lambda i: (i, 0)),
                  _const_spec((1, d)),
                  _col_window_spec(d, 0, n_in)],
        out_specs=[pl.BlockSpec((tm, pool_w), lambda i: (i, 0)),
                   pl.BlockSpec((tm, 2 * sb_w), lambda i: (i, 0)),
                   pl.BlockSpec((sb_w, tm), lambda i: (0, i)),
                   pl.BlockSpec((tm, x_w), lambda i: (i, 0))],
        out_shape=[jax.ShapeDtypeStruct((t, pool_w), BF16),
                   jax.ShapeDtypeStruct((t, 2 * sb_w), BF16),
                   jax.ShapeDtypeStruct((sb_w, t), BF16),
                   jax.ShapeDtypeStruct((t, x_w), BF16)],
        scratch_shapes=[pltpu.VMEM((sb_w, d), w.dtype)],
        compiler_params=pltpu.CompilerParams(dimension_semantics=("arbitrary",),
                                             vmem_limit_bytes=VMEM_LIMIT),
        name="in_proj",
    )(x2, g, w)


def _sb_kernel(q_ref, kt_ref, v_ref, cm_ref, o_ref, qm_all_ref, z_all_ref, cat_all_ref, a_all_ref,
               acc_all_ref, carry_all_ref, *, heads):
    blk = SB_BLOCK
    npair = heads // 2
    n_qblk = q_ref.shape[1] // blk
    iq0 = pl.program_id(1) * n_qblk
    row = lax.broadcasted_iota(jnp.int32, (blk, blk), 0)
    col = lax.broadcasted_iota(jnp.int32, (blk, blk), 1)
    tri = col < row
    lo_half = col < HEAD_DIM

    def load_queries(kq):
        for p in range(npair):
            qp = q_ref[0, kq * blk:(kq + 1) * blk, p * LANES:(p + 1) * LANES]
            zero = jnp.zeros_like(qp)
            qm_all_ref[kq, p, :blk] = jnp.where(lo_half, qp, zero)
            qm_all_ref[kq, p, blk:] = jnp.where(lo_half, zero, qp)

    def visit(kq, entries, diag, fresh):
        qm_ref, z_ref, cat_ref, a_ref, acc_ref, carry_ref = (
            r.at[kq] for r in (qm_all_ref, z_all_ref, cat_all_ref, a_all_ref, acc_all_ref,
                               carry_all_ref))
        nb = len(entries)
        assert all((lo, hi) == (0, blk) for _, lo, hi, _ in entries[1:])
        blk0, lo0, hi0, bias0 = entries[0]
        n0 = hi0 - lo0
        partial = n0 != blk
        full = entries[1:] if partial else entries
        first_full = 1 if partial else 0

        def kv_rows(ref, p, blocks):
            parts = [ref[0, pl.ds(pl.multiple_of(bk * blk, blk), blk), p * LANES:(p + 1) * LANES]
                     for bk in blocks]
            return parts[0] if len(parts) == 1 else jnp.concatenate(parts, axis=0)

        def kt_cols(p, blocks):
            parts = [kt_ref[p * LANES:(p + 1) * LANES, pl.ds(pl.multiple_of(bk * blk, blk), blk)]
                     for bk in blocks]
            return parts[0] if len(parts) == 1 else jnp.concatenate(parts, axis=1)

        def biased(z, bias):
            return z if bias is None else z + bias

        for p in range(npair):
            if full:
                z = _dot(qm_ref[p], kt_cols(p, [e[0] for e in full]))
                for i, e in enumerate(full):
                    sl = first_full + i
                    z_ref[p, :, sl * blk:(sl + 1) * blk] = biased(z[:, i * blk:(i + 1) * blk], e[3])
            if partial:
                lhs = jnp.concatenate([qm_ref[p, lo0:hi0], qm_ref[p, blk + lo0:blk + hi0]], axis=0)
                z = biased(_dot(lhs, kt_cols(p, [blk0])), bias0)
                z_ref[p, lo0:hi0, :blk] = z[:n0]
                z_ref[p, blk + lo0:blk + hi0, :blk] = z[n0:]

        def rows_of(sl):
            return (lo0, hi0) if (partial and sl == 0) else (0, blk)

        for h in range(heads):
            p, hh = divmod(h, 2)
            for sl in range(nb):
                lo, hi = rows_of(sl)
                z = z_ref[p, hh * blk + lo:hh * blk + hi, sl * blk:(sl + 1) * blk]
                sp = jnp.where(z > SB_LINEAR_Z, z, jnp.log2(1.0 + jnp.exp2(z)))
                if diag and sl == nb - 1:
                    sp = jnp.where(tri, sp, 0.0)
                r0 = (h * SB_SPAN + sl + 1) * blk - (hi - lo)
                cat_ref[r0:r0 + hi - lo, :] = sp.astype(BF16)
        mx = None
        for h in range(heads):
            p, hh = divmod(h, 2)
            r_start = (h * SB_SPAN + 1) * blk - n0
            r_end = (h * SB_SPAN + nb) * blk
            r = _dot(cat_ref[r_start:r_end, :], cm_ref[...])
            c = None if fresh else carry_ref[h]
            for sl in reversed(range(nb)):
                lo, hi = rows_of(sl)
                off = 0 if sl == 0 else n0 + (sl - 1) * blk
                arg = (z_ref[p, hh * blk + lo:hh * blk + hi, sl * blk:(sl + 1) * blk]
                       + r[off:off + hi - lo, :blk])
                if c is not None:
                    arg = arg + c[lo:hi]
                a = jnp.exp2(arg)
                if diag and sl == nb - 1:
                    a = jnp.where(tri, a, 0.0)
                a_ref[h * blk + lo:h * blk + hi, sl * blk:(sl + 1) * blk] = a.astype(BF16)
                tot = r[off:off + hi - lo, blk:]
                if c is None:
                    c = tot
                elif (lo, hi) == (0, blk):
                    c = c + tot
                else:
                    pieces = ([c[:lo]] if lo else []) + [c[lo:hi] + tot] + ([c[hi:]] if hi < blk else [])
                    c = jnp.concatenate(pieces, axis=0)
            carry_ref[h] = c
            mx = c if mx is None else jnp.maximum(mx, c)
        for p in range(npair):
            if full:
                res = _dot(a_ref[2 * p * blk:2 * (p + 1) * blk, first_full * blk:nb * blk],
                           kv_rows(v_ref, p, [e[0] for e in full]))
                for hh in range(2):
                    if fresh:
                        acc_ref[2 * p + hh] = res[hh * blk:(hh + 1) * blk]
                    else:
                        acc_ref[2 * p + hh] += res[hh * blk:(hh + 1) * blk]
            if partial:
                lhs = jnp.concatenate([a_ref[2 * p * blk + lo0:2 * p * blk + hi0, :blk],
                                       a_ref[(2 * p + 1) * blk + lo0:(2 * p + 1) * blk + hi0, :blk]],
                                      axis=0)
                res = _dot(lhs, kv_rows(v_ref, p, [blk0]))
                acc_ref[2 * p, lo0:hi0] += res[:n0]
                acc_ref[2 * p + 1, lo0:hi0] += res[n0:]
        return jnp.max(mx), jnp.max(mx[SB_TOP_ROWS:])

    def first_visit(kq, iq):
        entries = []
        for back in range(SB_SPAN - 1, -1, -1):
            hi = SB_TOP_ROWS if back == SB_SPAN - 1 else blk
            bias = None if back == 0 else jnp.where(iq >= back, 0.0, SB_MASK_BIAS)
            entries.append((jnp.maximum(iq - back, 0), 0, hi, bias))
        return visit(kq, entries, True, True)

    def finish_oldest_block(kq, iq):
        back = SB_SPAN - 1
        bias = jnp.where(iq >= back, 0.0, SB_MASK_BIAS)
        return visit(kq, [(jnp.maximum(iq - back, 0), SB_TOP_ROWS, blk, bias)], False, False)[0]

    def continue_walk(kq, iq, m0):
        def cond(st):
            j, m = st
            return jnp.logical_and(j >= 0, m > SB_EXIT_LOG2)

        def body(st):
            j, _ = st
            return j - 1, visit(kq, [(j, 0, blk, None)], False, False)[0]

        lax.while_loop(cond, body, (iq - SB_SPAN, m0))

    carries = []
    for kq in range(n_qblk):
        load_queries(kq)
        carries.append(first_visit(kq, iq0 + kq))
    for kq in range(n_qblk):
        iq = iq0 + kq
        m_all, m_rest = carries[kq]
        m_all = lax.cond(m_rest > SB_EXIT_LOG2, lambda: finish_oldest_block(kq, iq), lambda: m_all)
        continue_walk(kq, iq, m_all)
        for p in range(npair):
            o = jnp.where(lo_half, acc_all_ref[kq, 2 * p], acc_all_ref[kq, 2 * p + 1])
            o_ref[0, kq * blk:(kq + 1) * blk, p * LANES:(p + 1) * LANES] = o.astype(BF16)


def _sb_attn(qv, kt, *, heads):
    b, s, w2 = qv.shape
    w = w2 // 2
    blk = SB_BLOCK
    nq = SB_TILE // blk
    cm = _suffix_sum_matrix()
    kern = functools.partial(_sb_kernel, heads=heads)
    return pl.pallas_call(
        kern,
        grid=(b, s // SB_TILE),
        in_specs=[pl.BlockSpec((1, SB_TILE, w), lambda bi, i: (bi, i, 0)),
                  pl.BlockSpec((w, s), lambda bi, i: (0, bi)),
                  pl.BlockSpec((1, s, w), lambda bi, i: (bi, 0, 1)),
                  _const_spec(cm.shape)],
        out_specs=pl.BlockSpec((1, SB_TILE, w), lambda bi, i: (bi, i, 0)),
        out_shape=jax.ShapeDtypeStruct((b, s, w), BF16),
        scratch_shapes=[pltpu.VMEM((nq, heads // 2, 2 * blk, LANES), BF16),
                        pltpu.VMEM((nq, heads // 2, 2 * blk, SB_SPAN * blk), F32),
                        pltpu.VMEM((nq, heads * SB_SPAN * blk, blk), BF16),
                        pltpu.VMEM((nq, heads * blk, SB_SPAN * blk), BF16),
                        pltpu.VMEM((nq, heads, blk, LANES), F32),
                        pltpu.VMEM((nq, heads, blk, LANES), F32)],
        compiler_params=pltpu.CompilerParams(dimension_semantics=("arbitrary", "arbitrary"),
                                             vmem_limit_bytes=VMEM_LIMIT),
        name="sb_attn",
    )(qv, kt, qv, cm)


def _suffix_sum_matrix():
    blk = SB_BLOCK
    j = jnp.arange(blk)[:, None]
    s = jnp.arange(2 * blk)[None, :]
    return jnp.where(jnp.logical_or(s >= blk, j >= s), -1.0, 0.0).astype(BF16)


def _mix_kernel(x_ref, pin_ref, halo_ref, ysb_ref, xq_ref, kv_ref, band_ref, gpre_ref, wg_ref,
                wmix_ref, pscale_ref, wpo_ref, wsbo_ref, wxo_ref, wout_ref, gpost_ref, o_ref,
                ext_ref, *, seq, x_heads):
    t = pl.program_id(0)
    tm, d = x_ref.shape
    x = x_ref[...]
    nb_half = _rmsnorm(x, gpre_ref[...] * 0.5).astype(BF16)

    def gated(col0, y_half):
        return y_half + y_half * jnp.tanh(_dot(nb_half, wg_ref[:, col0:col0 + d]))

    tok0 = (t * tm) % seq
    halo = halo_ref[...]
    ext_ref[:POOL_BLK, :] = jnp.where(tok0 == 0, jnp.zeros_like(halo), halo)
    ext_ref[POOL_BLK:, :] = pin_ref[...]
    pos1 = tok0 + 1 + lax.broadcasted_iota(jnp.int32, (tm, 1), 0)
    groups = []
    for g, w in enumerate(POOL_WINDOWS):
        cs = slice(g * LANES, (g + 1) * LANES)
        inv = 1.0 / jnp.minimum(pos1, w).astype(F32)
        chunks = []
        for c in range(tm // POOL_BLK):
            wsum = _dot(band_ref[g], ext_ref[c * POOL_BLK:(c + 2) * POOL_BLK, cs])
            cur = ext_ref[(c + 1) * POOL_BLK:(c + 2) * POOL_BLK, cs].astype(F32)
            chunks.append(wsum * inv[c * POOL_BLK:(c + 1) * POOL_BLK] - cur)
        mixed = jnp.concatenate(chunks, axis=0)
        groups.append(_dot(mixed.astype(BF16), wmix_ref[g]))
    ypool_half = jnp.concatenate(groups, axis=1) * (pscale_ref[...] * 0.5)
    merged = gated(0, _dot(ypool_half.astype(BF16), wpo_ref[...]))

    merged += gated(d, _dot(ysb_ref[...] * 0.5, wsbo_ref[...]))

    m_len = kv_ref.shape[1]
    xw = x_heads * HEAD_DIM
    lane = lax.broadcasted_iota(jnp.int32, (tm, LANES), 1)
    lo_half = lane < HEAD_DIM
    ones = jnp.full((m_len, LANES), 2.0, BF16)
    pairs = []
    for p in range(x_heads // 2):
        cs = slice(p * LANES, (p + 1) * LANES)
        xq = xq_ref[:, cs]
        mk = kv_ref[0, :, cs]
        mv1 = jnp.concatenate([kv_ref[0, :, xw + p * LANES:xw + (p + 1) * LANES], ones], axis=1)
        outs = []
        for hh in range(2):
            keep = lo_half if hh == 0 else jnp.logical_not(lo_half)
            sc = _dot_nt(jnp.where(keep, xq, jnp.zeros_like(xq)), mk)
            e = jnp.exp(sc - jnp.max(sc, axis=-1, keepdims=True)).astype(BF16)
            r = _dot(e, mv1)
            outs.append(r[:, :LANES] / r[:, LANES:])
        pairs.append(jnp.where(lo_half, outs[0], outs[1]))
    yx_half = jnp.concatenate(pairs, axis=1).astype(BF16)
    merged += gated(2 * d, _dot(yx_half, wxo_ref[...]))

    mo = _dot(merged.astype(BF16), wout_ref[...])
    o_ref[...] = x + _rmsnorm(mo, gpost_ref[...])


def _mix(x2, pin, ysb, xq, kv, gpre, w_in, wmix, pscale, wpo, wsbo, wxo, wout, gpost, *, tm, seq,
         x_heads, gate_col0):
    t, d = x2.shape
    pw = pin.shape[1]
    hb = tm // POOL_BLK
    n_gate = w_in.shape[1] - gate_col0
    band = _pool_band_matrices()
    kern = functools.partial(_mix_kernel, seq=seq, x_heads=x_heads)
    return pl.pallas_call(
        kern,
        grid=(t // tm,),
        in_specs=[pl.BlockSpec((tm, d), lambda i: (i, 0)),
                  pl.BlockSpec((tm, pw), lambda i: (i, 0)),
                  pl.BlockSpec((POOL_BLK, pw), lambda i: (jnp.maximum(i * hb - 1, 0), 0)),
                  pl.BlockSpec((tm, ysb.shape[1]), lambda i: (i, 0)),
                  pl.BlockSpec((tm, xq.shape[1]), lambda i: (i, 0)),
                  pl.BlockSpec((1,) + kv.shape[1:], lambda i: ((i * tm) // seq, 0, 0)),
                  _const_spec(band.shape),
                  _const_spec(gpre.shape), _col_window_spec(d, gate_col0, n_gate),
                  _const_spec(wmix.shape),
                  _const_spec(pscale.shape), _const_spec(wpo.shape), _const_spec(wsbo.shape),
                  _const_spec(wxo.shape), _const_spec(wout.shape), _const_spec(gpost.shape)],
        out_specs=pl.BlockSpec((tm, d), lambda i: (i, 0)),
        out_shape=jax.ShapeDtypeStruct((t, d), F32),
        scratch_shapes=[pltpu.VMEM((tm + POOL_BLK, pw), BF16)],
        compiler_params=pltpu.CompilerParams(dimension_semantics=("arbitrary",),
                                             vmem_limit_bytes=VMEM_LIMIT),
        name="mix",
    )(x2, pin, pin, ysb, xq, kv, band, gpre, w_in, wmix, pscale, wpo, wsbo, wxo, wout, gpost)


def _pool_band_matrices():
    t = jnp.arange(POOL_BLK)[None, :, None] + POOL_BLK
    j = jnp.arange(2 * POOL_BLK)[None, None, :]
    w = jnp.asarray(POOL_WINDOWS)[:, None, None]
    return jnp.where(jnp.logical_and(j <= t, j > t - w), 1.0, 0.0).astype(BF16)


def _ffn_kernel(h_ref, gpre_ref, win_ref, wout_ref, gpost_ref, o_ref, a_ref, *, d_ff, chunk):
    h = h_ref[...]
    nb = _rmsnorm(h, gpre_ref[...]).astype(BF16)
    nb_half = nb * 0.5
    for c in range(d_ff // chunk):
        gh = _dot(nb_half, win_ref[:, c * chunk:(c + 1) * chunk])
        u = _dot(nb, win_ref[:, d_ff + c * chunk:d_ff + (c + 1) * chunk])
        hgu = gh * u
        a_ref[:, c * chunk:(c + 1) * chunk] = (hgu + hgu * jnp.tanh(gh)).astype(BF16)
    ff = _dot(a_ref[...], wout_ref[...])
    o_ref[...] = h + _rmsnorm(ff, gpost_ref[...])


def _ffn(h, gpre, win, wout, gpost, *, tm, chunk):
    t, d = h.shape
    d_ff = wout.shape[0]
    kern = functools.partial(_ffn_kernel, d_ff=d_ff, chunk=chunk)
    return pl.pallas_call(
        kern,
        grid=(t // tm,),
        in_specs=[pl.BlockSpec((tm, d), lambda i: (i, 0)),
                  _const_spec(gpre.shape), _const_spec(win.shape), _const_spec(wout.shape),
                  _const_spec(gpost.shape)],
        out_specs=pl.BlockSpec((tm, d), lambda i: (i, 0)),
        out_shape=jax.ShapeDtypeStruct((t, d), F32),
        scratch_shapes=[pltpu.VMEM((tm, d_ff), BF16)],
        compiler_params=pltpu.CompilerParams(dimension_semantics=("arbitrary",),
                                             vmem_limit_bytes=VMEM_LIMIT),
        name="ffn",
    )(h, gpre, win, wout, gpost)


def kernel(x, mem, norm_mix_pre, w_in, w_pool_mix, pool_scale, w_pool_o, w_sb_o, norm_mem,
           w_mem_kv, w_x_o, w_out, norm_mix_post, norm_ffn_pre, w_ffn_in, w_ffn_out,
           norm_ffn_post):
    b, s, d = x.shape
    depth = w_in.shape[0]
    pool_w = w_pool_o.shape[1]
    sb_w = w_sb_o.shape[1]
    x_w = w_x_o.shape[1]
    sb_heads = sb_w // HEAD_DIM
    x_heads = x_w // HEAD_DIM
    split = pool_w + 3 * sb_w + x_w

    h = x.reshape(b * s, d)
    for l in range(depth):
        row = lambda v: v[l].reshape(1, -1)
        kv = _mem_kv(mem, row(norm_mem), w_mem_kv[l])
        pin, qv, kt, xq = _in_proj(h, row(norm_mix_pre), w_in[l], tm=TM_IN_PROJ, pool_w=pool_w,
                                   sb_w=sb_w, x_w=x_w)
        ysb = _sb_attn(qv.reshape(b, s, 2 * sb_w), kt, heads=sb_heads).reshape(b * s, sb_w)
        h = _mix(h, pin, ysb, xq, kv, row(norm_mix_pre), w_in[l], w_pool_mix[l], row(pool_scale),
                 w_pool_o[l], w_sb_o[l], w_x_o[l], w_out[l], row(norm_mix_post), tm=TM_MIX, seq=s,
                 x_heads=x_heads, gate_col0=split)
        h = _ffn(h, row(norm_ffn_pre), w_ffn_in[l], w_ffn_out[l], row(norm_ffn_post), tm=TM_FFN,
                 chunk=256)
    return h.reshape(b, s, d)
```

```python
import functools

import jax
import jax.numpy as jnp
from jax import lax
from jax.experimental import pallas as pl
from jax.experimental.pallas import tpu as pltpu

F32 = jnp.float32
BF16 = jnp.bfloat16

RMS_EPS = 1e-6
POOL_WINDOWS = (2, 4, 8, 16)
POOL_BLK = 128
LANES = 128
HEAD_DIM = 64
SB_BLOCK = 128
SB_SPAN = 3
SB_TILE = 1024
SB_GROUP = 4
SB_TOP_ROWS = 48
LOG2E = 1.4426950408889634
SB_EXIT_LOG2 = -150.0
SB_MASK_BIAS = -1e30
SB_LINEAR_Z = 64.0
VMEM_LIMIT = 56 * 1024 * 1024
TM_IN_PROJ = 1024
TM_MIX = 1024
TM_FFN = 512
FFN_TILES_PER_STEP = 2


def _rmsnorm(x, g):
    ms = jnp.mean(x * x, axis=-1, keepdims=True)
    return x * lax.rsqrt(ms + RMS_EPS) * g


def _dot(a, b):
    return jnp.dot(a, b, preferred_element_type=F32)


def _dot_nt(a, b):
    return lax.dot_general(a, b, (((1,), (1,)), ((), ())), preferred_element_type=F32)


def _const_spec(shape):
    nd = len(shape)
    return pl.BlockSpec(shape, lambda *_: (0,) * nd, pipeline_mode=pl.Buffered(1))


def _col_window_spec(rows, col0, ncols):
    return pl.BlockSpec((pl.Element(rows), pl.Element(ncols)), lambda *_: (0, col0),
                        pipeline_mode=pl.Buffered(1))


def _mem_kv_kernel(mem_ref, g_ref, w_ref, o_ref):
    n = _rmsnorm(mem_ref[0], g_ref[...])
    o_ref[0] = _dot(n.astype(BF16), w_ref[...]).astype(BF16)


def _mem_kv(mem, g, w):
    b, m, d = mem.shape
    n_out = w.shape[1]
    return pl.pallas_call(
        _mem_kv_kernel,
        grid=(b,),
        in_specs=[pl.BlockSpec((1, m, d), lambda i: (i, 0, 0)),
                  _const_spec((1, d)),
                  _const_spec((d, n_out))],
        out_specs=pl.BlockSpec((1, m, n_out), lambda i: (i, 0, 0)),
        out_shape=jax.ShapeDtypeStruct((b, m, n_out), BF16),
        compiler_params=pltpu.CompilerParams(dimension_semantics=("arbitrary",)),
        name="mem_kv",
    )(mem, g, w)


def _in_proj_kernel(x_ref, g_ref, w_ref, p_ref, qv_ref, kt_ref, xq_ref, wkt_ref, *, pool_w, sb_w):
    scale = 1.0 / (HEAD_DIM ** 0.5)
    o0 = pool_w
    oq, ok, ov, ox = o0, o0 + sb_w, o0 + 2 * sb_w, o0 + 3 * sb_w

    @pl.when(pl.program_id(0) == 0)
    def _():
        wkt_ref[...] = w_ref[:, ok:ov].T

    nb = _rmsnorm(x_ref[...], g_ref[...]).astype(BF16)
    p_ref[...] = _dot(nb, w_ref[:, :o0]).astype(BF16)
    qv_ref[:, :sb_w] = (_dot(nb, w_ref[:, oq:ok]) * (scale * LOG2E)).astype(BF16)
    qv_ref[:, sb_w:] = _dot(nb, w_ref[:, ov:ox]).astype(BF16)
    kt_ref[...] = _dot_nt(wkt_ref[...], nb).astype(BF16)
    xq_ref[...] = (_dot(nb, w_ref[:, ox:]) * scale).astype(BF16)


def _in_proj(x2, g, w, *, tm, pool_w, sb_w, x_w):
    t, d = x2.shape
    n_in = pool_w + 3 * sb_w + x_w
    kern = functools.partial(_in_proj_kernel, pool_w=pool_w, sb_w=sb_w)
    return pl.pallas_call(
        kern,
        grid=(t // tm,),
        in_specs=[pl.BlockSpec((tm, d), lambda i: (i, 0)),
                  _const_spec((1, d)),
                  _col_window_spec(d, 0, n_in)],
        out_specs=[pl.BlockSpec((tm, pool_w), lambda i: (i, 0)),
                   pl.BlockSpec((tm, 2 * sb_w), lambda i: (i, 0)),
                   pl.BlockSpec((sb_w, tm), lambda i: (0, i)),
                   pl.BlockSpec((tm, x_w), lambda i: (i, 0))],
        out_shape=[jax.ShapeDtypeStruct((t, pool_w), BF16),
                   jax.ShapeDtypeStruct((t, 2 * sb_w), BF16),
                   jax.ShapeDtypeStruct((sb_w, t), BF16),
                   jax.ShapeDtypeStruct((t, x_w), BF16)],
        scratch_shapes=[pltpu.VMEM((sb_w, d), w.dtype)],
        compiler_params=pltpu.CompilerParams(dimension_semantics=("arbitrary",),
                                             vmem_limit_bytes=VMEM_LIMIT),
        name="in_proj",
    )(x2, g, w)


def _sb_kernel(q_ref, kt_ref, v_ref, cm_ref, o_ref, qm_all_ref, z_all_ref, cat_all_ref, a_all_ref,
               acc_all_ref, carry_all_ref, *, heads):
    blk = SB_BLOCK
    npair = heads // 2
    n_qblk = q_ref.shape[1] // blk
    iq0 = pl.program_id(1) * n_qblk
    row = lax.broadcasted_iota(jnp.int32, (blk, blk), 0)
    col = lax.broadcasted_iota(jnp.int32, (blk, blk), 1)
    tri = col < row
    lo_half = col < HEAD_DIM

    def load_queries(kq):
        for p in range(npair):
            qp = q_ref[0, kq * blk:(kq + 1) * blk, p * LANES:(p + 1) * LANES]
            zero = jnp.zeros_like(qp)
            qm_all_ref[kq % SB_GROUP, p, :blk] = jnp.where(lo_half, qp, zero)
            qm_all_ref[kq % SB_GROUP, p, blk:] = jnp.where(lo_half, zero, qp)

    def visit(kq, entries, diag, fresh):
        qm_ref, z_ref, cat_ref, a_ref, acc_ref, carry_ref = (
            r.at[kq % SB_GROUP] for r in (qm_all_ref, z_all_ref, cat_all_ref, a_all_ref, acc_all_ref,
                               carry_all_ref))
        nb = len(entries)
        assert all((lo, hi) == (0, blk) for _, lo, hi, _ in entries[1:])
        blk0, lo0, hi0, bias0 = entries[0]
        n0 = hi0 - lo0
        partial = n0 != blk
        full = entries[1:] if partial else entries
        first_full = 1 if partial else 0

        def kv_rows(ref, p, blocks):
            parts = [ref[0, pl.ds(pl.multiple_of(bk * blk, blk), blk), p * LANES:(p + 1) * LANES]
                     for bk in blocks]
            return parts[0] if len(parts) == 1 else jnp.concatenate(parts, axis=0)

        def kt_cols(p, blocks):
            parts = [kt_ref[p * LANES:(p + 1) * LANES, pl.ds(pl.multiple_of(bk * blk, blk), blk)]
                     for bk in blocks]
            return parts[0] if len(parts) == 1 else jnp.concatenate(parts, axis=1)

        def biased(z, bias):
            return z if bias is None else z + bias

        for p in range(npair):
            if full:
                z = _dot(qm_ref[p], kt_cols(p, [e[0] for e in full]))
                for i, e in enumerate(full):
                    sl = first_full + i
                    z_ref[p, :, sl * blk:(sl + 1) * blk] = biased(z[:, i * blk:(i + 1) * blk], e[3])
            if partial:
                lhs = jnp.concatenate([qm_ref[p, lo0:hi0], qm_ref[p, blk + lo0:blk + hi0]], axis=0)
                z = biased(_dot(lhs, kt_cols(p, [blk0])), bias0)
                z_ref[p, lo0:hi0, :blk] = z[:n0]
                z_ref[p, blk + lo0:blk + hi0, :blk] = z[n0:]

        def rows_of(sl):
            return (lo0, hi0) if (partial and sl == 0) else (0, blk)

        for h in range(heads):
            p, hh = divmod(h, 2)
            for sl in range(nb):
                lo, hi = rows_of(sl)
                z = z_ref[p, hh * blk + lo:hh * blk + hi, sl * blk:(sl + 1) * blk]
                sp = jnp.where(z > SB_LINEAR_Z, z, jnp.log2(1.0 + jnp.exp2(z)))
                if diag and sl == nb - 1:
                    sp = jnp.where(tri, sp, 0.0)
                r0 = (h * SB_SPAN + sl + 1) * blk - (hi - lo)
                cat_ref[r0:r0 + hi - lo, :] = sp.astype(BF16)
        mx = None
        for h in range(heads):
            p, hh = divmod(h, 2)
            r_start = (h * SB_SPAN + 1) * blk - n0
            r_end = (h * SB_SPAN + nb) * blk
            r = _dot(cat_ref[r_start:r_end, :], cm_ref[...])
            c = None if fresh else carry_ref[h]
            for sl in reversed(range(nb)):
                lo, hi = rows_of(sl)
                off = 0 if sl == 0 else n0 + (sl - 1) * blk
                arg = (z_ref[p, hh * blk + lo:hh * blk + hi, sl * blk:(sl + 1) * blk]
                       + r[off:off + hi - lo, :blk])
                if c is not None:
                    arg = arg + c[lo:hi]
                a = jnp.exp2(arg)
                if diag and sl == nb - 1:
                    a = jnp.where(tri, a, 0.0)
                a_ref[h * blk + lo:h * blk + hi, sl * blk:(sl + 1) * blk] = a.astype(BF16)
                tot = r[off:off + hi - lo, blk:]
                if c is None:
                    c = tot
                elif (lo, hi) == (0, blk):
                    c = c + tot
                else:
                    pieces = ([c[:lo]] if lo else []) + [c[lo:hi] + tot] + ([c[hi:]] if hi < blk else [])
                    c = jnp.concatenate(pieces, axis=0)
            carry_ref[h] = c
            mx = c if mx is None else jnp.maximum(mx, c)
        for p in range(npair):
            if full:
                res = _dot(a_ref[2 * p * blk:2 * (p + 1) * blk, first_full * blk:nb * blk],
                           kv_rows(v_ref, p, [e[0] for e in full]))
                for hh in range(2):
                    if fresh:
                        acc_ref[2 * p + hh] = res[hh * blk:(hh + 1) * blk]
                    else:
                        acc_ref[2 * p + hh] += res[hh * blk:(hh + 1) * blk]
            if partial:
                lhs = jnp.concatenate([a_ref[2 * p * blk + lo0:2 * p * blk + hi0, :blk],
                                       a_ref[(2 * p + 1) * blk + lo0:(2 * p + 1) * blk + hi0, :blk]],
                                      axis=0)
                res = _dot(lhs, kv_rows(v_ref, p, [blk0]))
                acc_ref[2 * p, lo0:hi0] += res[:n0]
                acc_ref[2 * p + 1, lo0:hi0] += res[n0:]
        return jnp.max(mx), jnp.max(mx[SB_TOP_ROWS:])

    def first_visit(kq, iq):
        entries = []
        for back in range(SB_SPAN - 1, -1, -1):
            hi = SB_TOP_ROWS if back == SB_SPAN - 1 else blk
            bias = None if back == 0 else jnp.where(iq >= back, 0.0, SB_MASK_BIAS)
            entries.append((jnp.maximum(iq - back, 0), 0, hi, bias))
        return visit(kq, entries, True, True)

    def finish_oldest_block(kq, iq):
        back = SB_SPAN - 1
        bias = jnp.where(iq >= back, 0.0, SB_MASK_BIAS)
        return visit(kq, [(jnp.maximum(iq - back, 0), SB_TOP_ROWS, blk, bias)], False, False)[0]

    def continue_walk(kq, iq, m0):
        def cond(st):
            j, m = st
            return jnp.logical_and(j >= 0, m > SB_EXIT_LOG2)

        def body(st):
            j, _ = st
            return j - 1, visit(kq, [(j, 0, blk, None)], False, False)[0]

        lax.while_loop(cond, body, (iq - SB_SPAN, m0))

    for g0 in range(0, n_qblk, SB_GROUP):
        group = range(g0, g0 + SB_GROUP)
        carries = {}
        for kq in group:
            load_queries(kq)
            carries[kq] = first_visit(kq, iq0 + kq)
        for kq in group:
            iq = iq0 + kq
            m_all, m_rest = carries[kq]
            m_all = lax.cond(m_rest > SB_EXIT_LOG2, lambda: finish_oldest_block(kq, iq),
                             lambda: m_all)
            continue_walk(kq, iq, m_all)
            acc_ref = acc_all_ref.at[kq % SB_GROUP]
            for p in range(npair):
                o = jnp.where(lo_half, acc_ref[2 * p], acc_ref[2 * p + 1])
                o_ref[0, kq * blk:(kq + 1) * blk, p * LANES:(p + 1) * LANES] = o.astype(BF16)


def _sb_attn(qv, kt, *, heads):
    b, s, w2 = qv.shape
    w = w2 // 2
    blk = SB_BLOCK
    nq = SB_GROUP
    cm = _suffix_sum_matrix()
    kern = functools.partial(_sb_kernel, heads=heads)
    return pl.pallas_call(
        kern,
        grid=(b, s // SB_TILE),
        in_specs=[pl.BlockSpec((1, SB_TILE, w), lambda bi, i: (bi, i, 0)),
                  pl.BlockSpec((w, s), lambda bi, i: (0, bi)),
                  pl.BlockSpec((1, s, w), lambda bi, i: (bi, 0, 1)),
                  _const_spec(cm.shape)],
        out_specs=pl.BlockSpec((1, SB_TILE, w), lambda bi, i: (bi, i, 0)),
        out_shape=jax.ShapeDtypeStruct((b, s, w), BF16),
        scratch_shapes=[pltpu.VMEM((nq, heads // 2, 2 * blk, LANES), BF16),
                        pltpu.VMEM((nq, heads // 2, 2 * blk, SB_SPAN * blk), F32),
                        pltpu.VMEM((nq, heads * SB_SPAN * blk, blk), BF16),
                        pltpu.VMEM((nq, heads * blk, SB_SPAN * blk), BF16),
                        pltpu.VMEM((nq, heads, blk, LANES), F32),
                        pltpu.VMEM((nq, heads, blk, LANES), F32)],
        compiler_params=pltpu.CompilerParams(dimension_semantics=("arbitrary", "arbitrary"),
                                             vmem_limit_bytes=VMEM_LIMIT),
        name="sb_attn",
    )(qv, kt, qv, cm)


def _suffix_sum_matrix():
    blk = SB_BLOCK
    j = jnp.arange(blk)[:, None]
    s = jnp.arange(2 * blk)[None, :]
    return jnp.where(jnp.logical_or(s >= blk, j >= s), -1.0, 0.0).astype(BF16)


def _mix_kernel(x_ref, pin_ref, halo_ref, ysb_ref, xq_ref, kv_ref, band_ref, gpre_ref, wg_ref,
                wmix_ref, pscale_ref, wpo_ref, wsbo_ref, wxo_ref, wout_ref, gpost_ref, o_ref,
                ext_ref, *, seq, x_heads):
    t = pl.program_id(0)
    tm, d = x_ref.shape
    x = x_ref[...]
    nb_half = _rmsnorm(x, gpre_ref[...] * 0.5).astype(BF16)

    def gated(col0, y_half):
        return y_half + y_half * jnp.tanh(_dot(nb_half, wg_ref[:, col0:col0 + d]))

    tok0 = (t * tm) % seq
    halo = halo_ref[...]
    ext_ref[:POOL_BLK, :] = jnp.where(tok0 == 0, jnp.zeros_like(halo), halo)
    ext_ref[POOL_BLK:, :] = pin_ref[...]
    pos1 = tok0 + 1 + lax.broadcasted_iota(jnp.int32, (tm, 1), 0)
    groups = []
    for g, w in enumerate(POOL_WINDOWS):
        cs = slice(g * LANES, (g + 1) * LANES)
        inv = 1.0 / jnp.minimum(pos1, w).astype(F32)
        chunks = []
        for c in range(tm // POOL_BLK):
            wsum = _dot(band_ref[g], ext_ref[c * POOL_BLK:(c + 2) * POOL_BLK, cs])
            cur = ext_ref[(c + 1) * POOL_BLK:(c + 2) * POOL_BLK, cs].astype(F32)
            chunks.append(wsum * inv[c * POOL_BLK:(c + 1) * POOL_BLK] - cur)
        mixed = jnp.concatenate(chunks, axis=0)
        groups.append(_dot(mixed.astype(BF16), wmix_ref[g]))
    ypool_half = jnp.concatenate(groups, axis=1) * (pscale_ref[...] * 0.5)
    merged = gated(0, _dot(ypool_half.astype(BF16), wpo_ref[...]))

    merged += gated(d, _dot(ysb_ref[...] * 0.5, wsbo_ref[...]))

    m_len = kv_ref.shape[1]
    xw = x_heads * HEAD_DIM
    lane = lax.broadcasted_iota(jnp.int32, (tm, LANES), 1)
    lo_half = lane < HEAD_DIM
    ones = jnp.full((m_len, LANES), 2.0, BF16)
    pairs = []
    for p in range(x_heads // 2):
        cs = slice(p * LANES, (p + 1) * LANES)
        xq = xq_ref[:, cs]
        mk = kv_ref[0, :, cs]
        mv1 = jnp.concatenate([kv_ref[0, :, xw + p * LANES:xw + (p + 1) * LANES], ones], axis=1)
        outs = []
        for hh in range(2):
            keep = lo_half if hh == 0 else jnp.logical_not(lo_half)
            sc = _dot_nt(jnp.where(keep, xq, jnp.zeros_like(xq)), mk)
            e = jnp.exp(sc - jnp.max(sc, axis=-1, keepdims=True)).astype(BF16)
            r = _dot(e, mv1)
            outs.append(r[:, :LANES] / r[:, LANES:])
        pairs.append(jnp.where(lo_half, outs[0], outs[1]))
    yx_half = jnp.concatenate(pairs, axis=1).astype(BF16)
    merged += gated(2 * d, _dot(yx_half, wxo_ref[...]))

    mo = _dot(merged.astype(BF16), wout_ref[...])
    o_ref[...] = x + _rmsnorm(mo, gpost_ref[...])


def _mix(x2, pin, ysb, xq, kv, gpre, w_in, wmix, pscale, wpo, wsbo, wxo, wout, gpost, *, tm, seq,
         x_heads, gate_col0):
    t, d = x2.shape
    pw = pin.shape[1]
    hb = tm // POOL_BLK
    n_gate = w_in.shape[1] - gate_col0
    band = _pool_band_matrices()
    kern = functools.partial(_mix_kernel, seq=seq, x_heads=x_heads)
    return pl.pallas_call(
        kern,
        grid=(t // tm,),
        in_specs=[pl.BlockSpec((tm, d), lambda i: (i, 0)),
                  pl.BlockSpec((tm, pw), lambda i: (i, 0)),
                  pl.BlockSpec((POOL_BLK, pw), lambda i: (jnp.maximum(i * hb - 1, 0), 0)),
                  pl.BlockSpec((tm, ysb.shape[1]), lambda i: (i, 0)),
                  pl.BlockSpec((tm, xq.shape[1]), lambda i: (i, 0)),
                  pl.BlockSpec((1,) + kv.shape[1:], lambda i: ((i * tm) // seq, 0, 0)),
                  _const_spec(band.shape),
                  _const_spec(gpre.shape), _col_window_spec(d, gate_col0, n_gate),
                  _const_spec(wmix.shape),
                  _const_spec(pscale.shape), _const_spec(wpo.shape), _const_spec(wsbo.shape),
                  _const_spec(wxo.shape), _const_spec(wout.shape), _const_spec(gpost.shape)],
        out_specs=pl.BlockSpec((tm, d), lambda i: (i, 0)),
        out_shape=jax.ShapeDtypeStruct((t, d), F32),
        scratch_shapes=[pltpu.VMEM((tm + POOL_BLK, pw), BF16)],
        compiler_params=pltpu.CompilerParams(dimension_semantics=("arbitrary",),
                                             vmem_limit_bytes=VMEM_LIMIT),
        name="mix",
    )(x2, pin, pin, ysb, xq, kv, band, gpre, w_in, wmix, pscale, wpo, wsbo, wxo, wout, gpost)


def _pool_band_matrices():
    t = jnp.arange(POOL_BLK)[None, :, None] + POOL_BLK
    j = jnp.arange(2 * POOL_BLK)[None, None, :]
    w = jnp.asarray(POOL_WINDOWS)[:, None, None]
    return jnp.where(jnp.logical_and(j <= t, j > t - w), 1.0, 0.0).astype(BF16)


def _ffn_kernel(h_ref, gpre_ref, win_ref, wout_ref, gpost_ref, o_ref, a_ref, *, d_ff, chunk):
    tm = a_ref.shape[0]
    for r0 in range(0, h_ref.shape[0], tm):
        h = h_ref[r0:r0 + tm, :]
        nb = _rmsnorm(h, gpre_ref[...]).astype(BF16)
        nb_half = nb * 0.5
        for c in range(d_ff // chunk):
            gh = _dot(nb_half, win_ref[:, c * chunk:(c + 1) * chunk])
            u = _dot(nb, win_ref[:, d_ff + c * chunk:d_ff + (c + 1) * chunk])
            hgu = gh * u
            a_ref[:, c * chunk:(c + 1) * chunk] = (hgu + hgu * jnp.tanh(gh)).astype(BF16)
        ff = _dot(a_ref[...], wout_ref[...])
        o_ref[r0:r0 + tm, :] = h + _rmsnorm(ff, gpost_ref[...])


def _ffn(h, gpre, win, wout, gpost, *, tm, chunk):
    t, d = h.shape
    d_ff = wout.shape[0]
    rows = tm * FFN_TILES_PER_STEP
    kern = functools.partial(_ffn_kernel, d_ff=d_ff, chunk=chunk)
    return pl.pallas_call(
        kern,
        grid=(t // rows,),
        in_specs=[pl.BlockSpec((rows, d), lambda i: (i, 0)),
                  _const_spec(gpre.shape), _const_spec(win.shape), _const_spec(wout.shape),
                  _const_spec(gpost.shape)],
        out_specs=pl.BlockSpec((rows, d), lambda i: (i, 0)),
        out_shape=jax.ShapeDtypeStruct((t, d), F32),
        scratch_shapes=[pltpu.VMEM((tm, d_ff), BF16)],
        compiler_params=pltpu.CompilerParams(dimension_semantics=("arbitrary",),
                                             vmem_limit_bytes=VMEM_LIMIT),
        name="ffn",
    )(h, gpre, win, wout, gpost)


def kernel(x, mem, norm_mix_pre, w_in, w_pool_mix, pool_scale, w_pool_o, w_sb_o, norm_mem,
           w_mem_kv, w_x_o, w_out, norm_mix_post, norm_ffn_pre, w_ffn_in, w_ffn_out,
           norm_ffn_post):
    b, s, d = x.shape
    depth = w_in.shape[0]
    pool_w = w_pool_o.shape[1]
    sb_w = w_sb_o.shape[1]
    x_w = w_x_o.shape[1]
    sb_heads = sb_w // HEAD_DIM
    x_heads = x_w // HEAD_DIM
    split = pool_w + 3 * sb_w + x_w

    h = x.reshape(b * s, d)
    for l in range(depth):
        row = lambda v: v[l].reshape(1, -1)
        kv = _mem_kv(mem, row(norm_mem), w_mem_kv[l])
        pin, qv, kt, xq = _in_proj(h, row(norm_mix_pre), w_in[l], tm=TM_IN_PROJ, pool_w=pool_w,
                                   sb_w=sb_w, x_w=x_w)
        ysb = _sb_attn(qv.reshape(b, s, 2 * sb_w), kt, heads=sb_heads).reshape(b * s, sb_w)
        h = _mix(h, pin, ysb, xq, kv, row(norm_mix_pre), w_in[l], w_pool_mix[l], row(pool_scale),
                 w_pool_o[l], w_sb_o[l], w_x_o[l], w_out[l], row(norm_mix_post), tm=TM_MIX, seq=s,
                 x_heads=x_heads, gate_col0=split)
        h = _ffn(h, row(norm_ffn_pre), w_ffn_in[l], w_ffn_out[l], row(norm_ffn_post), tm=TM_FFN,
                 chunk=256)
    return h.reshape(b, s, d)
```

```python
import functools

import jax
import jax.numpy as jnp
from jax import lax
from jax.experimental import pallas as pl
from jax.experimental.pallas import tpu as pltpu

F32 = jnp.float32
BF16 = jnp.bfloat16

RMS_EPS = 1e-6
POOL_WINDOWS = (2, 4, 8, 16)
POOL_BLK = 128
LANES = 128
HEAD_DIM = 64
SB_BLOCK = 128
SB_SPAN = 3
SB_TILE = 1024
SB_GROUP = 4
SB_TOP_ROWS = 48
LOG2E = 1.4426950408889634
SB_EXIT_LOG2 = -150.0
SB_MASK_BIAS = -1e30
SB_LINEAR_Z = 64.0
VMEM_LIMIT = 56 * 1024 * 1024
TM_IN_PROJ = 1024
TM_MIX = 1024
TM_FFN = 512
FFN_TILES_PER_STEP = 2


def _rmsnorm(x, g):
    ms = jnp.mean(x * x, axis=-1, keepdims=True)
    return x * lax.rsqrt(ms + RMS_EPS) * g


def _dot(a, b):
    return jnp.dot(a, b, preferred_element_type=F32)


def _dot_nt(a, b):
    return lax.dot_general(a, b, (((1,), (1,)), ((), ())), preferred_element_type=F32)


def _const_spec(shape):
    nd = len(shape)
    return pl.BlockSpec(shape, lambda *_: (0,) * nd, pipeline_mode=pl.Buffered(1))


def _col_window_spec(rows, col0, ncols):
    return pl.BlockSpec((pl.Element(rows), pl.Element(ncols)), lambda *_: (0, col0),
                        pipeline_mode=pl.Buffered(1))


def _mem_kv_kernel(mem_ref, g_ref, w_ref, o_ref):
    n = _rmsnorm(mem_ref[0], g_ref[...])
    o_ref[0] = _dot(n.astype(BF16), w_ref[...]).astype(BF16)


def _mem_kv(mem, g, w):
    b, m, d = mem.shape
    n_out = w.shape[1]
    return pl.pallas_call(
        _mem_kv_kernel,
        grid=(b,),
        in_specs=[pl.BlockSpec((1, m, d), lambda i: (i, 0, 0)),
                  _const_spec((1, d)),
                  _const_spec((d, n_out))],
        out_specs=pl.BlockSpec((1, m, n_out), lambda i: (i, 0, 0)),
        out_shape=jax.ShapeDtypeStruct((b, m, n_out), BF16),
        compiler_params=pltpu.CompilerParams(dimension_semantics=("arbitrary",)),
        name="mem_kv",
    )(mem, g, w)


def _in_proj_kernel(x_ref, g_ref, w_ref, p_ref, qv_ref, kt_ref, xq_ref, wkt_ref, *, pool_w, sb_w):
    scale = 1.0 / (HEAD_DIM ** 0.5)
    o0 = pool_w
    oq, ok, ov, ox = o0, o0 + sb_w, o0 + 2 * sb_w, o0 + 3 * sb_w

    @pl.when(pl.program_id(0) == 0)
    def _():
        wkt_ref[...] = w_ref[:, ok:ov].T

    nb = _rmsnorm(x_ref[...], g_ref[...]).astype(BF16)
    p_ref[...] = _dot(nb, w_ref[:, :o0]).astype(BF16)
    qv_ref[:, :sb_w] = (_dot(nb, w_ref[:, oq:ok]) * (scale * LOG2E)).astype(BF16)
    qv_ref[:, sb_w:] = _dot(nb, w_ref[:, ov:ox]).astype(BF16)
    kt_ref[...] = _dot_nt(wkt_ref[...], nb).astype(BF16)
    xq_ref[...] = (_dot(nb, w_ref[:, ox:]) * scale).astype(BF16)


def _in_proj(x2, g, w, *, tm, pool_w, sb_w, x_w):
    t, d = x2.shape
    n_in = pool_w + 3 * sb_w + x_w
    kern = functools.partial(_in_proj_kernel, pool_w=pool_w, sb_w=sb_w)
    return pl.pallas_call(
        kern,
        grid=(t // tm,),
        in_specs=[pl.BlockSpec((tm, d), lambda i: (i, 0)),
                  _const_spec((1, d)),
                  _col_window_spec(d, 0, n_in)],
        out_specs=[pl.BlockSpec((tm, pool_w), lambda i: (i, 0)),
                   pl.BlockSpec((tm, 2 * sb_w), lambda i: (i, 0)),
                   pl.BlockSpec((sb_w, tm), lambda i: (0, i)),
                   pl.BlockSpec((tm, x_w), lambda i: (i, 0))],
        out_shape=[jax.ShapeDtypeStruct((t, pool_w), BF16),
                   jax.ShapeDtypeStruct((t, 2 * sb_w), BF16),
                   jax.ShapeDtypeStruct((sb_w, t), BF16),
                   jax.ShapeDtypeStruct((t, x_w), BF16)],
        scratch_shapes=[pltpu.VMEM((sb_w, d), w.dtype)],
        compiler_params=pltpu.CompilerParams(dimension_semantics=("arbitrary",),
                                             vmem_limit_bytes=VMEM_LIMIT),
        name="in_proj",
    )(x2, g, w)


def _sb_kernel(q_ref, kt_ref, v_ref, cm_ref, c2_ref, o_ref, qm_all_ref, z_all_ref, cat_all_ref,
               cat2_all_ref, a_all_ref, acc_all_ref, carry_all_ref, *, heads):
    blk = SB_BLOCK
    npair = heads // 2
    n_qblk = q_ref.shape[1] // blk
    iq0 = pl.program_id(1) * n_qblk
    row = lax.broadcasted_iota(jnp.int32, (blk, blk), 0)
    col = lax.broadcasted_iota(jnp.int32, (blk, blk), 1)
    tri = col < row
    lo_half = col < HEAD_DIM

    def load_queries(kq):
        for p in range(npair):
            qp = q_ref[0, kq * blk:(kq + 1) * blk, p * LANES:(p + 1) * LANES]
            zero = jnp.zeros_like(qp)
            qm_all_ref[kq % SB_GROUP, p, :blk] = jnp.where(lo_half, qp, zero)
            qm_all_ref[kq % SB_GROUP, p, blk:] = jnp.where(lo_half, zero, qp)

    def kv_rows(ref, p, blocks):
        parts = [ref[0, pl.ds(pl.multiple_of(bk * blk, blk), blk), p * LANES:(p + 1) * LANES]
                 for bk in blocks]
        return parts[0] if len(parts) == 1 else jnp.concatenate(parts, axis=0)

    def kt_cols(p, blocks):
        parts = [kt_ref[p * LANES:(p + 1) * LANES, pl.ds(pl.multiple_of(bk * blk, blk), blk)]
                 for bk in blocks]
        return parts[0] if len(parts) == 1 else jnp.concatenate(parts, axis=1)

    def softplus2(z):
        return jnp.where(z > SB_LINEAR_Z, z, jnp.log2(1.0 + jnp.exp2(z)))

    def scratch(kq):
        return tuple(r.at[kq % SB_GROUP] for r in (qm_all_ref, z_all_ref, cat_all_ref, cat2_all_ref,
                                                   a_all_ref, acc_all_ref, carry_all_ref))

    def first_visit(kq, iq):
        qm_ref, z_ref, cat_ref, cat2_ref, a_ref, acc_ref, carry_ref = scratch(kq)
        top = SB_TOP_ROWS
        far, near = jnp.maximum(iq - 2, 0), jnp.maximum(iq - 1, 0)
        bias_far = jnp.where(iq >= 2, 0.0, SB_MASK_BIAS)
        bias_near = jnp.where(iq >= 1, 0.0, SB_MASK_BIAS)

        for p in range(npair):
            z = _dot(qm_ref[p], kt_cols(p, [near, iq]))
            z_ref[p, :, blk:2 * blk] = z[:, :blk] + bias_near
            z_ref[p, :, 2 * blk:] = z[:, blk:]
            lhs = jnp.concatenate([qm_ref[p, :top], qm_ref[p, blk:blk + top]], axis=0)
            z = _dot(lhs, kt_cols(p, [far])) + bias_far
            z_ref[p, :top, :blk] = z[:top]
            z_ref[p, blk:blk + top, :blk] = z[top:]

        for h in range(heads):
            p, hh = divmod(h, 2)
            r0 = hh * blk
            cat2_ref[h, :, :blk] = softplus2(z_ref[p, r0:r0 + blk, blk:2 * blk]).astype(BF16)
            sp = softplus2(z_ref[p, r0:r0 + blk, 2 * blk:])
            cat2_ref[h, :, blk:] = jnp.where(tri, sp, 0.0).astype(BF16)
            cat_ref[h * blk:h * blk + top, :] = softplus2(z_ref[p, r0:r0 + top, :blk]).astype(BF16)
        mx = None
        for h in range(heads):
            p, hh = divmod(h, 2)
            r0 = hh * blk
            near2 = _dot(cat2_ref[h], c2_ref[:, :2 * blk])
            a = jnp.exp2(z_ref[p, r0:r0 + blk, blk:2 * blk] + near2[:, :blk])
            a_ref[h * blk:(h + 1) * blk, blk:2 * blk] = a.astype(BF16)
            a = jnp.exp2(z_ref[p, r0:r0 + blk, 2 * blk:] + near2[:, blk:])
            a_ref[h * blk:(h + 1) * blk, 2 * blk:] = jnp.where(tri, a, 0.0).astype(BF16)
            c_top = _dot(cat2_ref[h, :top], c2_ref[:, 2 * blk:])
            far2 = _dot(cat_ref[h * blk:h * blk + top, :], cm_ref[...])
            a = jnp.exp2(z_ref[p, r0:r0 + top, :blk] + far2[:, :blk] + c_top)
            a_ref[h * blk:h * blk + top, :blk] = a.astype(BF16)
            c = jnp.concatenate([c_top + far2[:, blk:], near2[top:, :blk]], axis=0)
            carry_ref[h] = c
            c = jnp.where(col == 0, c, -jnp.inf)
            mx = c if mx is None else jnp.maximum(mx, c)
        for p in range(npair):
            res = _dot(a_ref[2 * p * blk:2 * (p + 1) * blk, blk:], kv_rows(v_ref, p, [near, iq]))
            lhs = jnp.concatenate([a_ref[2 * p * blk:2 * p * blk + top, :blk],
                                   a_ref[(2 * p + 1) * blk:(2 * p + 1) * blk + top, :blk]], axis=0)
            res_top = _dot(lhs, kv_rows(v_ref, p, [far]))
            for hh in range(2):
                acc_ref[2 * p + hh] = jnp.concatenate(
                    [res[hh * blk:hh * blk + top] + res_top[hh * top:(hh + 1) * top],
                     res[hh * blk + top:(hh + 1) * blk]], axis=0)
        return jnp.max(mx), jnp.max(mx[top:])

    def visit(kq, entries):
        qm_ref, z_ref, cat_ref, _, a_ref, acc_ref, carry_ref = scratch(kq)
        nb = len(entries)
        assert nb * heads * blk <= cat_ref.shape[0]
        assert all((lo, hi) == (0, blk) for _, lo, hi, _ in entries[1:])
        blk0, lo0, hi0, bias0 = entries[0]
        n0 = hi0 - lo0
        partial = n0 != blk
        full = entries[1:] if partial else entries
        first_full = 1 if partial else 0

        def biased(z, bias):
            return z if bias is None else z + bias

        for p in range(npair):
            if full:
                z = _dot(qm_ref[p], kt_cols(p, [e[0] for e in full]))
                for i, e in enumerate(full):
                    sl = first_full + i
                    z_ref[p, :, sl * blk:(sl + 1) * blk] = biased(z[:, i * blk:(i + 1) * blk], e[3])
            if partial:
                lhs = jnp.concatenate([qm_ref[p, lo0:hi0], qm_ref[p, blk + lo0:blk + hi0]], axis=0)
                z = biased(_dot(lhs, kt_cols(p, [blk0])), bias0)
                z_ref[p, lo0:hi0, :blk] = z[:n0]
                z_ref[p, blk + lo0:blk + hi0, :blk] = z[n0:]

        def rows_of(sl):
            return (lo0, hi0) if (partial and sl == 0) else (0, blk)

        for h in range(heads):
            p, hh = divmod(h, 2)
            for sl in range(nb):
                lo, hi = rows_of(sl)
                sp = softplus2(z_ref[p, hh * blk + lo:hh * blk + hi, sl * blk:(sl + 1) * blk])
                r0 = (h * nb + sl + 1) * blk - (hi - lo)
                cat_ref[r0:r0 + hi - lo, :] = sp.astype(BF16)
        mx = None
        for h in range(heads):
            p, hh = divmod(h, 2)
            r_start = (h * nb + 1) * blk - n0
            r_end = (h * nb + nb) * blk
            r = _dot(cat_ref[r_start:r_end, :], cm_ref[...])
            c = jnp.broadcast_to(carry_ref[h][:, :1], (blk, blk))
            for sl in reversed(range(nb)):
                lo, hi = rows_of(sl)
                off = 0 if sl == 0 else n0 + (sl - 1) * blk
                arg = (z_ref[p, hh * blk + lo:hh * blk + hi, sl * blk:(sl + 1) * blk]
                       + r[off:off + hi - lo, :blk] + c[lo:hi])
                a_ref[h * blk + lo:h * blk + hi, sl * blk:(sl + 1) * blk] = jnp.exp2(arg).astype(BF16)
                tot = r[off:off + hi - lo, blk:]
                if (lo, hi) == (0, blk):
                    c = c + tot
                else:
                    pieces = ([c[:lo]] if lo else []) + [c[lo:hi] + tot] + ([c[hi:]] if hi < blk else [])
                    c = jnp.concatenate(pieces, axis=0)
            carry_ref[h] = c
            mx = c if mx is None else jnp.maximum(mx, c)
        for p in range(npair):
            if full:
                res = _dot(a_ref[2 * p * blk:2 * (p + 1) * blk, first_full * blk:nb * blk],
                           kv_rows(v_ref, p, [e[0] for e in full]))
                for hh in range(2):
                    acc_ref[2 * p + hh] += res[hh * blk:(hh + 1) * blk]
            if partial:
                lhs = jnp.concatenate([a_ref[2 * p * blk + lo0:2 * p * blk + hi0, :blk],
                                       a_ref[(2 * p + 1) * blk + lo0:(2 * p + 1) * blk + hi0, :blk]],
                                      axis=0)
                res = _dot(lhs, kv_rows(v_ref, p, [blk0]))
                acc_ref[2 * p, lo0:hi0] += res[:n0]
                acc_ref[2 * p + 1, lo0:hi0] += res[n0:]
        return jnp.max(mx)

    def finish_oldest_block(kq, iq):
        back = SB_SPAN - 1
        bias = jnp.where(iq >= back, 0.0, SB_MASK_BIAS)
        return visit(kq, [(jnp.maximum(iq - back, 0), SB_TOP_ROWS, blk, bias)])

    def continue_walk(kq, iq, m0):
        def cond(st):
            j, m = st
            return jnp.logical_and(j >= 0, m > SB_EXIT_LOG2)

        def body(st):
            j, _ = st
            return j - 1, visit(kq, [(j, 0, blk, None)])

        lax.while_loop(cond, body, (iq - SB_SPAN, m0))

    for g0 in range(0, n_qblk, SB_GROUP):
        group = range(g0, g0 + SB_GROUP)
        carries = {}
        for kq in group:
            load_queries(kq)
            carries[kq] = first_visit(kq, iq0 + kq)
        for kq in group:
            iq = iq0 + kq
            m_all, m_rest = carries[kq]
            m_all = lax.cond(m_rest > SB_EXIT_LOG2, lambda: finish_oldest_block(kq, iq),
                             lambda: m_all)
            continue_walk(kq, iq, m_all)
            acc_ref = acc_all_ref.at[kq % SB_GROUP]
            for p in range(npair):
                o = jnp.where(lo_half, acc_ref[2 * p], acc_ref[2 * p + 1])
                o_ref[0, kq * blk:(kq + 1) * blk, p * LANES:(p + 1) * LANES] = o.astype(BF16)


def _sb_attn(qv, kt, *, heads):
    b, s, w2 = qv.shape
    w = w2 // 2
    blk = SB_BLOCK
    nq = SB_GROUP
    cm = _suffix_sum_matrix()
    c2 = _span_suffix_matrix()
    kern = functools.partial(_sb_kernel, heads=heads)
    return pl.pallas_call(
        kern,
        grid=(b, s // SB_TILE),
        in_specs=[pl.BlockSpec((1, SB_TILE, w), lambda bi, i: (bi, i, 0)),
                  pl.BlockSpec((w, s), lambda bi, i: (0, bi)),
                  pl.BlockSpec((1, s, w), lambda bi, i: (bi, 0, 1)),
                  _const_spec(cm.shape), _const_spec(c2.shape)],
        out_specs=pl.BlockSpec((1, SB_TILE, w), lambda bi, i: (bi, i, 0)),
        out_shape=jax.ShapeDtypeStruct((b, s, w), BF16),
        scratch_shapes=[pltpu.VMEM((nq, heads // 2, 2 * blk, LANES), BF16),
                        pltpu.VMEM((nq, heads // 2, 2 * blk, SB_SPAN * blk), F32),
                        pltpu.VMEM((nq, heads * blk, blk), BF16),
                        pltpu.VMEM((nq, heads, blk, 2 * blk), BF16),
                        pltpu.VMEM((nq, heads * blk, SB_SPAN * blk), BF16),
                        pltpu.VMEM((nq, heads, blk, LANES), F32),
                        pltpu.VMEM((nq, heads, blk, LANES), F32)],
        compiler_params=pltpu.CompilerParams(dimension_semantics=("arbitrary", "arbitrary"),
                                             vmem_limit_bytes=VMEM_LIMIT),
        name="sb_attn",
    )(qv, kt, qv, cm, c2)


def _suffix_sum_matrix():
    blk = SB_BLOCK
    j = jnp.arange(blk)[:, None]
    s = jnp.arange(2 * blk)[None, :]
    return jnp.where(jnp.logical_or(s >= blk, j >= s), -1.0, 0.0).astype(BF16)


def _span_suffix_matrix():
    span = 2 * SB_BLOCK
    j = jnp.arange(span)[:, None]
    s = jnp.arange(span + SB_BLOCK)[None, :]
    return jnp.where(jnp.logical_or(s >= span, j >= s), -1.0, 0.0).astype(BF16)


def _mix_kernel(x_ref, pin_ref, halo_ref, ysb_ref, xq_ref, kv_ref, band_ref, gpre_ref, wg_ref,
                wmix_ref, pscale_ref, wpo_ref, wsbo_ref, wxo_ref, wout_ref, gpost_ref, o_ref,
                ext_ref, *, seq, x_heads):
    t = pl.program_id(0)
    tm, d = x_ref.shape
    x = x_ref[...]
    nb_half = _rmsnorm(x, gpre_ref[...] * 0.5).astype(BF16)

    def gated(col0, y_half):
        return y_half + y_half * jnp.tanh(_dot(nb_half, wg_ref[:, col0:col0 + d]))

    tok0 = (t * tm) % seq
    halo = halo_ref[...]
    ext_ref[:POOL_BLK, :] = jnp.where(tok0 == 0, jnp.zeros_like(halo), halo)
    ext_ref[POOL_BLK:, :] = pin_ref[...]
    pos1 = tok0 + 1 + lax.broadcasted_iota(jnp.int32, (tm, 1), 0)
    groups = []
    for g, w in enumerate(POOL_WINDOWS):
        cs = slice(g * LANES, (g + 1) * LANES)
        inv = 1.0 / jnp.minimum(pos1, w).astype(F32)
        chunks = []
        for c in range(tm // POOL_BLK):
            wsum = _dot(band_ref[g], ext_ref[c * POOL_BLK:(c + 2) * POOL_BLK, cs])
            cur = ext_ref[(c + 1) * POOL_BLK:(c + 2) * POOL_BLK, cs].astype(F32)
            chunks.append(wsum * inv[c * POOL_BLK:(c + 1) * POOL_BLK] - cur)
        mixed = jnp.concatenate(chunks, axis=0)
        groups.append(_dot(mixed.astype(BF16), wmix_ref[g]))
    ypool_half = jnp.concatenate(groups, axis=1) * (pscale_ref[...] * 0.5)
    merged = gated(0, _dot(ypool_half.astype(BF16), wpo_ref[...]))

    merged += gated(d, _dot(ysb_ref[...] * 0.5, wsbo_ref[...]))

    m_len = kv_ref.shape[1]
    xw = x_heads * HEAD_DIM
    lane = lax.broadcasted_iota(jnp.int32, (tm, LANES), 1)
    lo_half = lane < HEAD_DIM
    ones = jnp.full((m_len, LANES), 2.0, BF16)
    pairs = []
    for p in range(x_heads // 2):
        cs = slice(p * LANES, (p + 1) * LANES)
        xq = xq_ref[:, cs]
        mk = kv_ref[0, :, cs]
        mv1 = jnp.concatenate([kv_ref[0, :, xw + p * LANES:xw + (p + 1) * LANES], ones], axis=1)
        outs = []
        for hh in range(2):
            keep = lo_half if hh == 0 else jnp.logical_not(lo_half)
            sc = _dot_nt(jnp.where(keep, xq, jnp.zeros_like(xq)), mk)
            e = jnp.exp(sc - jnp.max(sc, axis=-1, keepdims=True)).astype(BF16)
            r = _dot(e, mv1)
            outs.append(r[:, :LANES] / r[:, LANES:])
        pairs.append(jnp.where(lo_half, outs[0], outs[1]))
    yx_half = jnp.concatenate(pairs, axis=1).astype(BF16)
    merged += gated(2 * d, _dot(yx_half, wxo_ref[...]))

    mo = _dot(merged.astype(BF16), wout_ref[...])
    o_ref[...] = x + _rmsnorm(mo, gpost_ref[...])


def _mix(x2, pin, ysb, xq, kv, gpre, w_in, wmix, pscale, wpo, wsbo, wxo, wout, gpost, *, tm, seq,
         x_heads, gate_col0):
    t, d = x2.shape
    pw = pin.shape[1]
    hb = tm // POOL_BLK
    n_gate = w_in.shape[1] - gate_col0
    band = _pool_band_matrices()
    kern = functools.partial(_mix_kernel, seq=seq, x_heads=x_heads)
    return pl.pallas_call(
        kern,
        grid=(t // tm,),
        in_specs=[pl.BlockSpec((tm, d), lambda i: (i, 0)),
                  pl.BlockSpec((tm, pw), lambda i: (i, 0)),
                  pl.BlockSpec((POOL_BLK, pw), lambda i: (jnp.maximum(i * hb - 1, 0), 0)),
                  pl.BlockSpec((tm, ysb.shape[1]), lambda i: (i, 0)),
                  pl.BlockSpec((tm, xq.shape[1]), lambda i: (i, 0)),
                  pl.BlockSpec((1,) + kv.shape[1:], lambda i: ((i * tm) // seq, 0, 0)),
                  _const_spec(band.shape),
                  _const_spec(gpre.shape), _col_window_spec(d, gate_col0, n_gate),
                  _const_spec(wmix.shape),
                  _const_spec(pscale.shape), _const_spec(wpo.shape), _const_spec(wsbo.shape),
                  _const_spec(wxo.shape), _const_spec(wout.shape), _const_spec(gpost.shape)],
        out_specs=pl.BlockSpec((tm, d), lambda i: (i, 0)),
        out_shape=jax.ShapeDtypeStruct((t, d), F32),
        scratch_shapes=[pltpu.VMEM((tm + POOL_BLK, pw), BF16)],
        compiler_params=pltpu.CompilerParams(dimension_semantics=("arbitrary",),
                                             vmem_limit_bytes=VMEM_LIMIT),
        name="mix",
    )(x2, pin, pin, ysb, xq, kv, band, gpre, w_in, wmix, pscale, wpo, wsbo, wxo, wout, gpost)


def _pool_band_matrices():
    t = jnp.arange(POOL_BLK)[None, :, None] + POOL_BLK
    j = jnp.arange(2 * POOL_BLK)[None, None, :]
    w = jnp.asarray(POOL_WINDOWS)[:, None, None]
    return jnp.where(jnp.logical_and(j <= t, j > t - w), 1.0, 0.0).astype(BF16)


def _ffn_kernel(h_ref, gpre_ref, win_ref, wout_ref, gpost_ref, o_ref, a_ref, *, d_ff, chunk):
    tm = a_ref.shape[0]
    for r0 in range(0, h_ref.shape[0], tm):
        h = h_ref[r0:r0 + tm, :]
        nb = _rmsnorm(h, gpre_ref[...]).astype(BF16)
        nb_half = nb * 0.5
        for c in range(d_ff // chunk):
            gh = _dot(nb_half, win_ref[:, c * chunk:(c + 1) * chunk])
            u = _dot(nb, win_ref[:, d_ff + c * chunk:d_ff + (c + 1) * chunk])
            hgu = gh * u
            a_ref[:, c * chunk:(c + 1) * chunk] = (hgu + hgu * jnp.tanh(gh)).astype(BF16)
        ff = _dot(a_ref[...], wout_ref[...])
        o_ref[r0:r0 + tm, :] = h + _rmsnorm(ff, gpost_ref[...])


def _ffn(h, gpre, win, wout, gpost, *, tm, chunk):
    t, d = h.shape
    d_ff = wout.shape[0]
    rows = tm * FFN_TILES_PER_STEP
    kern = functools.partial(_ffn_kernel, d_ff=d_ff, chunk=chunk)
    return pl.pallas_call(
        kern,
        grid=(t // rows,),
        in_specs=[pl.BlockSpec((rows, d), lambda i: (i, 0)),
                  _const_spec(gpre.shape), _const_spec(win.shape), _const_spec(wout.shape),
                  _const_spec(gpost.shape)],
        out_specs=pl.BlockSpec((rows, d), lambda i: (i, 0)),
        out_shape=jax.ShapeDtypeStruct((t, d), F32),
        scratch_shapes=[pltpu.VMEM((tm, d_ff), BF16)],
        compiler_params=pltpu.CompilerParams(dimension_semantics=("arbitrary",),
                                             vmem_limit_bytes=VMEM_LIMIT),
        name="ffn",
    )(h, gpre, win, wout, gpost)


def kernel(x, mem, norm_mix_pre, w_in, w_pool_mix, pool_scale, w_pool_o, w_sb_o, norm_mem,
           w_mem_kv, w_x_o, w_out, norm_mix_post, norm_ffn_pre, w_ffn_in, w_ffn_out,
           norm_ffn_post):
    b, s, d = x.shape
    depth = w_in.shape[0]
    pool_w = w_pool_o.shape[1]
    sb_w = w_sb_o.shape[1]
    x_w = w_x_o.shape[1]
    sb_heads = sb_w // HEAD_DIM
    x_heads = x_w // HEAD_DIM
    split = pool_w + 3 * sb_w + x_w

    h = x.reshape(b * s, d)
    for l in range(depth):
        row = lambda v: v[l].reshape(1, -1)
        kv = _mem_kv(mem, row(norm_mem), w_mem_kv[l])
        pin, qv, kt, xq = _in_proj(h, row(norm_mix_pre), w_in[l], tm=TM_IN_PROJ, pool_w=pool_w,
                                   sb_w=sb_w, x_w=x_w)
        ysb = _sb_attn(qv.reshape(b, s, 2 * sb_w), kt, heads=sb_heads).reshape(b * s, sb_w)
        h = _mix(h, pin, ysb, xq, kv, row(norm_mix_pre), w_in[l], w_pool_mix[l], row(pool_scale),
                 w_pool_o[l], w_sb_o[l], w_x_o[l], w_out[l], row(norm_mix_post), tm=TM_MIX, seq=s,
                 x_heads=x_heads, gate_col0=split)
        h = _ffn(h, row(norm_ffn_pre), w_ffn_in[l], w_ffn_out[l], row(norm_ffn_post), tm=TM_FFN,
                 chunk=256)
    return h.reshape(b, s, d)
```

```python
import functools

import jax
import jax.numpy as jnp
from jax import lax
from jax.experimental import pallas as pl
from jax.experimental.pallas import tpu as pltpu

F32 = jnp.float32
BF16 = jnp.bfloat16

RMS_EPS = 1e-6
POOL_WINDOWS = (2, 4, 8, 16)
POOL_BLK = 128
LANES = 128
HEAD_DIM = 64
SB_BLOCK = 128
SB_SPAN = 3
SB_TILE = 1024
SB_GROUP = 4
SB_TOP_ROWS = 48
LOG2E = 1.4426950408889634
SB_EXIT_LOG2 = -150.0
SB_MASK_BIAS = -1e30
SB_LINEAR_Z = 64.0
VMEM_LIMIT = 56 * 1024 * 1024
TM_IN_PROJ = 1024
TM_MIX = 1024
TM_FFN = 512
FFN_TILES_PER_STEP = 2


def _rmsnorm(x, g):
    ms = jnp.mean(x * x, axis=-1, keepdims=True)
    return x * lax.rsqrt(ms + RMS_EPS) * g


def _dot(a, b):
    return jnp.dot(a, b, preferred_element_type=F32)


def _dot_nt(a, b):
    return lax.dot_general(a, b, (((1,), (1,)), ((), ())), preferred_element_type=F32)


def _const_spec(shape):
    nd = len(shape)
    return pl.BlockSpec(shape, lambda *_: (0,) * nd, pipeline_mode=pl.Buffered(1))


def _col_window_spec(rows, col0, ncols):
    return pl.BlockSpec((pl.Element(rows), pl.Element(ncols)), lambda *_: (0, col0),
                        pipeline_mode=pl.Buffered(1))


def _mem_kv_kernel(mem_ref, g_ref, w_ref, o_ref):
    n = _rmsnorm(mem_ref[0], g_ref[...])
    o_ref[0] = _dot(n.astype(BF16), w_ref[...]).astype(BF16)


def _mem_kv(mem, g, w):
    b, m, d = mem.shape
    n_out = w.shape[1]
    return pl.pallas_call(
        _mem_kv_kernel,
        grid=(b,),
        in_specs=[pl.BlockSpec((1, m, d), lambda i: (i, 0, 0)),
                  _const_spec((1, d)),
                  _const_spec((d, n_out))],
        out_specs=pl.BlockSpec((1, m, n_out), lambda i: (i, 0, 0)),
        out_shape=jax.ShapeDtypeStruct((b, m, n_out), BF16),
        compiler_params=pltpu.CompilerParams(dimension_semantics=("arbitrary",)),
        name="mem_kv",
    )(mem, g, w)


def _in_proj_kernel(x_ref, g_ref, w_ref, p_ref, qv_ref, kt_ref, xq_ref, wkt_ref, wb_ref, *, pool_w,
                    sb_w):
    scale = 1.0 / (HEAD_DIM ** 0.5)
    o0 = pool_w
    oq, ok, ov, ox = o0, o0 + sb_w, o0 + 2 * sb_w, o0 + 3 * sb_w

    @pl.when(pl.program_id(0) == 0)
    def _():
        for c0 in range(0, w_ref.shape[1], sb_w):
            wb_ref[:, c0:c0 + sb_w] = w_ref[:, c0:c0 + sb_w].astype(BF16)
        wkt_ref[...] = w_ref[:, ok:ov].T.astype(BF16)

    nb = _rmsnorm(x_ref[...], g_ref[...]).astype(BF16)
    p_ref[...] = _dot(nb, wb_ref[:, :o0]).astype(BF16)
    qv_ref[:, :sb_w] = (_dot(nb, wb_ref[:, oq:ok]) * (scale * LOG2E)).astype(BF16)
    qv_ref[:, sb_w:] = _dot(nb, wb_ref[:, ov:ox]).astype(BF16)
    kt_ref[...] = _dot_nt(wkt_ref[...], nb).astype(BF16)
    xq_ref[...] = (_dot(nb, wb_ref[:, ox:]) * scale).astype(BF16)


def _in_proj(x2, g, w, *, tm, pool_w, sb_w, x_w):
    t, d = x2.shape
    n_in = pool_w + 3 * sb_w + x_w
    kern = functools.partial(_in_proj_kernel, pool_w=pool_w, sb_w=sb_w)
    return pl.pallas_call(
        kern,
        grid=(t // tm,),
        in_specs=[pl.BlockSpec((tm, d), lambda i: (i, 0)),
                  _const_spec((1, d)),
                  _col_window_spec(d, 0, n_in)],
        out_specs=[pl.BlockSpec((tm, pool_w), lambda i: (i, 0)),
                   pl.BlockSpec((tm, 2 * sb_w), lambda i: (i, 0)),
                   pl.BlockSpec((sb_w, tm), lambda i: (0, i)),
                   pl.BlockSpec((tm, x_w), lambda i: (i, 0))],
        out_shape=[jax.ShapeDtypeStruct((t, pool_w), BF16),
                   jax.ShapeDtypeStruct((t, 2 * sb_w), BF16),
                   jax.ShapeDtypeStruct((sb_w, t), BF16),
                   jax.ShapeDtypeStruct((t, x_w), BF16)],
        scratch_shapes=[pltpu.VMEM((sb_w, d), BF16),
                        pltpu.VMEM((d, n_in), BF16)],
        compiler_params=pltpu.CompilerParams(dimension_semantics=("arbitrary",),
                                             vmem_limit_bytes=VMEM_LIMIT),
        name="in_proj",
    )(x2, g, w)


def _sb_kernel(q_ref, kt_ref, v_ref, cm_ref, o_ref, qm_all_ref, z_all_ref, cat_all_ref, a_all_ref,
               acc_all_ref, carry_all_ref, *, heads):
    blk = SB_BLOCK
    npair = heads // 2
    n_qblk = q_ref.shape[1] // blk
    iq0 = pl.program_id(1) * n_qblk
    row = lax.broadcasted_iota(jnp.int32, (blk, blk), 0)
    col = lax.broadcasted_iota(jnp.int32, (blk, blk), 1)
    tri = col < row
    lo_half = col < HEAD_DIM

    def load_queries(kq):
        for p in range(npair):
            qp = q_ref[0, kq * blk:(kq + 1) * blk, p * LANES:(p + 1) * LANES]
            zero = jnp.zeros_like(qp)
            qm_all_ref[kq % SB_GROUP, p, :blk] = jnp.where(lo_half, qp, zero)
            qm_all_ref[kq % SB_GROUP, p, blk:] = jnp.where(lo_half, zero, qp)

    def visit(kq, entries, diag, fresh):
        qm_ref, z_ref, cat_ref, a_ref, acc_ref, carry_ref = (
            r.at[kq % SB_GROUP] for r in (qm_all_ref, z_all_ref, cat_all_ref, a_all_ref, acc_all_ref,
                               carry_all_ref))
        nb = len(entries)
        assert all((lo, hi) == (0, blk) for _, lo, hi, _ in entries[1:])
        blk0, lo0, hi0, bias0 = entries[0]
        n0 = hi0 - lo0
        partial = n0 != blk
        full = entries[1:] if partial else entries
        first_full = 1 if partial else 0

        def kv_rows(ref, p, blocks):
            parts = [ref[0, pl.ds(pl.multiple_of(bk * blk, blk), blk), p * LANES:(p + 1) * LANES]
                     for bk in blocks]
            return parts[0] if len(parts) == 1 else jnp.concatenate(parts, axis=0)

        def kt_cols(p, blocks):
            parts = [kt_ref[p * LANES:(p + 1) * LANES, pl.ds(pl.multiple_of(bk * blk, blk), blk)]
                     for bk in blocks]
            return parts[0] if len(parts) == 1 else jnp.concatenate(parts, axis=1)

        def biased(z, bias):
            return z if bias is None else z + bias

        for p in range(npair):
            if full:
                z = _dot(qm_ref[p], kt_cols(p, [e[0] for e in full]))
                for i, e in enumerate(full):
                    sl = first_full + i
                    z_ref[p, :, sl * blk:(sl + 1) * blk] = biased(z[:, i * blk:(i + 1) * blk], e[3])
            if partial:
                lhs = jnp.concatenate([qm_ref[p, lo0:hi0], qm_ref[p, blk + lo0:blk + hi0]], axis=0)
                z = biased(_dot(lhs, kt_cols(p, [blk0])), bias0)
                z_ref[p, lo0:hi0, :blk] = z[:n0]
                z_ref[p, blk + lo0:blk + hi0, :blk] = z[n0:]

        def rows_of(sl):
            return (lo0, hi0) if (partial and sl == 0) else (0, blk)

        for h in range(heads):
            p, hh = divmod(h, 2)
            for sl in range(nb):
                lo, hi = rows_of(sl)
                z = z_ref[p, hh * blk + lo:hh * blk + hi, sl * blk:(sl + 1) * blk]
                sp = jnp.where(z > SB_LINEAR_Z, z, jnp.log2(1.0 + jnp.exp2(z)))
                if diag and sl == nb - 1:
                    sp = jnp.where(tri, sp, 0.0)
                r0 = (h * SB_SPAN + sl + 1) * blk - (hi - lo)
                cat_ref[r0:r0 + hi - lo, :] = sp.astype(BF16)
        mx = None
        for h in range(heads):
            p, hh = divmod(h, 2)
            r_start = (h * SB_SPAN + 1) * blk - n0
            r_end = (h * SB_SPAN + nb) * blk
            r = _dot(cat_ref[r_start:r_end, :], cm_ref[...])
            c = None if fresh else carry_ref[h]
            for sl in reversed(range(nb)):
                lo, hi = rows_of(sl)
                off = 0 if sl == 0 else n0 + (sl - 1) * blk
                arg = (z_ref[p, hh * blk + lo:hh * blk + hi, sl * blk:(sl + 1) * blk]
                       + r[off:off + hi - lo, :blk])
                if c is not None:
                    arg = arg + c[lo:hi]
                a = jnp.exp2(arg)
                if diag and sl == nb - 1:
                    a = jnp.where(tri, a, 0.0)
                a_ref[h * blk + lo:h * blk + hi, sl * blk:(sl + 1) * blk] = a.astype(BF16)
                tot = r[off:off + hi - lo, blk:]
                if c is None:
                    c = tot
                elif (lo, hi) == (0, blk):
                    c = c + tot
                else:
                    pieces = ([c[:lo]] if lo else []) + [c[lo:hi] + tot] + ([c[hi:]] if hi < blk else [])
                    c = jnp.concatenate(pieces, axis=0)
            carry_ref[h] = c
            mx = c if mx is None else jnp.maximum(mx, c)
        for p in range(npair):
            if full:
                res = _dot(a_ref[2 * p * blk:2 * (p + 1) * blk, first_full * blk:nb * blk],
                           kv_rows(v_ref, p, [e[0] for e in full]))
                for hh in range(2):
                    if fresh:
                        acc_ref[2 * p + hh] = res[hh * blk:(hh + 1) * blk]
                    else:
                        acc_ref[2 * p + hh] += res[hh * blk:(hh + 1) * blk]
            if partial:
                lhs = jnp.concatenate([a_ref[2 * p * blk + lo0:2 * p * blk + hi0, :blk],
                                       a_ref[(2 * p + 1) * blk + lo0:(2 * p + 1) * blk + hi0, :blk]],
                                      axis=0)
                res = _dot(lhs, kv_rows(v_ref, p, [blk0]))
                acc_ref[2 * p, lo0:hi0] += res[:n0]
                acc_ref[2 * p + 1, lo0:hi0] += res[n0:]
        return jnp.max(mx), jnp.max(mx[SB_TOP_ROWS:])

    def first_visit(kq, iq):
        entries = []
        for back in range(SB_SPAN - 1, -1, -1):
            hi = SB_TOP_ROWS if back == SB_SPAN - 1 else blk
            bias = None if back == 0 else jnp.where(iq >= back, 0.0, SB_MASK_BIAS)
            entries.append((jnp.maximum(iq - back, 0), 0, hi, bias))
        return visit(kq, entries, True, True)

    def finish_oldest_block(kq, iq):
        back = SB_SPAN - 1
        bias = jnp.where(iq >= back, 0.0, SB_MASK_BIAS)
        return visit(kq, [(jnp.maximum(iq - back, 0), SB_TOP_ROWS, blk, bias)], False, False)[0]

    def continue_walk(kq, iq, m0):
        def cond(st):
            j, m = st
            return jnp.logical_and(j >= 0, m > SB_EXIT_LOG2)

        def body(st):
            j, _ = st
            return j - 1, visit(kq, [(j, 0, blk, None)], False, False)[0]

        lax.while_loop(cond, body, (iq - SB_SPAN, m0))

    for g0 in range(0, n_qblk, SB_GROUP):
        group = range(g0, g0 + SB_GROUP)
        carries = {}
        for kq in group:
            load_queries(kq)
            carries[kq] = first_visit(kq, iq0 + kq)
        for kq in group:
            iq = iq0 + kq
            m_all, m_rest = carries[kq]
            m_all = lax.cond(m_rest > SB_EXIT_LOG2, lambda: finish_oldest_block(kq, iq),
                             lambda: m_all)
            continue_walk(kq, iq, m_all)
            acc_ref = acc_all_ref.at[kq % SB_GROUP]
            for p in range(npair):
                o = jnp.where(lo_half, acc_ref[2 * p], acc_ref[2 * p + 1])
                o_ref[0, kq * blk:(kq + 1) * blk, p * LANES:(p + 1) * LANES] = o.astype(BF16)


def _sb_attn(qv, kt, *, heads):
    b, s, w2 = qv.shape
    w = w2 // 2
    blk = SB_BLOCK
    nq = SB_GROUP
    cm = _suffix_sum_matrix()
    kern = functools.partial(_sb_kernel, heads=heads)
    return pl.pallas_call(
        kern,
        grid=(b, s // SB_TILE),
        in_specs=[pl.BlockSpec((1, SB_TILE, w), lambda bi, i: (bi, i, 0)),
                  pl.BlockSpec((w, s), lambda bi, i: (0, bi)),
                  pl.BlockSpec((1, s, w), lambda bi, i: (bi, 0, 1)),
                  _const_spec(cm.shape)],
        out_specs=pl.BlockSpec((1, SB_TILE, w), lambda bi, i: (bi, i, 0)),
        out_shape=jax.ShapeDtypeStruct((b, s, w), BF16),
        scratch_shapes=[pltpu.VMEM((nq, heads // 2, 2 * blk, LANES), BF16),
                        pltpu.VMEM((nq, heads // 2, 2 * blk, SB_SPAN * blk), F32),
                        pltpu.VMEM((nq, heads * SB_SPAN * blk, blk), BF16),
                        pltpu.VMEM((nq, heads * blk, SB_SPAN * blk), BF16),
                        pltpu.VMEM((nq, heads, blk, LANES), F32),
                        pltpu.VMEM((nq, heads, blk, LANES), F32)],
        compiler_params=pltpu.CompilerParams(dimension_semantics=("arbitrary", "arbitrary"),
                                             vmem_limit_bytes=VMEM_LIMIT),
        name="sb_attn",
    )(qv, kt, qv, cm)


def _suffix_sum_matrix():
    blk = SB_BLOCK
    j = jnp.arange(blk)[:, None]
    s = jnp.arange(2 * blk)[None, :]
    return jnp.where(jnp.logical_or(s >= blk, j >= s), -1.0, 0.0).astype(BF16)


def _mix_kernel(x_ref, pin_ref, halo_ref, ysb_ref, xq_ref, kv_ref, band_ref, gpre_ref, wg_ref,
                wmix_ref, pscale_ref, wpo_ref, wsbo_ref, wxo_ref, wout_ref, gpost_ref, o_ref,
                ext_ref, *, seq, x_heads):
    t = pl.program_id(0)
    tm, d = x_ref.shape
    x = x_ref[...]
    nb_half = _rmsnorm(x, gpre_ref[...] * 0.5).astype(BF16)

    def gated(col0, y_half):
        return y_half + y_half * jnp.tanh(_dot(nb_half, wg_ref[:, col0:col0 + d]))

    tok0 = (t * tm) % seq
    halo = halo_ref[...]
    ext_ref[:POOL_BLK, :] = jnp.where(tok0 == 0, jnp.zeros_like(halo), halo)
    ext_ref[POOL_BLK:, :] = pin_ref[...]
    pos1 = tok0 + 1 + lax.broadcasted_iota(jnp.int32, (tm, 1), 0)
    groups = []
    for g, w in enumerate(POOL_WINDOWS):
        cs = slice(g * LANES, (g + 1) * LANES)
        inv = 1.0 / jnp.minimum(pos1, w).astype(F32)
        chunks = []
        for c in range(tm // POOL_BLK):
            wsum = _dot(band_ref[g], ext_ref[c * POOL_BLK:(c + 2) * POOL_BLK, cs])
            cur = ext_ref[(c + 1) * POOL_BLK:(c + 2) * POOL_BLK, cs].astype(F32)
            chunks.append(wsum * inv[c * POOL_BLK:(c + 1) * POOL_BLK] - cur)
        mixed = jnp.concatenate(chunks, axis=0)
        groups.append(_dot(mixed.astype(BF16), wmix_ref[g]))
    ypool_half = jnp.concatenate(groups, axis=1) * (pscale_ref[...] * 0.5)
    merged = gated(0, _dot(ypool_half.astype(BF16), wpo_ref[...]))

    merged += gated(d, _dot(ysb_ref[...] * 0.5, wsbo_ref[...]))

    m_len = kv_ref.shape[1]
    xw = x_heads * HEAD_DIM
    lane = lax.broadcasted_iota(jnp.int32, (tm, LANES), 1)
    lo_half = lane < HEAD_DIM
    ones = jnp.full((m_len, LANES), 2.0, BF16)
    pairs = []
    for p in range(x_heads // 2):
        cs = slice(p * LANES, (p + 1) * LANES)
        xq = xq_ref[:, cs]
        mk = kv_ref[0, :, cs]
        mv1 = jnp.concatenate([kv_ref[0, :, xw + p * LANES:xw + (p + 1) * LANES], ones], axis=1)
        outs = []
        for hh in range(2):
            keep = lo_half if hh == 0 else jnp.logical_not(lo_half)
            sc = _dot_nt(jnp.where(keep, xq, jnp.zeros_like(xq)), mk)
            e = jnp.exp(sc - jnp.max(sc, axis=-1, keepdims=True)).astype(BF16)
            r = _dot(e, mv1)
            outs.append(r[:, :LANES] / r[:, LANES:])
        pairs.append(jnp.where(lo_half, outs[0], outs[1]))
    yx_half = jnp.concatenate(pairs, axis=1).astype(BF16)
    merged += gated(2 * d, _dot(yx_half, wxo_ref[...]))

    mo = _dot(merged.astype(BF16), wout_ref[...])
    o_ref[...] = x + _rmsnorm(mo, gpost_ref[...])


def _mix(x2, pin, ysb, xq, kv, gpre, w_in, wmix, pscale, wpo, wsbo, wxo, wout, gpost, *, tm, seq,
         x_heads, gate_col0):
    t, d = x2.shape
    pw = pin.shape[1]
    hb = tm // POOL_BLK
    n_gate = w_in.shape[1] - gate_col0
    band = _pool_band_matrices()
    kern = functools.partial(_mix_kernel, seq=seq, x_heads=x_heads)
    return pl.pallas_call(
        kern,
        grid=(t // tm,),
        in_specs=[pl.BlockSpec((tm, d), lambda i: (i, 0)),
                  pl.BlockSpec((tm, pw), lambda i: (i, 0)),
                  pl.BlockSpec((POOL_BLK, pw), lambda i: (jnp.maximum(i * hb - 1, 0), 0)),
                  pl.BlockSpec((tm, ysb.shape[1]), lambda i: (i, 0)),
                  pl.BlockSpec((tm, xq.shape[1]), lambda i: (i, 0)),
                  pl.BlockSpec((1,) + kv.shape[1:], lambda i: ((i * tm) // seq, 0, 0)),
                  _const_spec(band.shape),
                  _const_spec(gpre.shape), _col_window_spec(d, gate_col0, n_gate),
                  _const_spec(wmix.shape),
                  _const_spec(pscale.shape), _const_spec(wpo.shape), _const_spec(wsbo.shape),
                  _const_spec(wxo.shape), _const_spec(wout.shape), _const_spec(gpost.shape)],
        out_specs=pl.BlockSpec((tm, d), lambda i: (i, 0)),
        out_shape=jax.ShapeDtypeStruct((t, d), F32),
        scratch_shapes=[pltpu.VMEM((tm + POOL_BLK, pw), BF16)],
        compiler_params=pltpu.CompilerParams(dimension_semantics=("arbitrary",),
                                             vmem_limit_bytes=VMEM_LIMIT),
        name="mix",
    )(x2, pin, pin, ysb, xq, kv, band, gpre, w_in, wmix, pscale, wpo, wsbo, wxo, wout, gpost)


def _pool_band_matrices():
    t = jnp.arange(POOL_BLK)[None, :, None] + POOL_BLK
    j = jnp.arange(2 * POOL_BLK)[None, None, :]
    w = jnp.asarray(POOL_WINDOWS)[:, None, None]
    return jnp.where(jnp.logical_and(j <= t, j > t - w), 1.0, 0.0).astype(BF16)


def _ffn_kernel(h_ref, gpre_ref, win_ref, wout_ref, gpost_ref, o_ref, a_ref, *, d_ff, chunk):
    tm = a_ref.shape[0]
    for r0 in range(0, h_ref.shape[0], tm):
        h = h_ref[r0:r0 + tm, :]
        nb = _rmsnorm(h, gpre_ref[...]).astype(BF16)
        nb_half = nb * 0.5
        for c in range(d_ff // chunk):
            gh = _dot(nb_half, win_ref[:, c * chunk:(c + 1) * chunk])
            u = _dot(nb, win_ref[:, d_ff + c * chunk:d_ff + (c + 1) * chunk])
            hgu = gh * u
            a_ref[:, c * chunk:(c + 1) * chunk] = (hgu + hgu * jnp.tanh(gh)).astype(BF16)
        ff = _dot(a_ref[...], wout_ref[...])
        o_ref[r0:r0 + tm, :] = h + _rmsnorm(ff, gpost_ref[...])


def _ffn(h, gpre, win, wout, gpost, *, tm, chunk):
    t, d = h.shape
    d_ff = wout.shape[0]
    rows = tm * FFN_TILES_PER_STEP
    kern = functools.partial(_ffn_kernel, d_ff=d_ff, chunk=chunk)
    return pl.pallas_call(
        kern,
        grid=(t // rows,),
        in_specs=[pl.BlockSpec((rows, d), lambda i: (i, 0)),
                  _const_spec(gpre.shape), _const_spec(win.shape), _const_spec(wout.shape),
                  _const_spec(gpost.shape)],
        out_specs=pl.BlockSpec((rows, d), lambda i: (i, 0)),
        out_shape=jax.ShapeDtypeStruct((t, d), F32),
        scratch_shapes=[pltpu.VMEM((tm, d_ff), BF16)],
        compiler_params=pltpu.CompilerParams(dimension_semantics=("arbitrary",),
                                             vmem_limit_bytes=VMEM_LIMIT),
        name="ffn",
    )(h, gpre, win, wout, gpost)


def kernel(x, mem, norm_mix_pre, w_in, w_pool_mix, pool_scale, w_pool_o, w_sb_o, norm_mem,
           w_mem_kv, w_x_o, w_out, norm_mix_post, norm_ffn_pre, w_ffn_in, w_ffn_out,
           norm_ffn_post):
    b, s, d = x.shape
    depth = w_in.shape[0]
    pool_w = w_pool_o.shape[1]
    sb_w = w_sb_o.shape[1]
    x_w = w_x_o.shape[1]
    sb_heads = sb_w // HEAD_DIM
    x_heads = x_w // HEAD_DIM
    split = pool_w + 3 * sb_w + x_w

    h = x.reshape(b * s, d)
    for l in range(depth):
        row = lambda v: v[l].reshape(1, -1)
        kv = _mem_kv(mem, row(norm_mem), w_mem_kv[l])
        pin, qv, kt, xq = _in_proj(h, row(norm_mix_pre), w_in[l], tm=TM_IN_PROJ, pool_w=pool_w,
                                   sb_w=sb_w, x_w=x_w)
        ysb = _sb_attn(qv.reshape(b, s, 2 * sb_w), kt, heads=sb_heads).reshape(b * s, sb_w)
        h = _mix(h, pin, ysb, xq, kv, row(norm_mix_pre), w_in[l], w_pool_mix[l], row(pool_scale),
                 w_pool_o[l], w_sb_o[l], w_x_o[l], w_out[l], row(norm_mix_post), tm=TM_MIX, seq=s,
                 x_heads=x_heads, gate_col0=split)
        h = _ffn(h, row(norm_ffn_pre), w_ffn_in[l], w_ffn_out[l], row(norm_ffn_post), tm=TM_FFN,
                 chunk=256)
    return h.reshape(b, s, d)
```
